```python
import math
import jax
import jax.numpy as jnp
from jax import lax
import numpy as np

D_MODEL = 1024
BATCH = 16
SEQ = 4096
DEPTH = 1

GRID_W = 64
CTX_LEN = 256

ATTN_HEADS = 4
ATTN_HEAD_DIM = 64
ATTN_QK_WIDTH = ATTN_HEADS * 2 * ATTN_HEAD_DIM
ATTN_V_WIDTH = ATTN_HEADS * 2 * ATTN_HEAD_DIM
Q_BLOCK = 128
ROPE_BASE = 10000.0

SSM_WIDTH = D_MODEL - ATTN_V_WIDTH
SSM_GROUP = 16
SSM_GROUPS = SSM_WIDTH // SSM_GROUP
SSM_STATE = 64
DT_MIN = 1e-3
DT_MAX = 1e-1

IN_PROJ_WIDTH = 2 * ATTN_QK_WIDTH + ATTN_V_WIDTH + SSM_WIDTH

MOE_GROUPS = 4
EXPERTS_PER_GROUP = 8
N_EXPERTS = MOE_GROUPS * EXPERTS_PER_GROUP
TOP_K_IN_GROUP = 2
EXPERT_FF = D_MODEL // 2

RMS_EPS = 1e-6

kernel_name = 'hybrid_diffattn_s5_hmoe_dit_block'


def rms_norm(x, g):
    xf = x.astype(jnp.float32)
    y = xf * lax.rsqrt(jnp.mean(xf * xf, axis=-1, keepdims=True) + RMS_EPS)
    return (y * g.astype(jnp.float32)).astype(x.dtype)


def modulate(h, shift, scale):
    return h * (1.0 + scale) + shift


def axial_rope_tables(n_tokens):
    rows = n_tokens // GRID_W
    row = jnp.broadcast_to(jnp.arange(rows, dtype=jnp.float32)[:, None], (rows, GRID_W)).reshape(-1)
    col = jnp.broadcast_to(jnp.arange(GRID_W, dtype=jnp.float32)[None, :], (rows, GRID_W)).reshape(-1)
    half = ATTN_HEAD_DIM // 2
    inv = ROPE_BASE ** (-jnp.arange(0, half, 2, dtype=jnp.float32) / half)
    ang = jnp.stack([row[:, None] * inv, col[:, None] * inv], axis=1)
    return jnp.cos(ang), jnp.sin(ang)


def apply_axial_rope(x, cos, sin):
    xr = x.astype(jnp.float32).reshape(*x.shape[:-1], 2, 2, ATTN_HEAD_DIM // 4)
    x1, x2 = xr[..., 0, :], xr[..., 1, :]
    cs = cos[None, :, None, None]
    sn = sin[None, :, None, None]
    out = jnp.stack([x1 * cs - x2 * sn, x2 * cs + x1 * sn], axis=-2)
    return out.reshape(x.shape).astype(x.dtype)


def split_in_proj(p):
    b, n, _ = p.shape
    q = p[..., :ATTN_QK_WIDTH].reshape(b, n, ATTN_HEADS, 2, ATTN_HEAD_DIM)
    k = p[..., ATTN_QK_WIDTH:2 * ATTN_QK_WIDTH].reshape(b, n, ATTN_HEADS, 2, ATTN_HEAD_DIM)
    v = p[..., 2 * ATTN_QK_WIDTH:2 * ATTN_QK_WIDTH + ATTN_V_WIDTH].reshape(b, n, ATTN_HEADS, 2 * ATTN_HEAD_DIM)
    u = p[..., 2 * ATTN_QK_WIDTH + ATTN_V_WIDTH:]
    return q, k, v, u


def diff_lambda(lq1, lk1, lq2, lk2, lam_init):
    e1 = jnp.exp(jnp.sum(lq1.astype(jnp.float32) * lk1.astype(jnp.float32)))
    e2 = jnp.exp(jnp.sum(lq2.astype(jnp.float32) * lk2.astype(jnp.float32)))
    return e1 - e2 + lam_init


def diff_softmax_attend(q, k, v, lam):
    s = jnp.einsum('bqhcd,bkhcd->bhcqk', q, k, preferred_element_type=jnp.float32) * (ATTN_HEAD_DIM ** -0.5)
    p = jax.nn.softmax(s, axis=-1)
    a = p[:, :, 0] - lam * p[:, :, 1]
    return jnp.einsum('bhqk,bkhe->bqhe', a.astype(v.dtype), v)


def diff_attention_latent(q, k_lat, v_lat, k_ctx, v_ctx, lam):
    b, n = q.shape[:2]
    k_all = jnp.concatenate([k_lat, k_ctx], axis=1)
    v_all = jnp.concatenate([v_lat, v_ctx], axis=1)
    nblk = n // Q_BLOCK
    qb = q.reshape(b, nblk, Q_BLOCK, *q.shape[2:]).swapaxes(0, 1)
    ob = lax.map(lambda qi: diff_softmax_attend(qi, k_all, v_all, lam), qb)
    return ob.swapaxes(0, 1).reshape(b, n, ATTN_HEADS, 2 * ATTN_HEAD_DIM)


def diff_head_out(o, subln_g, lam_init):
    o = rms_norm(o, subln_g) * (1.0 - lam_init)
    return o.reshape(*o.shape[:2], ATTN_V_WIDTH)


def cplx_mul(ar, ai, br, bi):
    return ar * br - ai * bi, ar * bi + ai * br


def s5_combine(e_i, e_j):
    a_r, a_i, b_r, b_i = e_i
    c_r, c_i, d_r, d_i = e_j
    ar, ai = cplx_mul(c_r, c_i, a_r, a_i)
    br, bi = cplx_mul(c_r, c_i, b_r, b_i)
    return ar, ai, br + d_r, bi + d_i


def s5_discretise(a_re, a_im, log_dt, b_re, b_im):
    dt = jnp.exp(log_dt.astype(jnp.float32))[:, None]
    ar, ai = a_re.astype(jnp.float32), a_im.astype(jnp.float32)
    mag = jnp.exp(ar * dt)
    lr, li = mag * jnp.cos(ai * dt), mag * jnp.sin(ai * dt)
    den = ar * ar + ai * ai
    nr, ni = lr - 1.0, li
    cr = (nr * ar + ni * ai) / den
    ci = (ni * ar - nr * ai) / den
    bbr, bbi = cplx_mul(cr[..., None], ci[..., None], b_re.astype(jnp.float32), b_im.astype(jnp.float32))
    return lr, li, bbr, bbi


def s5_scan(u, lam_r, lam_i, bb_r, bb_i, h0):
    bu_r = jnp.einsum('gph,bngh->bngp', bb_r, u)
    bu_i = jnp.einsum('gph,bngh->bngp', bb_i, u)
    if h0 is not None:
        ir, ii = cplx_mul(lam_r, lam_i, h0[0], h0[1])
        bu_r = bu_r.at[:, 0].add(ir)
        bu_i = bu_i.at[:, 0].add(ii)
    n = u.shape[1]
    a_r = jnp.broadcast_to(lam_r, (1, n) + lam_r.shape)
    a_i = jnp.broadcast_to(lam_i, (1, n) + lam_i.shape)
    _, _, h_r, h_i = lax.associative_scan(s5_combine, (a_r, a_i, bu_r, bu_i), axis=1)
    return h_r, h_i


def s5_readout(c_re, c_im, h_r, h_i):
    return jnp.einsum('ghp,bngp->bngh', c_re, h_r) - jnp.einsum('ghp,bngp->bngh', c_im, h_i)


def s5_glu(y, w_glu, b_glu, dtype):
    y = jax.nn.gelu(y.reshape(*y.shape[:2], SSM_WIDTH)).astype(dtype)
    return y * jax.nn.sigmoid(y @ w_glu + b_glu)


def s5_mixer(u_lat, u_ctx, a_re, a_im, log_dt, b_re, b_im, c_re, c_im, d_skip, w_glu, b_glu, need_ctx):
    b, n, _ = u_lat.shape
    nc = u_ctx.shape[1]
    ul = u_lat.astype(jnp.float32).reshape(b, n, SSM_GROUPS, SSM_GROUP)
    uc = u_ctx.astype(jnp.float32).reshape(b, nc, SSM_GROUPS, SSM_GROUP)
    cr, ci = c_re.astype(jnp.float32), c_im.astype(jnp.float32)
    dd = d_skip.astype(jnp.float32)
    y_lat = dd * ul
    y_ctx = dd * uc if need_ctx else None
    for direction in range(2):
        lr, li, bbr, bbi = s5_discretise(a_re[direction], a_im[direction], log_dt[direction],
                                         b_re[direction], b_im[direction])
        ul_d = ul if direction == 0 else ul[:, ::-1]
        uc_d = uc if direction == 0 else uc[:, ::-1]
        hc_r, hc_i = s5_scan(uc_d, lr, li, bbr, bbi, None)
        hl_r, hl_i = s5_scan(ul_d, lr, li, bbr, bbi, (hc_r[:, -1], hc_i[:, -1]))
        yl = s5_readout(cr, ci, hl_r, hl_i)
        y_lat = y_lat + (yl if direction == 0 else yl[:, ::-1])
        if need_ctx:
            yc = s5_readout(cr, ci, hc_r, hc_i)
            y_ctx = y_ctx + (yc if direction == 0 else yc[:, ::-1])
    out_lat = s5_glu(y_lat, w_glu, b_glu, u_lat.dtype)
    out_ctx = s5_glu(y_ctx, w_glu, b_glu, u_ctx.dtype) if need_ctx else None
    return out_lat, out_ctx


def hier_moe(h, w_rg, b_rg, w_re, b_re, w_g, w_u, w_d):
    b, n, d = h.shape
    t = h.reshape(b * n, d)
    g_prob = jax.nn.softmax((t @ w_rg + b_rg).astype(jnp.float32), axis=-1)
    p_grp, g_idx = lax.top_k(g_prob, 1)
    e_logit = (t @ w_re + b_re).astype(jnp.float32).reshape(-1, MOE_GROUPS, EXPERTS_PER_GROUP)
    e_logit = jnp.take_along_axis(e_logit, g_idx[:, :, None], axis=1)[:, 0]
    top_v, top_i = lax.top_k(e_logit, TOP_K_IN_GROUP)
    w_slot = jax.nn.softmax(top_v, axis=-1) * p_grp
    e_id = g_idx * EXPERTS_PER_GROUP + top_i
    gates = jnp.einsum('tk,tke->te', w_slot,
                       jax.nn.one_hot(e_id, N_EXPERTS, dtype=jnp.float32)).astype(t.dtype)
    out = jnp.zeros_like(t)
    for e in range(N_EXPERTS):
        hid = jax.nn.silu(t @ w_g[e]) * (t @ w_u[e])
        out = out + gates[:, e:e + 1] * (hid @ w_d[e])
    return out.reshape(b, n, d)


def setup_inputs(seed: int = 0) -> dict:
    key = jax.random.key(seed)
    k = jax.random.split(key, 34)

    def nrm(i, shape, scale):
        return scale * jax.random.normal(k[i], shape, jnp.float32)

    D, F = D_MODEL, EXPERT_FF
    G, P, HS = SSM_GROUPS, SSM_STATE, SSM_GROUP
    d = ATTN_HEAD_DIM
    n_idx = jnp.arange(P, dtype=jnp.float32)
    return {
        'x': nrm(0, (BATCH, SEQ, D), 1.0),
        'c': nrm(1, (BATCH, D), 1.0),
        'ctx': nrm(2, (BATCH, CTX_LEN, D), 1.0),
        'c_ctx': nrm(3, (D,), 1.0),
        'w_ada': nrm(4, (DEPTH, D, 6 * D), 0.5 * D ** -0.5),
        'b_ada': nrm(5, (DEPTH, 6 * D), 0.01),
        'norm1_g': 1.0 + nrm(6, (DEPTH, D), 0.02),
        'w_in': nrm(7, (DEPTH, D, IN_PROJ_WIDTH), D ** -0.5),
        'q_norm_g': 1.0 + nrm(8, (DEPTH, d), 0.02),
        'k_norm_g': 1.0 + nrm(9, (DEPTH, d), 0.02),
        'lambda_q1': nrm(10, (DEPTH, d), 0.1),
        'lambda_k1': nrm(11, (DEPTH, d), 0.1),
        'lambda_q2': nrm(12, (DEPTH, d), 0.1),
        'lambda_k2': nrm(13, (DEPTH, d), 0.1),
        'subln_g': 1.0 + nrm(14, (DEPTH, 2 * d), 0.02),
        'ssm_a_re': -0.5 + nrm(15, (DEPTH, 2, G, P), 0.01),
        'ssm_a_im': math.pi * n_idx * (1.0 + nrm(16, (DEPTH, 2, G, P), 0.01)),
        'ssm_log_dt': jax.random.uniform(k[17], (DEPTH, 2, G), jnp.float32, math.log(DT_MIN), math.log(DT_MAX)),
        'ssm_b_re': nrm(18, (DEPTH, 2, G, P, HS), (2 * HS) ** -0.5),
        'ssm_b_im': nrm(19, (DEPTH, 2, G, P, HS), (2 * HS) ** -0.5),
        'ssm_c_re': nrm(20, (DEPTH, G, HS, P), P ** -0.5),
        'ssm_c_im': nrm(21, (DEPTH, G, HS, P), P ** -0.5),
        'ssm_d': nrm(22, (DEPTH, G, HS), 1.0),
        'w_glu': nrm(23, (DEPTH, SSM_WIDTH, SSM_WIDTH), SSM_WIDTH ** -0.5),
        'b_glu': nrm(24, (DEPTH, SSM_WIDTH), 0.01),
        'w_out': nrm(25, (DEPTH, D, D), D ** -0.5),
        'norm2_g': 1.0 + nrm(26, (DEPTH, D), 0.02),
        'w_route_group': nrm(27, (DEPTH, D, MOE_GROUPS), D ** -0.5),
        'b_route_group': nrm(28, (DEPTH, MOE_GROUPS), 0.01),
        'w_route_expert': nrm(29, (DEPTH, D, N_EXPERTS), D ** -0.5),
        'b_route_expert': nrm(30, (DEPTH, N_EXPERTS), 0.01),
        'w_exp_gate': nrm(31, (DEPTH, N_EXPERTS, D, F), D ** -0.5),
        'w_exp_up': nrm(32, (DEPTH, N_EXPERTS, D, F), D ** -0.5),
        'w_exp_down': nrm(33, (DEPTH, N_EXPERTS, F, D), F ** -0.5),
    }


def reference(x, c, ctx, c_ctx, w_ada, b_ada, norm1_g, w_in, q_norm_g, k_norm_g,
              lambda_q1, lambda_k1, lambda_q2, lambda_k2, subln_g,
              ssm_a_re, ssm_a_im, ssm_log_dt, ssm_b_re, ssm_b_im, ssm_c_re, ssm_c_im, ssm_d,
              w_glu, b_glu, w_out, norm2_g,
              w_route_group, b_route_group, w_route_expert, b_route_expert,
              w_exp_gate, w_exp_up, w_exp_down):
    n_lat = x.shape[1]
    cos, sin = axial_rope_tables(n_lat)
    h_ctx = ctx
    for l in range(DEPTH):
        last = l == DEPTH - 1
        lam_init = 0.8 - 0.6 * math.exp(-0.3 * l)
        mod = jax.nn.silu(c) @ w_ada[l] + b_ada[l]
        mod_c = jax.nn.silu(c_ctx) @ w_ada[l] + b_ada[l]
        sh1, sc1, g1, sh2, sc2, g2 = jnp.split(mod[:, None, :], 6, axis=-1)
        csh1, csc1, cg1, csh2, csc2, cg2 = jnp.split(mod_c, 6, axis=-1)

        px = modulate(rms_norm(x, norm1_g[l]), sh1, sc1) @ w_in[l]
        pc = modulate(rms_norm(h_ctx, norm1_g[l]), csh1, csc1) @ w_in[l]
        q_x, k_x, v_x, u_x = split_in_proj(px)
        q_c, k_c, v_c, u_c = split_in_proj(pc)
        q_x = apply_axial_rope(rms_norm(q_x, q_norm_g[l]), cos, sin)
        k_x = apply_axial_rope(rms_norm(k_x, k_norm_g[l]), cos, sin)
        k_c = rms_norm(k_c, k_norm_g[l])
        lam = diff_lambda(lambda_q1[l], lambda_k1[l], lambda_q2[l], lambda_k2[l], lam_init)
        a_x = diff_head_out(diff_attention_latent(q_x, k_x, v_x, k_c, v_c, lam), subln_g[l], lam_init)
        s_x, s_c = s5_mixer(u_x, u_c, ssm_a_re[l], ssm_a_im[l], ssm_log_dt[l], ssm_b_re[l], ssm_b_im[l],
                            ssm_c_re[l], ssm_c_im[l], ssm_d[l], w_glu[l], b_glu[l], not last)
        x = x + g1 * (jnp.concatenate([a_x, s_x], axis=-1) @ w_out[l])
        if not last:
            q_c = rms_norm(q_c, q_norm_g[l])
            a_c = diff_head_out(diff_softmax_attend(q_c, k_c, v_c, lam), subln_g[l], lam_init)
            h_ctx = h_ctx + cg1 * (jnp.concatenate([a_c, s_c], axis=-1) @ w_out[l])

        x = x + g2 * hier_moe(modulate(rms_norm(x, norm2_g[l]), sh2, sc2),
                              w_route_group[l], b_route_group[l], w_route_expert[l], b_route_expert[l],
                              w_exp_gate[l], w_exp_up[l], w_exp_down[l])
        if not last:
            h_ctx = h_ctx + cg2 * hier_moe(modulate(rms_norm(h_ctx, norm2_g[l]), csh2, csc2),
                                           w_route_group[l], b_route_group[l], w_route_expert[l],
                                           b_route_expert[l], w_exp_gate[l], w_exp_up[l], w_exp_down[l])
    return x
```

```python
import functools
import math

import jax
import jax.numpy as jnp
from jax import lax
from jax.experimental import pallas as pl
from jax.experimental.pallas import tpu as pltpu

F32 = jnp.float32
BF16 = jnp.bfloat16

LANES = 128
HEADS = 4
HEAD_DIM = 64
QK_WIDTH = HEADS * 2 * HEAD_DIM
V_WIDTH = HEADS * 2 * HEAD_DIM
GRID_W = 64
ROPE_BASE = 10000.0
SSM_GROUP = 16
SSM_STATE = 64
CHUNK = 16
MOE_GROUPS = 4
EXPERTS_PER_GROUP = 8
N_EXPERTS = MOE_GROUPS * EXPERTS_PER_GROUP
RMS_EPS = 1e-6
VMEM_LIMIT = 48 * 1024 * 1024


def _cparams(sem):
    return pltpu.CompilerParams(dimension_semantics=sem, vmem_limit_bytes=VMEM_LIMIT)


def _split_bf16(a):
    hi = a.astype(BF16)
    lo = (a - hi.astype(F32)).astype(BF16)
    return hi, lo


def _dot(a, b):
    return jnp.dot(a, b, preferred_element_type=F32)


def _dot3(a, b):
    a_hi, a_lo = _split_bf16(a)
    b_hi, b_lo = _split_bf16(b)
    return _dot(a_hi, b_hi) + (_dot(a_hi, b_lo) + _dot(a_lo, b_hi))


def _mod_kernel(c_ref, w_ref, b_ref, o_ref):
    c = c_ref[...]
    a = c * jax.nn.sigmoid(c)
    o_ref[...] = _dot3(a, w_ref[...]) + b_ref[...]


def _mod_call(cc, w_ada, b_ada):
    rows, d = cc.shape
    n = w_ada.shape[1]
    bn = 1024
    return pl.pallas_call(
        _mod_kernel,
        out_shape=jax.ShapeDtypeStruct((rows, n), F32),
        grid=(n // bn,),
        in_specs=[pl.BlockSpec((rows, d), lambda j: (0, 0)),
                  pl.BlockSpec((d, bn), lambda j: (0, j)),
                  pl.BlockSpec((1, bn), lambda j: (0, j))],
        out_specs=pl.BlockSpec((rows, bn), lambda j: (0, j)),
        compiler_params=_cparams(("arbitrary",)),
        name="mod",
    )(cc, w_ada, b_ada.reshape(1, n))


def _inproj_kernel(x_ref, sh_ref, sc_ref, g_ref, w_ref, qg_ref, kg_ref, cos_ref, sin_ref, bd_ref,
                   q_ref, k_ref, v_ref, u_ref):
    x = x_ref[0]
    ms = jnp.mean(x * x, axis=-1, keepdims=True)
    h = (x * lax.rsqrt(ms + RMS_EPS)) * g_ref[...]
    h = h * (1.0 + sc_ref[0]) + sh_ref[0]
    p = _dot(h.astype(BF16), w_ref[...])

    cosf = cos_ref[...]
    sinf = sin_ref[...]
    lane = lax.broadcasted_iota(jnp.int32, cosf.shape, 1)
    first_half = (lane % 32) < 16

    def norm_rope(t, gain, scale):
        ss = _dot((t * t).astype(BF16), bd_ref[...])
        t = t * lax.rsqrt(ss * (1.0 / HEAD_DIM) + RMS_EPS)
        outs = []
        for s in range(QK_WIDTH // LANES):
            ts = t[:, s * LANES:(s + 1) * LANES] * gain
            partner = jnp.where(first_half, pltpu.roll(ts, LANES - 16, 1), pltpu.roll(ts, 16, 1))
            outs.append(((ts * cosf + partner * sinf) * scale).astype(BF16))
        return jnp.concatenate(outs, axis=1)

    q_ref[0] = norm_rope(p[:, :QK_WIDTH], qg_ref[...], HEAD_DIM ** -0.5)
    k_ref[0] = norm_rope(p[:, QK_WIDTH:2 * QK_WIDTH], kg_ref[...], 1.0)
    v_ref[0] = p[:, 2 * QK_WIDTH:2 * QK_WIDTH + V_WIDTH].astype(BF16)
    u_ref[0] = p[:, 2 * QK_WIDTH + V_WIDTH:].astype(BF16)


def _inproj_call(x, sh, sc, g, w_bf, qg, kg, cosf, sinf, bd, tm, name):
    b, n, d = x.shape
    wn = w_bf.shape[1]
    per_batch = sh.shape[0] > 1
    mod_map = (lambda bi, i: (bi, 0, 0)) if per_batch else (lambda bi, i: (0, 0, 0))
    const2 = lambda bi, i: (0, 0)
    outs = [jax.ShapeDtypeStruct((b, n, QK_WIDTH), BF16)] * 4
    tok_spec = pl.BlockSpec((1, tm, QK_WIDTH), lambda bi, i: (bi, i, 0))
    return pl.pallas_call(
        _inproj_kernel,
        out_shape=outs,
        grid=(b, n // tm),
        in_specs=[pl.BlockSpec((1, tm, d), lambda bi, i: (bi, i, 0)),
                  pl.BlockSpec((1, 1, d), mod_map),
                  pl.BlockSpec((1, 1, d), mod_map),
                  pl.BlockSpec((1, d), const2),
                  pl.BlockSpec((d, wn), const2),
                  pl.BlockSpec((1, LANES), const2),
                  pl.BlockSpec((1, LANES), const2),
                  pl.BlockSpec((tm, LANES), lambda bi, i: (i, 0)),
                  pl.BlockSpec((tm, LANES), lambda bi, i: (i, 0)),
                  pl.BlockSpec((QK_WIDTH, QK_WIDTH), const2)],
        out_specs=[tok_spec] * 4,
        compiler_params=_cparams(("parallel", "parallel")),
        name=name,
    )(x, sh, sc, g, w_bf, qg, kg, cosf, sinf, bd)


def _attn_kernel(lam_ref, q_ref, kl_ref, vl_ref, kc_ref, vc_ref, sg_ref, o_ref,
                 m1_ref, l1_ref, a1_ref, m2_ref, l2_ref, a2_ref, *, tk, out_scale):
    q = q_ref[0]
    lane = lax.broadcasted_iota(jnp.int32, q.shape, 1)
    zero = jnp.zeros_like(q)
    qa = jnp.where(lane < HEAD_DIM, q, zero)
    qb = jnp.where(lane >= HEAD_DIM, q, zero)

    for r in (m1_ref, m2_ref):
        r[...] = jnp.full(r.shape, -jnp.inf, F32)
    for r in (l1_ref, l2_ref, a1_ref, a2_ref):
        r[...] = jnp.zeros(r.shape, F32)

    def one_map(qm, kc, vc, m_ref, l_ref, a_ref):
        s = lax.dot_general(qm, kc, (((1,), (1,)), ((), ())), preferred_element_type=F32)
        m_prev = m_ref[...]
        m_next = jnp.maximum(m_prev, jnp.max(s, axis=1, keepdims=True))
        p = jnp.exp(s - m_next[:, :1])
        alpha = jnp.exp(m_prev - m_next)
        l_ref[...] = alpha * l_ref[...] + jnp.sum(p, axis=1, keepdims=True)
        a_ref[...] = alpha * a_ref[...] + _dot(p.astype(BF16), vc)
        m_ref[...] = m_next

    def step(kc, vc):
        one_map(qa, kc, vc, m1_ref, l1_ref, a1_ref)
        one_map(qb, kc, vc, m2_ref, l2_ref, a2_ref)

    def lat_body(j, carry):
        off = pl.multiple_of(j * tk, tk)
        step(kl_ref[0, pl.ds(off, tk), :], vl_ref[0, pl.ds(off, tk), :])
        return carry

    lax.fori_loop(0, kl_ref.shape[1] // tk, lat_body, 0)
    step(kc_ref[0], vc_ref[0])

    lam = lam_ref[...]
    o = a1_ref[...] / l1_ref[...] - lam * (a2_ref[...] / l2_ref[...])
    ms = jnp.mean(o * o, axis=-1, keepdims=True)
    o = o * lax.rsqrt(ms + RMS_EPS) * sg_ref[...]
    o_ref[0] = (o * out_scale).astype(BF16)


def _attn_call(lam_row, q, k_lat, v_lat, k_ctx, v_ctx, sg, tq, tk, out_scale):
    b, n, _ = q.shape
    nc = k_ctx.shape[1]
    kv_lat = pl.BlockSpec((1, n, LANES), lambda bi, h, i: (bi, 0, h))
    kv_ctx = pl.BlockSpec((1, nc, LANES), lambda bi, h, i: (bi, 0, h))
    q_spec = pl.BlockSpec((1, tq, LANES), lambda bi, h, i: (bi, i, h))
    row = pl.BlockSpec((1, LANES), lambda bi, h, i: (0, 0))
    acc = pltpu.VMEM((tq, LANES), F32)
    return pl.pallas_call(
        functools.partial(_attn_kernel, tk=tk, out_scale=out_scale),
        out_shape=jax.ShapeDtypeStruct((b, n, V_WIDTH), BF16),
        grid=(b, HEADS, n // tq),
        in_specs=[row, q_spec, kv_lat, kv_lat, kv_ctx, kv_ctx, row],
        out_specs=q_spec,
        scratch_shapes=[acc] * 6,
        compiler_params=_cparams(("parallel", "parallel", "arbitrary")),
        name="attn",
    )(lam_row, q, k_lat, v_lat, k_ctx, v_ctx, sg)


def _s5_kernel(ul_ref, uc_ref, win_ref, m_ref, wo_ref, lam_ref, o_ref, xl_ref, xc_ref, s_ref, *, nb):
    ul = ul_ref[0]
    xl_ref[...] = _dot(ul, win_ref[0])
    xc_ref[...] = _dot(uc_ref[0], win_ref[0])
    n_lat = ul.shape[0] // nb
    n_ctx = uc_ref.shape[1] // nb

    lam = lam_ref[0]
    lfr, lfi, lbr, lbi = (jnp.broadcast_to(lam[i:i + 1], (nb, LANES)) for i in range(4))

    def rows(c):
        return pl.ds(pl.multiple_of(c * nb, nb), nb)

    def advance(x_ref, c, lane0, ar, ai, sr, si):
        xr = x_ref[rows(c), lane0:lane0 + LANES]
        xi = x_ref[rows(c), lane0 + LANES:lane0 + 2 * LANES]
        return ar * sr - ai * si + xr, ar * si + ai * sr + xi

    def ctx_body(i, carry):
        fr, fi, br, bi = carry
        fr, fi = advance(xc_ref, i, 0, lfr, lfi, fr, fi)
        br, bi = advance(xc_ref, n_ctx - 1 - i, 2 * LANES, lbr, lbi, br, bi)
        return fr, fi, br, bi

    def lat_body(i, carry):
        fr, fi, br, bi = carry
        cb = n_lat - 1 - i
        s_ref[rows(i), 0:LANES] = fr.astype(BF16)
        s_ref[rows(i), LANES:2 * LANES] = fi.astype(BF16)
        s_ref[rows(cb), 2 * LANES:3 * LANES] = br.astype(BF16)
        s_ref[rows(cb), 3 * LANES:4 * LANES] = bi.astype(BF16)
        fr, fi = advance(xl_ref, i, 0, lfr, lfi, fr, fi)
        br, bi = advance(xl_ref, cb, 2 * LANES, lbr, lbi, br, bi)
        return fr, fi, br, bi

    z = jnp.zeros((nb, LANES), F32)
    carry = lax.fori_loop(0, n_ctx, ctx_body, (z, z, z, z))
    lax.fori_loop(0, n_lat, lat_body, carry)

    y = _dot(ul, m_ref[0]) + _dot(s_ref[...], wo_ref[0])
    o_ref[0] = jax.nn.gelu(y).astype(BF16)


def _s5_call(u_lat, u_ctx, win, m, wo, lam16, nb):
    npair, rl, kw = u_lat.shape
    rc = u_ctx.shape[1]
    wspec = pl.BlockSpec((1, kw, kw), lambda p: (p, 0, 0))
    return pl.pallas_call(
        functools.partial(_s5_kernel, nb=nb),
        out_shape=jax.ShapeDtypeStruct((npair, rl, kw), BF16),
        grid=(npair,),
        in_specs=[pl.BlockSpec((1, rl, kw), lambda p: (p, 0, 0)),
                  pl.BlockSpec((1, rc, kw), lambda p: (p, 0, 0)),
                  wspec, wspec, wspec,
                  pl.BlockSpec((1, 4, LANES), lambda p: (p, 0, 0))],
        out_specs=pl.BlockSpec((1, rl, kw), lambda p: (p, 0, 0)),
        scratch_shapes=[pltpu.VMEM((rl, kw), F32), pltpu.VMEM((rc, kw), F32), pltpu.VMEM((rl, kw), BF16)],
        compiler_params=_cparams(("parallel",)),
        name="s5",
    )(u_lat, u_ctx, win, m, wo, lam16)


def _s5_weights(a_re, a_im, log_dt, b_re, b_im, c_re, c_im, d_skip):
    hp = lax.Precision.HIGHEST
    g_n, p_n = a_re.shape[1], a_re.shape[2]
    t_n = CHUNK
    dt = jnp.exp(log_dt.astype(F32))[..., None]
    ar, ai = a_re.astype(F32), a_im.astype(F32)
    mag = jnp.exp(ar * dt)
    lr, li = mag * jnp.cos(ai * dt), mag * jnp.sin(ai * dt)
    den = ar * ar + ai * ai
    nr, ni = lr - 1.0, li
    cr = (nr * ar + ni * ai) / den
    ci = (ni * ar - nr * ai) / den
    bbr = cr[..., None] * b_re - ci[..., None] * b_im
    bbi = cr[..., None] * b_im + ci[..., None] * b_re
    n = jnp.arange(t_n + 1, dtype=F32)[:, None, None, None]
    pm = jnp.exp(n * (ar * dt))
    pw_r, pw_i = pm * jnp.cos(n * (ai * dt)), pm * jnp.sin(n * (ai * dt))
    lb_r = pw_r[:t_n, ..., None] * bbr - pw_i[:t_n, ..., None] * bbi
    lb_i = pw_r[:t_n, ..., None] * bbi + pw_i[:t_n, ..., None] * bbr
    cre, cim = c_re.astype(F32), c_im.astype(F32)
    kern = (jnp.einsum('gip,tdgpj->tdgij', cre, lb_r, precision=hp)
            - jnp.einsum('gip,tdgpj->tdgij', cim, lb_i, precision=hp))
    s_idx = jnp.arange(t_n)[:, None]
    t_idx = jnp.arange(t_n)[None, :]
    tau = t_idx - s_idx
    mf = jnp.where((tau >= 0)[..., None, None, None], kern[jnp.clip(tau, 0, t_n - 1), 0], 0.0)
    mb = jnp.where((tau <= 0)[..., None, None, None], kern[jnp.clip(-tau, 0, t_n - 1), 1], 0.0)
    hh = SSM_GROUP
    skip = (jnp.eye(t_n, dtype=F32)[:, :, None, None, None] * jnp.eye(hh, dtype=F32)[None, None, None]
            * d_skip.astype(F32)[None, None, :, :, None])
    m_g = jnp.transpose(mf + mb + skip, (2, 0, 4, 1, 3))

    def to_rows(w):
        return jnp.transpose(w, (1, 0, 3, 2))

    win4 = jnp.stack([to_rows(lb_r[::-1, 0]), to_rows(lb_i[::-1, 0]),
                      to_rows(lb_r[:, 1]), to_rows(lb_i[:, 1])], axis=1)

    def to_cols(zr, zi):
        o_re = cre[None] * zr[:, :, None, :] - cim[None] * zi[:, :, None, :]
        o_im = -(cre[None] * zi[:, :, None, :] + cim[None] * zr[:, :, None, :])
        f = lambda o: jnp.transpose(o, (1, 3, 0, 2))
        return f(o_re), f(o_im)

    of_re, of_im = to_cols(pw_r[1:, 0], pw_i[1:, 0])
    ob_re, ob_im = to_cols(pw_r[1:, 1][::-1], pw_i[1:, 1][::-1])
    wo4 = jnp.stack([of_re, of_im, ob_re, ob_im], axis=1)

    npair = g_n // 2
    eye2 = jnp.eye(2, dtype=F32)
    kw = 2 * t_n * hh
    m_pair = jnp.einsum('ab,pasjti->psajtbi', eye2,
                        m_g.reshape(npair, 2, t_n, hh, t_n, hh)).reshape(npair, kw, kw)
    win_pair = jnp.einsum('ab,paksjq->psajkbq', eye2,
                          win4.reshape(npair, 2, 4, t_n, hh, p_n)).reshape(npair, kw, 8 * p_n)
    wo_pair = jnp.einsum('ab,pakqti->pkaqtbi', eye2,
                         wo4.reshape(npair, 2, 4, p_n, t_n, hh)).reshape(npair, 8 * p_n, kw)
    lam16 = jnp.stack([pw_r[t_n, 0], pw_i[t_n, 0], pw_r[t_n, 1], pw_i[t_n, 1]], axis=1)
    lam16 = jnp.transpose(lam16.reshape(npair, 2, 4, p_n), (0, 2, 1, 3)).reshape(npair, 4, 2 * p_n)
    return win_pair.astype(BF16), m_pair.astype(BF16), wo_pair.astype(BF16), lam16


def _to_chunk_rows(u):
    b, n, w = u.shape
    npair = w // (2 * SSM_GROUP)
    u5 = u.reshape(b, n // CHUNK, CHUNK, npair, 2 * SSM_GROUP)
    return jnp.transpose(u5, (3, 1, 0, 2, 4)).reshape(npair, (n // CHUNK) * b, CHUNK * 2 * SSM_GROUP)


def _from_chunk_rows(y, b):
    npair, r, kw = y.shape
    nchunk = r // b
    y5 = y.reshape(npair, nchunk, b, CHUNK, 2 * SSM_GROUP)
    return jnp.transpose(y5, (2, 1, 3, 0, 4)).reshape(b, nchunk * CHUNK, npair * 2 * SSM_GROUP)


def _outproj_kernel(a_ref, y_ref, x_ref, g1_ref, sh_ref, sc_ref, n2_ref, wglu_ref, bglu_ref, wout_ref,
                    wr_hi_ref, wr_lo_ref, br_ref, x1_ref, h2_ref, gate_ref):
    y = y_ref[0]
    z = _dot(y, wglu_ref[...]) + bglu_ref[...]
    s = (y.astype(F32) * jax.nn.sigmoid(z)).astype(BF16)
    half = a_ref.shape[2]
    o = _dot(a_ref[0], wout_ref[:half, :]) + _dot(s, wout_ref[half:, :])
    x1 = x_ref[0] + g1_ref[0] * o
    x1_ref[0] = x1

    ms = jnp.mean(x1 * x1, axis=-1, keepdims=True)
    h = (x1 * lax.rsqrt(ms + RMS_EPS)) * n2_ref[...]
    h = h * (1.0 + sc_ref[0]) + sh_ref[0]
    h2_ref[0] = h.astype(BF16)

    h_hi, h_lo = _split_bf16(h)
    lg = _dot(h_hi, wr_hi_ref[...]) + (_dot(h_hi, wr_lo_ref[...]) + _dot(h_lo, wr_hi_ref[...])) + br_ref[...]
    lane = lax.broadcasted_iota(jnp.int32, lg.shape, 1)
    neg = jnp.float32(-jnp.inf)
    big = jnp.int32(LANES)

    def top1(vals):
        vmax = jnp.max(vals, axis=1, keepdims=True)
        idx = jnp.min(jnp.where(vals == vmax, lane, big), axis=1, keepdims=True)
        return vmax, idx

    is_grp = lane < MOE_GROUPS
    g_vals = jnp.where(is_grp, lg, neg)
    g_max, g_idx = top1(g_vals)
    p_grp = 1.0 / jnp.sum(jnp.where(is_grp, jnp.exp(g_vals - g_max), 0.0), axis=1, keepdims=True)
    e_lo = MOE_GROUPS + EXPERTS_PER_GROUP * g_idx
    in_grp = (lane >= e_lo) & (lane < e_lo + EXPERTS_PER_GROUP)
    e_vals = jnp.where(in_grp, lg, neg)
    v1, i1 = top1(e_vals)
    v2, i2 = top1(jnp.where(lane == i1, neg, e_vals))
    r = jnp.exp(v2 - v1)
    w1 = p_grp / (1.0 + r)
    w2 = w1 * r
    gate = jnp.where(lane == i1, w1, 0.0) + jnp.where(lane == i2, w2, 0.0)
    gate_ref[0] = gate


def _outproj_call(a, yg, x, g1, sh2, sc2, n2g, wglu, bglu, wout, wr_hi, wr_lo, br, tm):
    b, n, d = x.shape
    half = a.shape[2]
    tok = lambda w: pl.BlockSpec((1, tm, w), lambda bi, i: (bi, i, 0))
    mod = pl.BlockSpec((1, 1, d), lambda bi, i: (bi, 0, 0))
    const = lambda r, c: pl.BlockSpec((r, c), lambda bi, i: (0, 0))
    return pl.pallas_call(
        _outproj_kernel,
        out_shape=[jax.ShapeDtypeStruct((b, n, d), F32), jax.ShapeDtypeStruct((b, n, d), BF16),
                   jax.ShapeDtypeStruct((b, n, LANES), F32)],
        grid=(b, n // tm),
        in_specs=[tok(half), tok(half), tok(d), mod, mod, mod, const(1, d),
                  const(half, half), const(1, half), const(d, d),
                  const(d, LANES), const(d, LANES), const(1, LANES)],
        out_specs=[tok(d), tok(d), tok(LANES)],
        compiler_params=_cparams(("parallel", "parallel")),
        name="outproj",
    )(a, yg, x, g1, sh2, sc2, n2g, wglu, bglu, wout, wr_hi, wr_lo, br)


def _moe_dense_kernel(h_ref, gate_ref, x1_ref, g2_ref, wg_ref, wu_ref, wd_ref, o_ref, acc_ref):
    e = pl.program_id(2)

    @pl.when(e == 0)
    def _():
        acc_ref[...] = jnp.zeros(acc_ref.shape, F32)

    h = h_ref[0]
    hid = jax.nn.silu(_dot(h, wg_ref[0])) * _dot(h, wu_ref[0])
    gate = gate_ref[0]
    lane = lax.broadcasted_iota(jnp.int32, gate.shape, 1)
    ge = jnp.sum(jnp.where(lane == MOE_GROUPS + e, gate, 0.0), axis=1, keepdims=True)
    acc_ref[...] += ge * _dot(hid.astype(BF16), wd_ref[0])

    @pl.when(e == pl.num_programs(2) - 1)
    def _():
        o_ref[0] = x1_ref[0] + g2_ref[0] * acc_ref[...]


def _moe_dense_call(h2, gate, x1, g2, wg, wu, wd, tm):
    b, n, d = x1.shape
    ne, _, f = wg.shape
    tok = lambda w: pl.BlockSpec((1, tm, w), lambda bi, i, e: (bi, i, 0))
    return pl.pallas_call(
        _moe_dense_kernel,
        out_shape=jax.ShapeDtypeStruct((b, n, d), F32),
        grid=(b, n // tm, ne),
        in_specs=[tok(d), tok(LANES), tok(d),
                  pl.BlockSpec((1, 1, d), lambda bi, i, e: (bi, 0, 0)),
                  pl.BlockSpec((1, d, f), lambda bi, i, e: (e, 0, 0)),
                  pl.BlockSpec((1, d, f), lambda bi, i, e: (e, 0, 0)),
                  pl.BlockSpec((1, f, d), lambda bi, i, e: (e, 0, 0))],
        out_specs=tok(d),
        scratch_shapes=[pltpu.VMEM((tm, d), F32)],
        compiler_params=_cparams(("parallel", "parallel", "arbitrary")),
        name="moe",
    )(h2, gate, x1, g2, wg, wu, wd)


def _rope_tables(n_tokens):
    rows = n_tokens // GRID_W
    row = jnp.broadcast_to(jnp.arange(rows, dtype=F32)[:, None], (rows, GRID_W)).reshape(-1)
    col = jnp.broadcast_to(jnp.arange(GRID_W, dtype=F32)[None, :], (rows, GRID_W)).reshape(-1)
    half = HEAD_DIM // 2
    inv = ROPE_BASE ** (-jnp.arange(0, half, 2, dtype=F32) / half)
    ang = jnp.stack([row[:, None] * inv, col[:, None] * inv], axis=1)
    cos, sin = jnp.cos(ang), jnp.sin(ang)
    cos64 = jnp.concatenate([cos[:, 0], cos[:, 0], cos[:, 1], cos[:, 1]], axis=1)
    sin64 = jnp.concatenate([-sin[:, 0], sin[:, 0], -sin[:, 1], sin[:, 1]], axis=1)
    return jnp.tile(cos64, (1, LANES // HEAD_DIM)), jnp.tile(sin64, (1, LANES // HEAD_DIM))


def _pick_tile(n, target):
    t = min(n, target)
    while n % t:
        t //= 2
    return t


def kernel(x, c, ctx, c_ctx, w_ada, b_ada, norm1_g, w_in, q_norm_g, k_norm_g, lambda_q1, lambda_k1, lambda_q2, lambda_k2, subln_g, ssm_a_re, ssm_a_im, ssm_log_dt, ssm_b_re, ssm_b_im, ssm_c_re, ssm_c_im, ssm_d, w_glu, b_glu, w_out, norm2_g, w_route_group, b_route_group, w_route_expert, b_route_expert, w_exp_gate, w_exp_up, w_exp_down):
    depth = w_ada.shape[0]
    assert depth == 1, "single-layer block: the context stream is never updated"
    b, n_lat, d = x.shape
    n_ctx = ctx.shape[1]
    assert n_lat % CHUNK == 0 and n_ctx % CHUNK == 0 and n_lat % GRID_W == 0
    l = 0
    lam_init = 0.8 - 0.6 * math.exp(-0.3 * l)

    rows = b + 1
    rows_pad = -(-rows // 8) * 8
    cc = jnp.concatenate([c, c_ctx[None, :], jnp.zeros((rows_pad - rows, d), F32)], axis=0)
    mod = _mod_call(cc, w_ada[l], b_ada[l])
    sh1, sc1, g1, sh2, sc2, g2 = (mod[:b, i * d:(i + 1) * d].reshape(b, 1, d) for i in range(6))
    csh1, csc1 = (mod[b:b + 1, i * d:(i + 1) * d].reshape(1, 1, d) for i in range(2))

    w_in_bf = w_in[l].astype(BF16)
    bd = jnp.kron(jnp.eye(QK_WIDTH // HEAD_DIM, dtype=F32), jnp.ones((HEAD_DIM, HEAD_DIM), F32)).astype(BF16)
    qg = jnp.tile(q_norm_g[l], LANES // HEAD_DIM).reshape(1, LANES)
    kg = jnp.tile(k_norm_g[l], LANES // HEAD_DIM).reshape(1, LANES)
    cosf, sinf = _rope_tables(n_lat)
    ones_c, zeros_c = jnp.ones((n_ctx, LANES), F32), jnp.zeros((n_ctx, LANES), F32)
    g1n = norm1_g[l].reshape(1, d)
    tm = _pick_tile(n_lat, 512)
    q_x, k_x, v_x, u_x = _inproj_call(x, sh1, sc1, g1n, w_in_bf, qg, kg, cosf, sinf, bd, tm, "inproj_lat")
    _, k_c, v_c, u_c = _inproj_call(ctx, csh1, csc1, g1n, w_in_bf, qg, kg, ones_c, zeros_c, bd,
                                    _pick_tile(n_ctx, 512), "inproj_ctx")

    e1 = jnp.exp(jnp.sum(lambda_q1[l] * lambda_k1[l]))
    e2 = jnp.exp(jnp.sum(lambda_q2[l] * lambda_k2[l]))
    lam_row = jnp.full((1, LANES), e1 - e2 + lam_init, F32)
    a_x = _attn_call(lam_row, q_x, k_x, v_x, k_c, v_c, subln_g[l].reshape(1, LANES),
                     _pick_tile(n_lat, 512), _pick_tile(n_lat, 512), 1.0 - lam_init)

    win, m_op, wo, lam16 = _s5_weights(ssm_a_re[l], ssm_a_im[l], ssm_log_dt[l], ssm_b_re[l], ssm_b_im[l],
                                       ssm_c_re[l], ssm_c_im[l], ssm_d[l])
    yg = _s5_call(_to_chunk_rows(u_x), _to_chunk_rows(u_c), win, m_op, wo, lam16, b)
    yg = _from_chunk_rows(yg, b)

    wr = jnp.concatenate([w_route_group[l], w_route_expert[l]], axis=1)
    wr = jnp.pad(wr, ((0, 0), (0, LANES - wr.shape[1])))
    wr_hi, wr_lo = _split_bf16(wr)
    br = jnp.pad(jnp.concatenate([b_route_group[l], b_route_expert[l]]), (0, LANES - MOE_GROUPS - N_EXPERTS))
    x1, h2, gate = _outproj_call(a_x, yg, x, g1, sh2, sc2, norm2_g[l].reshape(1, d),
                                 w_glu[l].astype(BF16), b_glu[l].reshape(1, -1), w_out[l].astype(BF16),
                                 wr_hi, wr_lo, br.reshape(1, LANES), tm)

    return _moe_dense_call(h2, gate, x1, g2, w_exp_gate[l].astype(BF16), w_exp_up[l].astype(BF16),
                           w_exp_down[l].astype(BF16), _pick_tile(n_lat, 1024))
```

```python
import functools
import math

import jax
import jax.numpy as jnp
from jax import lax
from jax.experimental import pallas as pl
from jax.experimental.pallas import tpu as pltpu

F32 = jnp.float32
BF16 = jnp.bfloat16

LANES = 128
HEADS = 4
HEAD_DIM = 64
QK_WIDTH = HEADS * 2 * HEAD_DIM
V_WIDTH = HEADS * 2 * HEAD_DIM
GRID_W = 64
ROPE_BASE = 10000.0
SSM_GROUP = 16
SSM_STATE = 64
CHUNK = 16
MOE_GROUPS = 4
EXPERTS_PER_GROUP = 8
N_EXPERTS = MOE_GROUPS * EXPERTS_PER_GROUP
RMS_EPS = 1e-6
MOE_TILE = 512
COMBINE_TILE = 256
VMEM_LIMIT = 48 * 1024 * 1024


def _cparams(sem):
    return pltpu.CompilerParams(dimension_semantics=sem, vmem_limit_bytes=VMEM_LIMIT)


def _split_bf16(a):
    hi = a.astype(BF16)
    lo = (a - hi.astype(F32)).astype(BF16)
    return hi, lo


def _dot(a, b):
    return jnp.dot(a, b, preferred_element_type=F32)


def _dot3(a, b):
    a_hi, a_lo = _split_bf16(a)
    b_hi, b_lo = _split_bf16(b)
    return _dot(a_hi, b_hi) + (_dot(a_hi, b_lo) + _dot(a_lo, b_hi))


def _mod_kernel(c_ref, w_ref, b_ref, o_ref):
    c = c_ref[...]
    a = c * jax.nn.sigmoid(c)
    o_ref[...] = _dot3(a, w_ref[...]) + b_ref[...]


def _mod_call(cc, w_ada, b_ada):
    rows, d = cc.shape
    n = w_ada.shape[1]
    bn = 1024
    return pl.pallas_call(
        _mod_kernel,
        out_shape=jax.ShapeDtypeStruct((rows, n), F32),
        grid=(n // bn,),
        in_specs=[pl.BlockSpec((rows, d), lambda j: (0, 0)),
                  pl.BlockSpec((d, bn), lambda j: (0, j)),
                  pl.BlockSpec((1, bn), lambda j: (0, j))],
        out_specs=pl.BlockSpec((rows, bn), lambda j: (0, j)),
        compiler_params=_cparams(("arbitrary",)),
        name="mod",
    )(cc, w_ada, b_ada.reshape(1, n))


def _inproj_kernel(x_ref, sh_ref, sc_ref, g_ref, w_ref, qg_ref, kg_ref, cos_ref, sin_ref, bd_ref,
                   q_ref, k_ref, v_ref, u_ref):
    x = x_ref[0]
    ms = jnp.mean(x * x, axis=-1, keepdims=True)
    h = (x * lax.rsqrt(ms + RMS_EPS)) * g_ref[...]
    h = h * (1.0 + sc_ref[0]) + sh_ref[0]
    p = _dot(h.astype(BF16), w_ref[...])

    cosf = cos_ref[...]
    sinf = sin_ref[...]
    lane = lax.broadcasted_iota(jnp.int32, cosf.shape, 1)
    first_half = (lane % 32) < 16

    def norm_rope(t, gain, scale):
        ss = _dot((t * t).astype(BF16), bd_ref[...])
        t = t * lax.rsqrt(ss * (1.0 / HEAD_DIM) + RMS_EPS)
        outs = []
        for s in range(QK_WIDTH // LANES):
            ts = t[:, s * LANES:(s + 1) * LANES] * gain
            partner = jnp.where(first_half, pltpu.roll(ts, LANES - 16, 1), pltpu.roll(ts, 16, 1))
            outs.append(((ts * cosf + partner * sinf) * scale).astype(BF16))
        return jnp.concatenate(outs, axis=1)

    q_ref[0] = norm_rope(p[:, :QK_WIDTH], qg_ref[...], HEAD_DIM ** -0.5)
    k_ref[0] = norm_rope(p[:, QK_WIDTH:2 * QK_WIDTH], kg_ref[...], 1.0)
    v_ref[0] = p[:, 2 * QK_WIDTH:2 * QK_WIDTH + V_WIDTH].astype(BF16)
    u_ref[0] = p[:, 2 * QK_WIDTH + V_WIDTH:].astype(BF16)


def _inproj_call(x, sh, sc, g, w_bf, qg, kg, cosf, sinf, bd, tm, name):
    b, n, d = x.shape
    wn = w_bf.shape[1]
    per_batch = sh.shape[0] > 1
    mod_map = (lambda bi, i: (bi, 0, 0)) if per_batch else (lambda bi, i: (0, 0, 0))
    const2 = lambda bi, i: (0, 0)
    outs = [jax.ShapeDtypeStruct((b, n, QK_WIDTH), BF16)] * 4
    tok_spec = pl.BlockSpec((1, tm, QK_WIDTH), lambda bi, i: (bi, i, 0))
    return pl.pallas_call(
        _inproj_kernel,
        out_shape=outs,
        grid=(b, n // tm),
        in_specs=[pl.BlockSpec((1, tm, d), lambda bi, i: (bi, i, 0)),
                  pl.BlockSpec((1, 1, d), mod_map),
                  pl.BlockSpec((1, 1, d), mod_map),
                  pl.BlockSpec((1, d), const2),
                  pl.BlockSpec((d, wn), const2),
                  pl.BlockSpec((1, LANES), const2),
                  pl.BlockSpec((1, LANES), const2),
                  pl.BlockSpec((tm, LANES), lambda bi, i: (i, 0)),
                  pl.BlockSpec((tm, LANES), lambda bi, i: (i, 0)),
                  pl.BlockSpec((QK_WIDTH, QK_WIDTH), const2)],
        out_specs=[tok_spec] * 4,
        compiler_params=_cparams(("parallel", "parallel")),
        name=name,
    )(x, sh, sc, g, w_bf, qg, kg, cosf, sinf, bd)


def _attn_kernel(lam_ref, q_ref, kl_ref, vl_ref, kc_ref, vc_ref, sg_ref, o_ref,
                 m1_ref, l1_ref, a1_ref, m2_ref, l2_ref, a2_ref, *, tk, out_scale):
    q = q_ref[0]
    lane = lax.broadcasted_iota(jnp.int32, q.shape, 1)
    zero = jnp.zeros_like(q)
    qa = jnp.where(lane < HEAD_DIM, q, zero)
    qb = jnp.where(lane >= HEAD_DIM, q, zero)

    for r in (m1_ref, m2_ref):
        r[...] = jnp.full(r.shape, -jnp.inf, F32)
    for r in (l1_ref, l2_ref, a1_ref, a2_ref):
        r[...] = jnp.zeros(r.shape, F32)

    def one_map(qm, kc, vc, m_ref, l_ref, a_ref):
        s = lax.dot_general(qm, kc, (((1,), (1,)), ((), ())), preferred_element_type=F32)
        m_prev = m_ref[...]
        m_next = jnp.maximum(m_prev, jnp.max(s, axis=1, keepdims=True))
        p = jnp.exp(s - m_next[:, :1])
        alpha = jnp.exp(m_prev - m_next)
        l_ref[...] = alpha * l_ref[...] + jnp.sum(p, axis=1, keepdims=True)
        a_ref[...] = alpha * a_ref[...] + _dot(p.astype(BF16), vc)
        m_ref[...] = m_next

    def step(kc, vc):
        one_map(qa, kc, vc, m1_ref, l1_ref, a1_ref)
        one_map(qb, kc, vc, m2_ref, l2_ref, a2_ref)

    def lat_body(j, carry):
        off = pl.multiple_of(j * tk, tk)
        step(kl_ref[0, pl.ds(off, tk), :], vl_ref[0, pl.ds(off, tk), :])
        return carry

    lax.fori_loop(0, kl_ref.shape[1] // tk, lat_body, 0)
    step(kc_ref[0], vc_ref[0])

    lam = lam_ref[...]
    o = a1_ref[...] / l1_ref[...] - lam * (a2_ref[...] / l2_ref[...])
    ms = jnp.mean(o * o, axis=-1, keepdims=True)
    o = o * lax.rsqrt(ms + RMS_EPS) * sg_ref[...]
    o_ref[0] = (o * out_scale).astype(BF16)


def _attn_call(lam_row, q, k_lat, v_lat, k_ctx, v_ctx, sg, tq, tk, out_scale):
    b, n, _ = q.shape
    nc = k_ctx.shape[1]
    kv_lat = pl.BlockSpec((1, n, LANES), lambda bi, h, i: (bi, 0, h))
    kv_ctx = pl.BlockSpec((1, nc, LANES), lambda bi, h, i: (bi, 0, h))
    q_spec = pl.BlockSpec((1, tq, LANES), lambda bi, h, i: (bi, i, h))
    row = pl.BlockSpec((1, LANES), lambda bi, h, i: (0, 0))
    acc = pltpu.VMEM((tq, LANES), F32)
    return pl.pallas_call(
        functools.partial(_attn_kernel, tk=tk, out_scale=out_scale),
        out_shape=jax.ShapeDtypeStruct((b, n, V_WIDTH), BF16),
        grid=(b, HEADS, n // tq),
        in_specs=[row, q_spec, kv_lat, kv_lat, kv_ctx, kv_ctx, row],
        out_specs=q_spec,
        scratch_shapes=[acc] * 6,
        compiler_params=_cparams(("parallel", "parallel", "arbitrary")),
        name="attn",
    )(lam_row, q, k_lat, v_lat, k_ctx, v_ctx, sg)


def _s5_kernel(ul_ref, uc_ref, win_ref, m_ref, wo_ref, lam_ref, o_ref, xl_ref, xc_ref, s_ref, *, nb):
    ul = ul_ref[0]
    xl_ref[...] = _dot(ul, win_ref[0])
    xc_ref[...] = _dot(uc_ref[0], win_ref[0])
    n_lat = ul.shape[0] // nb
    n_ctx = uc_ref.shape[1] // nb

    lam = lam_ref[0]
    lfr, lfi, lbr, lbi = (jnp.broadcast_to(lam[i:i + 1], (nb, LANES)) for i in range(4))

    def rows(c):
        return pl.ds(pl.multiple_of(c * nb, nb), nb)

    def advance(x_ref, c, lane0, ar, ai, sr, si):
        xr = x_ref[rows(c), lane0:lane0 + LANES]
        xi = x_ref[rows(c), lane0 + LANES:lane0 + 2 * LANES]
        return ar * sr - ai * si + xr, ar * si + ai * sr + xi

    def ctx_body(i, carry):
        fr, fi, br, bi = carry
        fr, fi = advance(xc_ref, i, 0, lfr, lfi, fr, fi)
        br, bi = advance(xc_ref, n_ctx - 1 - i, 2 * LANES, lbr, lbi, br, bi)
        return fr, fi, br, bi

    def lat_body(i, carry):
        fr, fi, br, bi = carry
        cb = n_lat - 1 - i
        s_ref[rows(i), 0:LANES] = fr.astype(BF16)
        s_ref[rows(i), LANES:2 * LANES] = fi.astype(BF16)
        s_ref[rows(cb), 2 * LANES:3 * LANES] = br.astype(BF16)
        s_ref[rows(cb), 3 * LANES:4 * LANES] = bi.astype(BF16)
        fr, fi = advance(xl_ref, i, 0, lfr, lfi, fr, fi)
        br, bi = advance(xl_ref, cb, 2 * LANES, lbr, lbi, br, bi)
        return fr, fi, br, bi

    z = jnp.zeros((nb, LANES), F32)
    carry = lax.fori_loop(0, n_ctx, ctx_body, (z, z, z, z))
    lax.fori_loop(0, n_lat, lat_body, carry)

    y = _dot(ul, m_ref[0]) + _dot(s_ref[...], wo_ref[0])
    o_ref[0] = jax.nn.gelu(y).astype(BF16)


def _s5_call(u_lat, u_ctx, win, m, wo, lam16, nb):
    npair, rl, kw = u_lat.shape
    rc = u_ctx.shape[1]
    wspec = pl.BlockSpec((1, kw, kw), lambda p: (p, 0, 0))
    return pl.pallas_call(
        functools.partial(_s5_kernel, nb=nb),
        out_shape=jax.ShapeDtypeStruct((npair, rl, kw), BF16),
        grid=(npair,),
        in_specs=[pl.BlockSpec((1, rl, kw), lambda p: (p, 0, 0)),
                  pl.BlockSpec((1, rc, kw), lambda p: (p, 0, 0)),
                  wspec, wspec, wspec,
                  pl.BlockSpec((1, 4, LANES), lambda p: (p, 0, 0))],
        out_specs=pl.BlockSpec((1, rl, kw), lambda p: (p, 0, 0)),
        scratch_shapes=[pltpu.VMEM((rl, kw), F32), pltpu.VMEM((rc, kw), F32), pltpu.VMEM((rl, kw), BF16)],
        compiler_params=_cparams(("parallel",)),
        name="s5",
    )(u_lat, u_ctx, win, m, wo, lam16)


def _s5_weights(a_re, a_im, log_dt, b_re, b_im, c_re, c_im, d_skip):
    hp = lax.Precision.HIGHEST
    g_n, p_n = a_re.shape[1], a_re.shape[2]
    t_n = CHUNK
    dt = jnp.exp(log_dt.astype(F32))[..., None]
    ar, ai = a_re.astype(F32), a_im.astype(F32)
    mag = jnp.exp(ar * dt)
    lr, li = mag * jnp.cos(ai * dt), mag * jnp.sin(ai * dt)
    den = ar * ar + ai * ai
    nr, ni = lr - 1.0, li
    cr = (nr * ar + ni * ai) / den
    ci = (ni * ar - nr * ai) / den
    bbr = cr[..., None] * b_re - ci[..., None] * b_im
    bbi = cr[..., None] * b_im + ci[..., None] * b_re
    n = jnp.arange(t_n + 1, dtype=F32)[:, None, None, None]
    pm = jnp.exp(n * (ar * dt))
    pw_r, pw_i = pm * jnp.cos(n * (ai * dt)), pm * jnp.sin(n * (ai * dt))
    lb_r = pw_r[:t_n, ..., None] * bbr - pw_i[:t_n, ..., None] * bbi
    lb_i = pw_r[:t_n, ..., None] * bbi + pw_i[:t_n, ..., None] * bbr
    cre, cim = c_re.astype(F32), c_im.astype(F32)
    kern = (jnp.einsum('gip,tdgpj->tdgij', cre, lb_r, precision=hp)
            - jnp.einsum('gip,tdgpj->tdgij', cim, lb_i, precision=hp))
    s_idx = jnp.arange(t_n)[:, None]
    t_idx = jnp.arange(t_n)[None, :]
    tau = t_idx - s_idx
    mf = jnp.where((tau >= 0)[..., None, None, None], kern[jnp.clip(tau, 0, t_n - 1), 0], 0.0)
    mb = jnp.where((tau <= 0)[..., None, None, None], kern[jnp.clip(-tau, 0, t_n - 1), 1], 0.0)
    hh = SSM_GROUP
    skip = (jnp.eye(t_n, dtype=F32)[:, :, None, None, None] * jnp.eye(hh, dtype=F32)[None, None, None]
            * d_skip.astype(F32)[None, None, :, :, None])
    m_g = jnp.transpose(mf + mb + skip, (2, 0, 4, 1, 3))

    def to_rows(w):
        return jnp.transpose(w, (1, 0, 3, 2))

    win4 = jnp.stack([to_rows(lb_r[::-1, 0]), to_rows(lb_i[::-1, 0]),
                      to_rows(lb_r[:, 1]), to_rows(lb_i[:, 1])], axis=1)

    def to_cols(zr, zi):
        o_re = cre[None] * zr[:, :, None, :] - cim[None] * zi[:, :, None, :]
        o_im = -(cre[None] * zi[:, :, None, :] + cim[None] * zr[:, :, None, :])
        f = lambda o: jnp.transpose(o, (1, 3, 0, 2))
        return f(o_re), f(o_im)

    of_re, of_im = to_cols(pw_r[1:, 0], pw_i[1:, 0])
    ob_re, ob_im = to_cols(pw_r[1:, 1][::-1], pw_i[1:, 1][::-1])
    wo4 = jnp.stack([of_re, of_im, ob_re, ob_im], axis=1)

    npair = g_n // 2
    eye2 = jnp.eye(2, dtype=F32)
    kw = 2 * t_n * hh
    m_pair = jnp.einsum('ab,pasjti->psajtbi', eye2,
                        m_g.reshape(npair, 2, t_n, hh, t_n, hh)).reshape(npair, kw, kw)
    win_pair = jnp.einsum('ab,paksjq->psajkbq', eye2,
                          win4.reshape(npair, 2, 4, t_n, hh, p_n)).reshape(npair, kw, 8 * p_n)
    wo_pair = jnp.einsum('ab,pakqti->pkaqtbi', eye2,
                         wo4.reshape(npair, 2, 4, p_n, t_n, hh)).reshape(npair, 8 * p_n, kw)
    lam16 = jnp.stack([pw_r[t_n, 0], pw_i[t_n, 0], pw_r[t_n, 1], pw_i[t_n, 1]], axis=1)
    lam16 = jnp.transpose(lam16.reshape(npair, 2, 4, p_n), (0, 2, 1, 3)).reshape(npair, 4, 2 * p_n)
    return win_pair.astype(BF16), m_pair.astype(BF16), wo_pair.astype(BF16), lam16


def _to_chunk_rows(u):
    b, n, w = u.shape
    npair = w // (2 * SSM_GROUP)
    u5 = u.reshape(b, n // CHUNK, CHUNK, npair, 2 * SSM_GROUP)
    return jnp.transpose(u5, (3, 1, 0, 2, 4)).reshape(npair, (n // CHUNK) * b, CHUNK * 2 * SSM_GROUP)


def _from_chunk_rows(y, b):
    npair, r, kw = y.shape
    nchunk = r // b
    y5 = y.reshape(npair, nchunk, b, CHUNK, 2 * SSM_GROUP)
    return jnp.transpose(y5, (2, 1, 3, 0, 4)).reshape(b, nchunk * CHUNK, npair * 2 * SSM_GROUP)


def _outproj_kernel(a_ref, y_ref, x_ref, g1_ref, sh_ref, sc_ref, n2_ref, wglu_ref, bglu_ref, wout_ref,
                    wr_hi_ref, wr_lo_ref, br_ref, x1_ref, h2_ref, gate_ref):
    y = y_ref[0]
    z = _dot(y, wglu_ref[...]) + bglu_ref[...]
    s = (y.astype(F32) * jax.nn.sigmoid(z)).astype(BF16)
    half = a_ref.shape[2]
    o = _dot(a_ref[0], wout_ref[:half, :]) + _dot(s, wout_ref[half:, :])
    x1 = x_ref[0] + g1_ref[0] * o
    x1_ref[0] = x1

    ms = jnp.mean(x1 * x1, axis=-1, keepdims=True)
    h = (x1 * lax.rsqrt(ms + RMS_EPS)) * n2_ref[...]
    h = h * (1.0 + sc_ref[0]) + sh_ref[0]
    h2_ref[0] = h

    h_hi, h_lo = _split_bf16(h)
    lg = _dot(h_hi, wr_hi_ref[...]) + (_dot(h_hi, wr_lo_ref[...]) + _dot(h_lo, wr_hi_ref[...])) + br_ref[...]
    lane = lax.broadcasted_iota(jnp.int32, lg.shape, 1)
    neg = jnp.float32(-jnp.inf)
    big = jnp.int32(LANES)

    def top1(vals):
        vmax = jnp.max(vals, axis=1, keepdims=True)
        idx = jnp.min(jnp.where(vals == vmax, lane, big), axis=1, keepdims=True)
        return vmax, idx

    is_grp = lane < MOE_GROUPS
    g_vals = jnp.where(is_grp, lg, neg)
    g_max, g_idx = top1(g_vals)
    p_grp = 1.0 / jnp.sum(jnp.where(is_grp, jnp.exp(g_vals - g_max), 0.0), axis=1, keepdims=True)
    e_lo = MOE_GROUPS + EXPERTS_PER_GROUP * g_idx
    in_grp = (lane >= e_lo) & (lane < e_lo + EXPERTS_PER_GROUP)
    e_vals = jnp.where(in_grp, lg, neg)
    v1, i1 = top1(e_vals)
    v2, i2 = top1(jnp.where(lane == i1, neg, e_vals))
    r = jnp.exp(v2 - v1)
    w1 = p_grp / (1.0 + r)
    w2 = w1 * r
    e1 = (i1 - MOE_GROUPS).astype(F32)
    e2 = (i2 - MOE_GROUPS).astype(F32)
    gate_ref[0] = (jnp.where(lane == 0, w1, 0.0) + jnp.where(lane == 1, w2, 0.0)
                   + jnp.where(lane == 2, e1, 0.0) + jnp.where(lane == 3, e2, 0.0))


def _outproj_call(a, yg, x, g1, sh2, sc2, n2g, wglu, bglu, wout, wr_hi, wr_lo, br, tm):
    b, n, d = x.shape
    half = a.shape[2]
    tok = lambda w: pl.BlockSpec((1, tm, w), lambda bi, i: (bi, i, 0))
    mod = pl.BlockSpec((1, 1, d), lambda bi, i: (bi, 0, 0))
    const = lambda r, c: pl.BlockSpec((r, c), lambda bi, i: (0, 0))
    return pl.pallas_call(
        _outproj_kernel,
        out_shape=[jax.ShapeDtypeStruct((b, n, d), F32), jax.ShapeDtypeStruct((b, n, d), F32),
                   jax.ShapeDtypeStruct((b, n, LANES), F32)],
        grid=(b, n // tm),
        in_specs=[tok(half), tok(half), tok(d), mod, mod, mod, const(1, d),
                  const(half, half), const(1, half), const(d, d),
                  const(d, LANES), const(d, LANES), const(1, LANES)],
        out_specs=[tok(d), tok(d), tok(LANES)],
        compiler_params=_cparams(("parallel", "parallel")),
        name="outproj",
    )(a, yg, x, g1, sh2, sc2, n2g, wglu, bglu, wout, wr_hi, wr_lo, br)


def _row_gather_start(idx_smem, slot, src_hbm, dst, sem, nrows):
    def body(r, carry):
        tok = idx_smem[slot, r]
        pltpu.make_async_copy(src_hbm.at[pl.ds(tok, 1), :], dst.at[slot, pl.ds(r, 1), :], sem.at[slot]).start()
        return carry
    lax.fori_loop(0, nrows, body, 0, unroll=8)


def _row_gather_wait(src_hbm, dst, sem, slot, nrows):
    pltpu.make_async_copy(src_hbm.at[pl.ds(0, nrows), :], dst.at[slot], sem.at[slot]).wait()


def _idx_copy(idx_hbm, idx_smem, isem, tile, slot):
    return pltpu.make_async_copy(idx_hbm.at[pl.ds(tile, 1), :], idx_smem.at[pl.ds(slot, 1), :], isem.at[slot])


def _gather_pipeline_step(i, n_tiles, n_valid, idx_hbm, idx_smem, isem, src_hbm, buf, gsem, nrows):
    slot = i % 2
    nxt = 1 - slot

    @pl.when(i == 0)
    def _():
        first = _idx_copy(idx_hbm, idx_smem, isem, 0, 0)
        first.start()
        first.wait()
        _row_gather_start(idx_smem, 0, src_hbm, buf, gsem, nrows)

        @pl.when(n_tiles > 1)
        def _():
            _idx_copy(idx_hbm, idx_smem, isem, 1, 1).start()

    @pl.when(i + 1 < n_tiles)
    def _():
        _idx_copy(idx_hbm, idx_smem, isem, i + 1, nxt).wait()

        @pl.when(i + 1 < n_valid)
        def _():
            _row_gather_start(idx_smem, nxt, src_hbm, buf, gsem, nrows)

        @pl.when(i + 2 < n_tiles)
        def _():
            _idx_copy(idx_hbm, idx_smem, isem, i + 2, slot).start()

    @pl.when(i < n_valid)
    def _():
        _row_gather_wait(src_hbm, buf, gsem, slot, nrows)

    return slot


def _moe_expert_kernel(te_ref, nv_ref, idx_hbm, h_hbm, wg_ref, wu_ref, wd_ref, y_ref,
                       idx_smem, hbuf, gsem, isem):
    i = pl.program_id(0)
    n_valid = nv_ref[0]
    tm = hbuf.shape[1]
    slot = _gather_pipeline_step(i, pl.num_programs(0), n_valid, idx_hbm, idx_smem, isem, h_hbm, hbuf, gsem, tm)

    @pl.when(i < n_valid)
    def _():
        h = hbuf[slot].astype(BF16)
        hid = jax.nn.silu(_dot(h, wg_ref[0])) * _dot(h, wu_ref[0])
        y_ref[...] = _dot(hid.astype(BF16), wd_ref[0])

    @pl.when(i >= n_valid)
    def _():
        y_ref[...] = jnp.zeros(y_ref.shape, F32)


def _moe_expert_call(tile_expert, n_valid, idx, h2, wg, wu, wd):
    n_tiles, tm = idx.shape
    d = h2.shape[1]
    f = wg.shape[2]
    wmap = lambda i, te, nv: (te[i], 0, 0)
    grid_spec = pltpu.PrefetchScalarGridSpec(
        num_scalar_prefetch=2,
        grid=(n_tiles,),
        in_specs=[pl.BlockSpec(memory_space=pl.ANY),
                  pl.BlockSpec(memory_space=pl.ANY),
                  pl.BlockSpec((1, d, f), wmap),
                  pl.BlockSpec((1, d, f), wmap),
                  pl.BlockSpec((1, f, d), wmap)],
        out_specs=pl.BlockSpec((tm, d), lambda i, te, nv: (i, 0)),
        scratch_shapes=[pltpu.SMEM((2, tm), jnp.int32), pltpu.VMEM((2, tm, d), F32),
                        pltpu.SemaphoreType.DMA((2,)), pltpu.SemaphoreType.DMA((2,))])
    return pl.pallas_call(
        _moe_expert_kernel,
        out_shape=jax.ShapeDtypeStruct((n_tiles * tm, d), F32),
        grid_spec=grid_spec,
        compiler_params=_cparams(("arbitrary",)),
        name="moe_experts",
    )(tile_expert, n_valid, idx, h2, wg, wu, wd)


def _moe_combine_kernel(pos_hbm, y_hbm, route_ref, x1_ref, g2_ref, o_ref, idx_smem, ybuf, gsem, isem):
    i = pl.program_id(0)
    n_tiles = pl.num_programs(0)
    tm = x1_ref.shape[0]
    slot = _gather_pipeline_step(i, n_tiles, n_tiles, pos_hbm, idx_smem, isem, y_hbm, ybuf, gsem, 2 * tm)
    route = route_ref[...]
    w1, w2 = route[:, 0:1], route[:, 1:2]
    moe = w1 * ybuf[slot, :tm, :] + w2 * ybuf[slot, tm:, :]
    o_ref[...] = x1_ref[...] + g2_ref[0] * moe


def _moe_combine_call(pos, y_sorted, route, x1, g2, tokens_per_batch):
    n_tiles, two_tm = pos.shape
    tm = two_tm // 2
    t, d = x1.shape
    per_b = tokens_per_batch // tm
    tok = lambda w: pl.BlockSpec((tm, w), lambda i: (i, 0))
    return pl.pallas_call(
        _moe_combine_kernel,
        out_shape=jax.ShapeDtypeStruct((t, d), F32),
        grid=(n_tiles,),
        in_specs=[pl.BlockSpec(memory_space=pl.ANY), pl.BlockSpec(memory_space=pl.ANY),
                  tok(LANES), tok(d), pl.BlockSpec((1, 1, d), lambda i: (i // per_b, 0, 0))],
        out_specs=tok(d),
        scratch_shapes=[pltpu.SMEM((2, two_tm), jnp.int32), pltpu.VMEM((2, two_tm, d), F32),
                        pltpu.SemaphoreType.DMA((2,)), pltpu.SemaphoreType.DMA((2,))],
        compiler_params=_cparams(("arbitrary",)),
        name="moe_combine",
    )(pos, y_sorted, route, x1, g2)


def _routing_plan(e_ids, tm):
    t = e_ids.shape[0]
    flat = e_ids.reshape(-1)
    n_slots = flat.shape[0]
    n_tiles = n_slots // tm + N_EXPERTS
    onehot = (flat[:, None] == jnp.arange(N_EXPERTS, dtype=jnp.int32)[None, :]).astype(jnp.int32)
    csum = jnp.cumsum(onehot, axis=0)
    rank = jnp.take_along_axis(csum, flat[:, None], axis=1)[:, 0] - 1
    counts = csum[-1]
    tiles_e = (counts + tm - 1) // tm
    tile_end = jnp.cumsum(tiles_e)
    tile_start = tile_end - tiles_e
    pos = tile_start[flat] * tm + rank
    n_valid = tile_end[-1]
    tile_ids = jnp.arange(n_tiles, dtype=jnp.int32)
    te = jnp.minimum(jnp.searchsorted(tile_end, tile_ids, side='right'), N_EXPERTS - 1).astype(jnp.int32)
    te = jnp.where(tile_ids < n_valid, te, te[jnp.maximum(n_valid - 1, 0)])
    src = jnp.zeros((n_tiles * tm,), jnp.int32).at[pos].set(jnp.arange(n_slots, dtype=jnp.int32) // 2)
    return te, n_valid.reshape(1).astype(jnp.int32), src.reshape(n_tiles, tm), pos.reshape(t, 2).astype(jnp.int32)


def _rope_tables(n_tokens):
    rows = n_tokens // GRID_W
    row = jnp.broadcast_to(jnp.arange(rows, dtype=F32)[:, None], (rows, GRID_W)).reshape(-1)
    col = jnp.broadcast_to(jnp.arange(GRID_W, dtype=F32)[None, :], (rows, GRID_W)).reshape(-1)
    half = HEAD_DIM // 2
    inv = ROPE_BASE ** (-jnp.arange(0, half, 2, dtype=F32) / half)
    ang = jnp.stack([row[:, None] * inv, col[:, None] * inv], axis=1)
    cos, sin = jnp.cos(ang), jnp.sin(ang)
    cos64 = jnp.concatenate([cos[:, 0], cos[:, 0], cos[:, 1], cos[:, 1]], axis=1)
    sin64 = jnp.concatenate([-sin[:, 0], sin[:, 0], -sin[:, 1], sin[:, 1]], axis=1)
    return jnp.tile(cos64, (1, LANES // HEAD_DIM)), jnp.tile(sin64, (1, LANES // HEAD_DIM))


def _pick_tile(n, target):
    t = min(n, target)
    while n % t:
        t //= 2
    return t


def kernel(x, c, ctx, c_ctx, w_ada, b_ada, norm1_g, w_in, q_norm_g, k_norm_g, lambda_q1, lambda_k1, lambda_q2, lambda_k2, subln_g, ssm_a_re, ssm_a_im, ssm_log_dt, ssm_b_re, ssm_b_im, ssm_c_re, ssm_c_im, ssm_d, w_glu, b_glu, w_out, norm2_g, w_route_group, b_route_group, w_route_expert, b_route_expert, w_exp_gate, w_exp_up, w_exp_down):
    depth = w_ada.shape[0]
    assert depth == 1, "single-layer block: the context stream is never updated"
    b, n_lat, d = x.shape
    n_ctx = ctx.shape[1]
    assert n_lat % CHUNK == 0 and n_ctx % CHUNK == 0 and n_lat % GRID_W == 0
    l = 0
    lam_init = 0.8 - 0.6 * math.exp(-0.3 * l)

    rows = b + 1
    rows_pad = -(-rows // 8) * 8
    cc = jnp.concatenate([c, c_ctx[None, :], jnp.zeros((rows_pad - rows, d), F32)], axis=0)
    mod = _mod_call(cc, w_ada[l], b_ada[l])
    sh1, sc1, g1, sh2, sc2, g2 = (mod[:b, i * d:(i + 1) * d].reshape(b, 1, d) for i in range(6))
    csh1, csc1 = (mod[b:b + 1, i * d:(i + 1) * d].reshape(1, 1, d) for i in range(2))

    w_in_bf = w_in[l].astype(BF16)
    bd = jnp.kron(jnp.eye(QK_WIDTH // HEAD_DIM, dtype=F32), jnp.ones((HEAD_DIM, HEAD_DIM), F32)).astype(BF16)
    qg = jnp.tile(q_norm_g[l], LANES // HEAD_DIM).reshape(1, LANES)
    kg = jnp.tile(k_norm_g[l], LANES // HEAD_DIM).reshape(1, LANES)
    cosf, sinf = _rope_tables(n_lat)
    ones_c, zeros_c = jnp.ones((n_ctx, LANES), F32), jnp.zeros((n_ctx, LANES), F32)
    g1n = norm1_g[l].reshape(1, d)
    tm = _pick_tile(n_lat, 512)
    q_x, k_x, v_x, u_x = _inproj_call(x, sh1, sc1, g1n, w_in_bf, qg, kg, cosf, sinf, bd, tm, "inproj_lat")
    _, k_c, v_c, u_c = _inproj_call(ctx, csh1, csc1, g1n, w_in_bf, qg, kg, ones_c, zeros_c, bd,
                                    _pick_tile(n_ctx, 512), "inproj_ctx")

    e1 = jnp.exp(jnp.sum(lambda_q1[l] * lambda_k1[l]))
    e2 = jnp.exp(jnp.sum(lambda_q2[l] * lambda_k2[l]))
    lam_row = jnp.full((1, LANES), e1 - e2 + lam_init, F32)
    a_x = _attn_call(lam_row, q_x, k_x, v_x, k_c, v_c, subln_g[l].reshape(1, LANES),
                     _pick_tile(n_lat, 512), _pick_tile(n_lat, 512), 1.0 - lam_init)

    win, m_op, wo, lam16 = _s5_weights(ssm_a_re[l], ssm_a_im[l], ssm_log_dt[l], ssm_b_re[l], ssm_b_im[l],
                                       ssm_c_re[l], ssm_c_im[l], ssm_d[l])
    yg = _s5_call(_to_chunk_rows(u_x), _to_chunk_rows(u_c), win, m_op, wo, lam16, b)
    yg = _from_chunk_rows(yg, b)

    wr = jnp.concatenate([w_route_group[l], w_route_expert[l]], axis=1)
    wr = jnp.pad(wr, ((0, 0), (0, LANES - wr.shape[1])))
    wr_hi, wr_lo = _split_bf16(wr)
    br = jnp.pad(jnp.concatenate([b_route_group[l], b_route_expert[l]]), (0, LANES - MOE_GROUPS - N_EXPERTS))
    x1, h2, route = _outproj_call(a_x, yg, x, g1, sh2, sc2, norm2_g[l].reshape(1, d),
                                 w_glu[l].astype(BF16), b_glu[l].reshape(1, -1), w_out[l].astype(BF16),
                                 wr_hi, wr_lo, br.reshape(1, LANES), tm)

    t_all = b * n_lat
    route = route.reshape(t_all, LANES)
    e_ids = route[:, 2:4].astype(jnp.int32)
    tile_expert, n_valid, src, pos = _routing_plan(e_ids, _pick_tile(2 * t_all, MOE_TILE))
    y_sorted = _moe_expert_call(tile_expert, n_valid, src, h2.reshape(t_all, d), w_exp_gate[l].astype(BF16),
                                w_exp_up[l].astype(BF16), w_exp_down[l].astype(BF16))
    tc = _pick_tile(n_lat, COMBINE_TILE)
    pos_tiles = jnp.transpose(pos.reshape(t_all // tc, tc, 2), (0, 2, 1)).reshape(t_all // tc, 2 * tc)
    out = _moe_combine_call(pos_tiles, y_sorted, route, x1.reshape(t_all, d), g2, n_lat)
    return out.reshape(b, n_lat, d)
```

```python
import functools
import math

import jax
import jax.numpy as jnp
from jax import lax
from jax.experimental import pallas as pl
from jax.experimental.pallas import tpu as pltpu

F32 = jnp.float32
BF16 = jnp.bfloat16

LANES = 128
HEADS = 4
HEAD_DIM = 64
QK_WIDTH = HEADS * 2 * HEAD_DIM
V_WIDTH = HEADS * 2 * HEAD_DIM
GRID_W = 64
ROPE_BASE = 10000.0
SSM_GROUP = 16
SSM_STATE = 64
CHUNK = 16
MOE_GROUPS = 4
EXPERTS_PER_GROUP = 8
N_EXPERTS = MOE_GROUPS * EXPERTS_PER_GROUP
RMS_EPS = 1e-6
ATTN_TQ = 512
ATTN_TK = 4096
SCORE_BOUND = 60.0
MOE_TILE = 512
COMBINE_TILE = 256
VMEM_LIMIT = 48 * 1024 * 1024


def _cparams(sem):
    return pltpu.CompilerParams(dimension_semantics=sem, vmem_limit_bytes=VMEM_LIMIT)


def _split_bf16(a):
    hi = a.astype(BF16)
    lo = (a - hi.astype(F32)).astype(BF16)
    return hi, lo


def _dot(a, b):
    return jnp.dot(a, b, preferred_element_type=F32)


def _dot3(a, b):
    a_hi, a_lo = _split_bf16(a)
    b_hi, b_lo = _split_bf16(b)
    return _dot(a_hi, b_hi) + (_dot(a_hi, b_lo) + _dot(a_lo, b_hi))


def _mod_kernel(c_ref, w_ref, b_ref, o_ref):
    c = c_ref[...]
    a = c * jax.nn.sigmoid(c)
    o_ref[...] = _dot3(a, w_ref[...]) + b_ref[...]


def _mod_call(cc, w_ada, b_ada):
    rows, d = cc.shape
    n = w_ada.shape[1]
    bn = 1024
    return pl.pallas_call(
        _mod_kernel,
        out_shape=jax.ShapeDtypeStruct((rows, n), F32),
        grid=(n // bn,),
        in_specs=[pl.BlockSpec((rows, d), lambda j: (0, 0)),
                  pl.BlockSpec((d, bn), lambda j: (0, j)),
                  pl.BlockSpec((1, bn), lambda j: (0, j))],
        out_specs=pl.BlockSpec((rows, bn), lambda j: (0, j)),
        compiler_params=_cparams(("arbitrary",)),
        name="mod",
    )(cc, w_ada, b_ada.reshape(1, n))


def _inproj_kernel(x_ref, sh_ref, sc_ref, g_ref, w_ref, qg_ref, kg_ref, cos_ref, sin_ref, bd_ref,
                   q_ref, k_ref, v_ref, u_ref):
    x = x_ref[0]
    ms = jnp.mean(x * x, axis=-1, keepdims=True)
    h = (x * lax.rsqrt(ms + RMS_EPS)) * g_ref[...]
    h = h * (1.0 + sc_ref[0]) + sh_ref[0]
    p = _dot(h.astype(BF16), w_ref[...])

    cosf = cos_ref[...]
    sinf = sin_ref[...]
    lane = lax.broadcasted_iota(jnp.int32, cosf.shape, 1)
    first_half = (lane % 32) < 16

    def norm_rope(t, gain, scale):
        ss = _dot((t * t).astype(BF16), bd_ref[...])
        t = t * lax.rsqrt(ss * (1.0 / HEAD_DIM) + RMS_EPS)
        outs = []
        for s in range(QK_WIDTH // LANES):
            ts = t[:, s * LANES:(s + 1) * LANES] * gain
            partner = jnp.where(first_half, pltpu.roll(ts, LANES - 16, 1), pltpu.roll(ts, 16, 1))
            outs.append(((ts * cosf + partner * sinf) * scale).astype(BF16))
        return jnp.concatenate(outs, axis=1)

    q_ref[0] = norm_rope(p[:, :QK_WIDTH], qg_ref[...], HEAD_DIM ** -0.5 * math.log2(math.e))
    k_ref[0] = norm_rope(p[:, QK_WIDTH:2 * QK_WIDTH], kg_ref[...], 1.0)
    v_ref[0] = p[:, 2 * QK_WIDTH:2 * QK_WIDTH + V_WIDTH].astype(BF16)
    u_ref[0] = p[:, 2 * QK_WIDTH + V_WIDTH:].astype(BF16)


def _inproj_call(x, sh, sc, g, w_bf, qg, kg, cosf, sinf, bd, tm, name):
    b, n, d = x.shape
    wn = w_bf.shape[1]
    per_batch = sh.shape[0] > 1
    mod_map = (lambda bi, i: (bi, 0, 0)) if per_batch else (lambda bi, i: (0, 0, 0))
    const2 = lambda bi, i: (0, 0)
    outs = [jax.ShapeDtypeStruct((b, n, QK_WIDTH), BF16)] * 4
    tok_spec = pl.BlockSpec((1, tm, QK_WIDTH), lambda bi, i: (bi, i, 0))
    return pl.pallas_call(
        _inproj_kernel,
        out_shape=outs,
        grid=(b, n // tm),
        in_specs=[pl.BlockSpec((1, tm, d), lambda bi, i: (bi, i, 0)),
                  pl.BlockSpec((1, 1, d), mod_map),
                  pl.BlockSpec((1, 1, d), mod_map),
                  pl.BlockSpec((1, d), const2),
                  pl.BlockSpec((d, wn), const2),
                  pl.BlockSpec((1, LANES), const2),
                  pl.BlockSpec((1, LANES), const2),
                  pl.BlockSpec((tm, LANES), lambda bi, i: (i, 0)),
                  pl.BlockSpec((tm, LANES), lambda bi, i: (i, 0)),
                  pl.BlockSpec((QK_WIDTH, QK_WIDTH), const2)],
        out_specs=[tok_spec] * 4,
        compiler_params=_cparams(("parallel", "parallel")),
        name=name,
    )(x, sh, sc, g, w_bf, qg, kg, cosf, sinf, bd)


def _attn_kernel(lam_ref, q_ref, kl_ref, vl_ref, kc_ref, vc_ref, sg_ref, o_ref, a1_ref, a2_ref, m1_ref, m2_ref,
                 *, tk, out_scale, bounded):
    q = q_ref[0]
    lane = lax.broadcasted_iota(jnp.int32, q.shape, 1)
    zero = jnp.zeros_like(q)
    qa = jnp.where(lane < HEAD_DIM, q, zero)
    qb = jnp.where(lane >= HEAD_DIM, q, zero)

    a1_ref[...] = jnp.zeros(a1_ref.shape, F32)
    a2_ref[...] = jnp.zeros(a2_ref.shape, F32)
    if not bounded:
        m1_ref[...] = jnp.full(m1_ref.shape, -jnp.inf, F32)
        m2_ref[...] = jnp.full(m2_ref.shape, -jnp.inf, F32)

    def ones_col(rows):
        col = lax.broadcasted_iota(jnp.int32, (rows, LANES), 1)
        return jnp.where(col == 0, 1.0, 0.0).astype(BF16)

    def one_map(qm, kc, va, a_ref, m_ref):
        s = lax.dot_general(qm, kc, (((1,), (1,)), ((), ())), preferred_element_type=F32)
        if bounded:
            a_ref[...] += _dot(jnp.exp2(s).astype(BF16), va)
        else:
            m_prev = m_ref[...]
            m_next = jnp.maximum(m_prev, jnp.max(s, axis=1, keepdims=True))
            p = jnp.exp2(s - m_next[:, :1])
            alpha = jnp.exp2(m_prev - m_next)
            a_ref[...] = jnp.concatenate([alpha, alpha], axis=1) * a_ref[...] + _dot(p.astype(BF16), va)
            m_ref[...] = m_next

    def step(kc, vc, ones):
        va = jnp.concatenate([vc, ones], axis=1)
        one_map(qa, kc, va, a1_ref, m1_ref)
        one_map(qb, kc, va, a2_ref, m2_ref)

    ones_lat = ones_col(tk)

    def lat_body(j, carry):
        off = pl.multiple_of(j * tk, tk)
        step(kl_ref[0, pl.ds(off, tk), :], vl_ref[0, pl.ds(off, tk), :], ones_lat)
        return carry

    lax.fori_loop(0, kl_ref.shape[1] // tk, lat_body, 0)
    step(kc_ref[0], vc_ref[0], ones_col(kc_ref.shape[1]))

    lam = lam_ref[...]
    a1, a2 = a1_ref[...], a2_ref[...]
    o = a1[:, :LANES] / a1[:, LANES:LANES + 1] - lam * (a2[:, :LANES] / a2[:, LANES:LANES + 1])
    ms = jnp.mean(o * o, axis=-1, keepdims=True)
    o = o * lax.rsqrt(ms + RMS_EPS) * sg_ref[...]
    o_ref[0] = (o * out_scale).astype(BF16)


def _attn_cfg(n_lat):
    return dict(tq=_pick_tile(n_lat, ATTN_TQ), tk=_pick_tile(n_lat, ATTN_TK))


def _attn_call(lam_row, q, k_lat, v_lat, k_ctx, v_ctx, sg, tq, tk, out_scale=1.0, bounded=True):
    b, n, _ = q.shape
    nc = k_ctx.shape[1]
    kv_lat = pl.BlockSpec((1, n, LANES), lambda bi, h, i: (bi, 0, h))
    kv_ctx = pl.BlockSpec((1, nc, LANES), lambda bi, h, i: (bi, 0, h))
    q_spec = pl.BlockSpec((1, tq, LANES), lambda bi, h, i: (bi, i, h))
    row = pl.BlockSpec((1, LANES), lambda bi, h, i: (0, 0))
    acc = pltpu.VMEM((tq, 2 * LANES), F32)
    run_max = pltpu.VMEM((tq, LANES), F32)
    return pl.pallas_call(
        functools.partial(_attn_kernel, tk=tk, out_scale=out_scale, bounded=bounded),
        out_shape=jax.ShapeDtypeStruct((b, n, V_WIDTH), BF16),
        grid=(b, HEADS, n // tq),
        in_specs=[row, q_spec, kv_lat, kv_lat, kv_ctx, kv_ctx, row],
        out_specs=q_spec,
        scratch_shapes=[acc, acc, run_max, run_max],
        compiler_params=_cparams(("parallel", "parallel", "arbitrary")),
        name="attn" if bounded else "attn_general",
    )(lam_row, q, k_lat, v_lat, k_ctx, v_ctx, sg)


def _s5_kernel(ul_ref, uc_ref, win_ref, m_ref, wo_ref, lam_ref, o_ref, xl_ref, xc_ref, s_ref, *, nb):
    ul = ul_ref[0]
    xl_ref[...] = _dot(ul, win_ref[0])
    xc_ref[...] = _dot(uc_ref[0], win_ref[0])
    n_lat = ul.shape[0] // nb
    n_ctx = uc_ref.shape[1] // nb

    lam = lam_ref[0]
    lfr, lfi, lbr, lbi = (jnp.broadcast_to(lam[i:i + 1], (nb, LANES)) for i in range(4))

    def rows(c):
        return pl.ds(pl.multiple_of(c * nb, nb), nb)

    def advance(x_ref, c, lane0, ar, ai, sr, si):
        xr = x_ref[rows(c), lane0:lane0 + LANES]
        xi = x_ref[rows(c), lane0 + LANES:lane0 + 2 * LANES]
        return ar * sr - ai * si + xr, ar * si + ai * sr + xi

    def ctx_body(i, carry):
        fr, fi, br, bi = carry
        fr, fi = advance(xc_ref, i, 0, lfr, lfi, fr, fi)
        br, bi = advance(xc_ref, n_ctx - 1 - i, 2 * LANES, lbr, lbi, br, bi)
        return fr, fi, br, bi

    def lat_body(i, carry):
        fr, fi, br, bi = carry
        cb = n_lat - 1 - i
        s_ref[rows(i), 0:LANES] = fr.astype(BF16)
        s_ref[rows(i), LANES:2 * LANES] = fi.astype(BF16)
        s_ref[rows(cb), 2 * LANES:3 * LANES] = br.astype(BF16)
        s_ref[rows(cb), 3 * LANES:4 * LANES] = bi.astype(BF16)
        fr, fi = advance(xl_ref, i, 0, lfr, lfi, fr, fi)
        br, bi = advance(xl_ref, cb, 2 * LANES, lbr, lbi, br, bi)
        return fr, fi, br, bi

    z = jnp.zeros((nb, LANES), F32)
    carry = lax.fori_loop(0, n_ctx, ctx_body, (z, z, z, z))
    lax.fori_loop(0, n_lat, lat_body, carry)

    y = _dot(ul, m_ref[0]) + _dot(s_ref[...], wo_ref[0])
    o_ref[0] = jax.nn.gelu(y).astype(BF16)


def _s5_call(u_lat, u_ctx, win, m, wo, lam16, nb):
    npair, rl, kw = u_lat.shape
    rc = u_ctx.shape[1]
    wspec = pl.BlockSpec((1, kw, kw), lambda p: (p, 0, 0))
    return pl.pallas_call(
        functools.partial(_s5_kernel, nb=nb),
        out_shape=jax.ShapeDtypeStruct((npair, rl, kw), BF16),
        grid=(npair,),
        in_specs=[pl.BlockSpec((1, rl, kw), lambda p: (p, 0, 0)),
                  pl.BlockSpec((1, rc, kw), lambda p: (p, 0, 0)),
                  wspec, wspec, wspec,
                  pl.BlockSpec((1, 4, LANES), lambda p: (p, 0, 0))],
        out_specs=pl.BlockSpec((1, rl, kw), lambda p: (p, 0, 0)),
        scratch_shapes=[pltpu.VMEM((rl, kw), F32), pltpu.VMEM((rc, kw), F32), pltpu.VMEM((rl, kw), BF16)],
        compiler_params=_cparams(("parallel",)),
        name="s5",
    )(u_lat, u_ctx, win, m, wo, lam16)


def _s5_weights(a_re, a_im, log_dt, b_re, b_im, c_re, c_im, d_skip):
    hp = lax.Precision.HIGHEST
    g_n, p_n = a_re.shape[1], a_re.shape[2]
    t_n = CHUNK
    dt = jnp.exp(log_dt.astype(F32))[..., None]
    ar, ai = a_re.astype(F32), a_im.astype(F32)
    mag = jnp.exp(ar * dt)
    lr, li = mag * jnp.cos(ai * dt), mag * jnp.sin(ai * dt)
    den = ar * ar + ai * ai
    nr, ni = lr - 1.0, li
    cr = (nr * ar + ni * ai) / den
    ci = (ni * ar - nr * ai) / den
    bbr = cr[..., None] * b_re - ci[..., None] * b_im
    bbi = cr[..., None] * b_im + ci[..., None] * b_re
    n = jnp.arange(t_n + 1, dtype=F32)[:, None, None, None]
    pm = jnp.exp(n * (ar * dt))
    pw_r, pw_i = pm * jnp.cos(n * (ai * dt)), pm * jnp.sin(n * (ai * dt))
    lb_r = pw_r[:t_n, ..., None] * bbr - pw_i[:t_n, ..., None] * bbi
    lb_i = pw_r[:t_n, ..., None] * bbi + pw_i[:t_n, ..., None] * bbr
    cre, cim = c_re.astype(F32), c_im.astype(F32)
    kern = (jnp.einsum('gip,tdgpj->tdgij', cre, lb_r, precision=hp)
            - jnp.einsum('gip,tdgpj->tdgij', cim, lb_i, precision=hp))
    s_idx = jnp.arange(t_n)[:, None]
    t_idx = jnp.arange(t_n)[None, :]
    tau = t_idx - s_idx
    mf = jnp.where((tau >= 0)[..., None, None, None], kern[jnp.clip(tau, 0, t_n - 1), 0], 0.0)
    mb = jnp.where((tau <= 0)[..., None, None, None], kern[jnp.clip(-tau, 0, t_n - 1), 1], 0.0)
    hh = SSM_GROUP
    skip = (jnp.eye(t_n, dtype=F32)[:, :, None, None, None] * jnp.eye(hh, dtype=F32)[None, None, None]
            * d_skip.astype(F32)[None, None, :, :, None])
    m_g = jnp.transpose(mf + mb + skip, (2, 0, 4, 1, 3))

    def to_rows(w):
        return jnp.transpose(w, (1, 0, 3, 2))

    win4 = jnp.stack([to_rows(lb_r[::-1, 0]), to_rows(lb_i[::-1, 0]),
                      to_rows(lb_r[:, 1]), to_rows(lb_i[:, 1])], axis=1)

    def to_cols(zr, zi):
        o_re = cre[None] * zr[:, :, None, :] - cim[None] * zi[:, :, None, :]
        o_im = -(cre[None] * zi[:, :, None, :] + cim[None] * zr[:, :, None, :])
        f = lambda o: jnp.transpose(o, (1, 3, 0, 2))
        return f(o_re), f(o_im)

    of_re, of_im = to_cols(pw_r[1:, 0], pw_i[1:, 0])
    ob_re, ob_im = to_cols(pw_r[1:, 1][::-1], pw_i[1:, 1][::-1])
    wo4 = jnp.stack([of_re, of_im, ob_re, ob_im], axis=1)

    npair = g_n // 2
    eye2 = jnp.eye(2, dtype=F32)
    kw = 2 * t_n * hh
    m_pair = jnp.einsum('ab,pasjti->psajtbi', eye2,
                        m_g.reshape(npair, 2, t_n, hh, t_n, hh)).reshape(npair, kw, kw)
    win_pair = jnp.einsum('ab,paksjq->psajkbq', eye2,
                          win4.reshape(npair, 2, 4, t_n, hh, p_n)).reshape(npair, kw, 8 * p_n)
    wo_pair = jnp.einsum('ab,pakqti->pkaqtbi', eye2,
                         wo4.reshape(npair, 2, 4, p_n, t_n, hh)).reshape(npair, 8 * p_n, kw)
    lam16 = jnp.stack([pw_r[t_n, 0], pw_i[t_n, 0], pw_r[t_n, 1], pw_i[t_n, 1]], axis=1)
    lam16 = jnp.transpose(lam16.reshape(npair, 2, 4, p_n), (0, 2, 1, 3)).reshape(npair, 4, 2 * p_n)
    return win_pair.astype(BF16), m_pair.astype(BF16), wo_pair.astype(BF16), lam16


def _to_chunk_rows(u):
    b, n, w = u.shape
    npair = w // (2 * SSM_GROUP)
    u5 = u.reshape(b, n // CHUNK, CHUNK, npair, 2 * SSM_GROUP)
    return jnp.transpose(u5, (3, 1, 0, 2, 4)).reshape(npair, (n // CHUNK) * b, CHUNK * 2 * SSM_GROUP)


def _from_chunk_rows(y, b):
    npair, r, kw = y.shape
    nchunk = r // b
    y5 = y.reshape(npair, nchunk, b, CHUNK, 2 * SSM_GROUP)
    return jnp.transpose(y5, (2, 1, 3, 0, 4)).reshape(b, nchunk * CHUNK, npair * 2 * SSM_GROUP)


def _outproj_kernel(a_ref, y_ref, x_ref, g1_ref, sh_ref, sc_ref, n2_ref, wglu_ref, bglu_ref, wout_ref,
                    wr_hi_ref, wr_lo_ref, br_ref, x1_ref, h2_ref, gate_ref):
    y = y_ref[0]
    z = _dot(y, wglu_ref[...]) + bglu_ref[...]
    s = (y.astype(F32) * jax.nn.sigmoid(z)).astype(BF16)
    half = a_ref.shape[2]
    o = _dot(a_ref[0], wout_ref[:half, :]) + _dot(s, wout_ref[half:, :])
    x1 = x_ref[0] + g1_ref[0] * o
    x1_ref[0] = x1

    ms = jnp.mean(x1 * x1, axis=-1, keepdims=True)
    h = (x1 * lax.rsqrt(ms + RMS_EPS)) * n2_ref[...]
    h = h * (1.0 + sc_ref[0]) + sh_ref[0]
    h2_ref[0] = h

    h_hi, h_lo = _split_bf16(h)
    lg = _dot(h_hi, wr_hi_ref[...]) + (_dot(h_hi, wr_lo_ref[...]) + _dot(h_lo, wr_hi_ref[...])) + br_ref[...]
    lane = lax.broadcasted_iota(jnp.int32, lg.shape, 1)
    neg = jnp.float32(-jnp.inf)
    big = jnp.int32(LANES)

    def top1(vals):
        vmax = jnp.max(vals, axis=1, keepdims=True)
        idx = jnp.min(jnp.where(vals == vmax, lane, big), axis=1, keepdims=True)
        return vmax, idx

    is_grp = lane < MOE_GROUPS
    g_vals = jnp.where(is_grp, lg, neg)
    g_max, g_idx = top1(g_vals)
    p_grp = 1.0 / jnp.sum(jnp.where(is_grp, jnp.exp(g_vals - g_max), 0.0), axis=1, keepdims=True)
    e_lo = MOE_GROUPS + EXPERTS_PER_GROUP * g_idx
    in_grp = (lane >= e_lo) & (lane < e_lo + EXPERTS_PER_GROUP)
    e_vals = jnp.where(in_grp, lg, neg)
    v1, i1 = top1(e_vals)
    v2, i2 = top1(jnp.where(lane == i1, neg, e_vals))
    r = jnp.exp(v2 - v1)
    w1 = p_grp / (1.0 + r)
    w2 = w1 * r
    e1 = (i1 - MOE_GROUPS).astype(F32)
    e2 = (i2 - MOE_GROUPS).astype(F32)
    gate_ref[0] = (jnp.where(lane == 0, w1, 0.0) + jnp.where(lane == 1, w2, 0.0)
                   + jnp.where(lane == 2, e1, 0.0) + jnp.where(lane == 3, e2, 0.0))


def _outproj_call(a, yg, x, g1, sh2, sc2, n2g, wglu, bglu, wout, wr_hi, wr_lo, br, tm):
    b, n, d = x.shape
    half = a.shape[2]
    tok = lambda w: pl.BlockSpec((1, tm, w), lambda bi, i: (bi, i, 0))
    mod = pl.BlockSpec((1, 1, d), lambda bi, i: (bi, 0, 0))
    const = lambda r, c: pl.BlockSpec((r, c), lambda bi, i: (0, 0))
    return pl.pallas_call(
        _outproj_kernel,
        out_shape=[jax.ShapeDtypeStruct((b, n, d), F32), jax.ShapeDtypeStruct((b, n, d), F32),
                   jax.ShapeDtypeStruct((b, n, LANES), F32)],
        grid=(b, n // tm),
        in_specs=[tok(half), tok(half), tok(d), mod, mod, mod, const(1, d),
                  const(half, half), const(1, half), const(d, d),
                  const(d, LANES), const(d, LANES), const(1, LANES)],
        out_specs=[tok(d), tok(d), tok(LANES)],
        compiler_params=_cparams(("parallel", "parallel")),
        name="outproj",
    )(a, yg, x, g1, sh2, sc2, n2g, wglu, bglu, wout, wr_hi, wr_lo, br)


def _row_gather_start(idx_smem, slot, src_hbm, dst, sem, nrows):
    def body(r, carry):
        tok = idx_smem[slot, r]
        pltpu.make_async_copy(src_hbm.at[pl.ds(tok, 1), :], dst.at[slot, pl.ds(r, 1), :], sem.at[slot]).start()
        return carry
    lax.fori_loop(0, nrows, body, 0, unroll=8)


def _row_gather_wait(src_hbm, dst, sem, slot, nrows):
    pltpu.make_async_copy(src_hbm.at[pl.ds(0, nrows), :], dst.at[slot], sem.at[slot]).wait()


def _idx_copy(idx_hbm, idx_smem, isem, tile, slot):
    return pltpu.make_async_copy(idx_hbm.at[pl.ds(tile, 1), :], idx_smem.at[pl.ds(slot, 1), :], isem.at[slot])


def _gather_pipeline_step(i, n_tiles, n_valid, idx_hbm, idx_smem, isem, src_hbm, buf, gsem, nrows):
    slot = i % 2
    nxt = 1 - slot

    @pl.when(i == 0)
    def _():
        first = _idx_copy(idx_hbm, idx_smem, isem, 0, 0)
        first.start()
        first.wait()
        _row_gather_start(idx_smem, 0, src_hbm, buf, gsem, nrows)

        @pl.when(n_tiles > 1)
        def _():
            _idx_copy(idx_hbm, idx_smem, isem, 1, 1).start()

    @pl.when(i + 1 < n_tiles)
    def _():
        _idx_copy(idx_hbm, idx_smem, isem, i + 1, nxt).wait()

        @pl.when(i + 1 < n_valid)
        def _():
            _row_gather_start(idx_smem, nxt, src_hbm, buf, gsem, nrows)

        @pl.when(i + 2 < n_tiles)
        def _():
            _idx_copy(idx_hbm, idx_smem, isem, i + 2, slot).start()

    @pl.when(i < n_valid)
    def _():
        _row_gather_wait(src_hbm, buf, gsem, slot, nrows)

    return slot


def _moe_expert_kernel(te_ref, nv_ref, idx_hbm, h_hbm, wg_ref, wu_ref, wd_ref, y_ref,
                       idx_smem, hbuf, gsem, isem):
    i = pl.program_id(0)
    n_valid = nv_ref[0]
    tm = hbuf.shape[1]
    slot = _gather_pipeline_step(i, pl.num_programs(0), n_valid, idx_hbm, idx_smem, isem, h_hbm, hbuf, gsem, tm)

    @pl.when(i < n_valid)
    def _():
        h = hbuf[slot].astype(BF16)
        hid = jax.nn.silu(_dot(h, wg_ref[0])) * _dot(h, wu_ref[0])
        y_ref[...] = _dot(hid.astype(BF16), wd_ref[0])

    @pl.when(i >= n_valid)
    def _():
        y_ref[...] = jnp.zeros(y_ref.shape, F32)


def _moe_expert_call(tile_expert, n_valid, idx, h2, wg, wu, wd):
    n_tiles, tm = idx.shape
    d = h2.shape[1]
    f = wg.shape[2]
    wmap = lambda i, te, nv: (te[i], 0, 0)
    grid_spec = pltpu.PrefetchScalarGridSpec(
        num_scalar_prefetch=2,
        grid=(n_tiles,),
        in_specs=[pl.BlockSpec(memory_space=pl.ANY),
                  pl.BlockSpec(memory_space=pl.ANY),
                  pl.BlockSpec((1, d, f), wmap),
                  pl.BlockSpec((1, d, f), wmap),
                  pl.BlockSpec((1, f, d), wmap)],
        out_specs=pl.BlockSpec((tm, d), lambda i, te, nv: (i, 0)),
        scratch_shapes=[pltpu.SMEM((2, tm), jnp.int32), pltpu.VMEM((2, tm, d), F32),
                        pltpu.SemaphoreType.DMA((2,)), pltpu.SemaphoreType.DMA((2,))])
    return pl.pallas_call(
        _moe_expert_kernel,
        out_shape=jax.ShapeDtypeStruct((n_tiles * tm, d), F32),
        grid_spec=grid_spec,
        compiler_params=_cparams(("arbitrary",)),
        name="moe_experts",
    )(tile_expert, n_valid, idx, h2, wg, wu, wd)


def _moe_combine_kernel(pos_hbm, y_hbm, route_ref, x1_ref, g2_ref, o_ref, idx_smem, ybuf, gsem, isem):
    i = pl.program_id(0)
    n_tiles = pl.num_programs(0)
    tm = x1_ref.shape[0]
    slot = _gather_pipeline_step(i, n_tiles, n_tiles, pos_hbm, idx_smem, isem, y_hbm, ybuf, gsem, 2 * tm)
    route = route_ref[...]
    w1, w2 = route[:, 0:1], route[:, 1:2]
    moe = w1 * ybuf[slot, :tm, :] + w2 * ybuf[slot, tm:, :]
    o_ref[...] = x1_ref[...] + g2_ref[0] * moe


def _moe_combine_call(pos, y_sorted, route, x1, g2, tokens_per_batch):
    n_tiles, two_tm = pos.shape
    tm = two_tm // 2
    t, d = x1.shape
    per_b = tokens_per_batch // tm
    tok = lambda w: pl.BlockSpec((tm, w), lambda i: (i, 0))
    return pl.pallas_call(
        _moe_combine_kernel,
        out_shape=jax.ShapeDtypeStruct((t, d), F32),
        grid=(n_tiles,),
        in_specs=[pl.BlockSpec(memory_space=pl.ANY), pl.BlockSpec(memory_space=pl.ANY),
                  tok(LANES), tok(d), pl.BlockSpec((1, 1, d), lambda i: (i // per_b, 0, 0))],
        out_specs=tok(d),
        scratch_shapes=[pltpu.SMEM((2, two_tm), jnp.int32), pltpu.VMEM((2, two_tm, d), F32),
                        pltpu.SemaphoreType.DMA((2,)), pltpu.SemaphoreType.DMA((2,))],
        compiler_params=_cparams(("arbitrary",)),
        name="moe_combine",
    )(pos, y_sorted, route, x1, g2)


def _routing_plan(e_ids, tm):
    t = e_ids.shape[0]
    flat = e_ids.reshape(-1)
    n_slots = flat.shape[0]
    n_tiles = n_slots // tm + N_EXPERTS
    onehot = (flat[:, None] == jnp.arange(N_EXPERTS, dtype=jnp.int32)[None, :]).astype(jnp.int32)
    csum = jnp.cumsum(onehot, axis=0)
    rank = jnp.take_along_axis(csum, flat[:, None], axis=1)[:, 0] - 1
    counts = csum[-1]
    tiles_e = (counts + tm - 1) // tm
    tile_end = jnp.cumsum(tiles_e)
    tile_start = tile_end - tiles_e
    pos = tile_start[flat] * tm + rank
    n_valid = tile_end[-1]
    tile_ids = jnp.arange(n_tiles, dtype=jnp.int32)
    te = jnp.minimum(jnp.searchsorted(tile_end, tile_ids, side='right'), N_EXPERTS - 1).astype(jnp.int32)
    te = jnp.where(tile_ids < n_valid, te, te[jnp.maximum(n_valid - 1, 0)])
    src = jnp.zeros((n_tiles * tm,), jnp.int32).at[pos].set(jnp.arange(n_slots, dtype=jnp.int32) // 2)
    return te, n_valid.reshape(1).astype(jnp.int32), src.reshape(n_tiles, tm), pos.reshape(t, 2).astype(jnp.int32)


def _rope_tables(n_tokens):
    rows = n_tokens // GRID_W
    row = jnp.broadcast_to(jnp.arange(rows, dtype=F32)[:, None], (rows, GRID_W)).reshape(-1)
    col = jnp.broadcast_to(jnp.arange(GRID_W, dtype=F32)[None, :], (rows, GRID_W)).reshape(-1)
    half = HEAD_DIM // 2
    inv = ROPE_BASE ** (-jnp.arange(0, half, 2, dtype=F32) / half)
    ang = jnp.stack([row[:, None] * inv, col[:, None] * inv], axis=1)
    cos, sin = jnp.cos(ang), jnp.sin(ang)
    cos64 = jnp.concatenate([cos[:, 0], cos[:, 0], cos[:, 1], cos[:, 1]], axis=1)
    sin64 = jnp.concatenate([-sin[:, 0], sin[:, 0], -sin[:, 1], sin[:, 1]], axis=1)
    return jnp.tile(cos64, (1, LANES // HEAD_DIM)), jnp.tile(sin64, (1, LANES // HEAD_DIM))


def _pick_tile(n, target):
    t = min(n, target)
    while n % t:
        t //= 2
    return t


def kernel(x, c, ctx, c_ctx, w_ada, b_ada, norm1_g, w_in, q_norm_g, k_norm_g, lambda_q1, lambda_k1, lambda_q2, lambda_k2, subln_g, ssm_a_re, ssm_a_im, ssm_log_dt, ssm_b_re, ssm_b_im, ssm_c_re, ssm_c_im, ssm_d, w_glu, b_glu, w_out, norm2_g, w_route_group, b_route_group, w_route_expert, b_route_expert, w_exp_gate, w_exp_up, w_exp_down):
    depth = w_ada.shape[0]
    assert depth == 1, "single-layer block: the context stream is never updated"
    b, n_lat, d = x.shape
    n_ctx = ctx.shape[1]
    assert n_lat % CHUNK == 0 and n_ctx % CHUNK == 0 and n_lat % GRID_W == 0
    l = 0
    lam_init = 0.8 - 0.6 * math.exp(-0.3 * l)

    rows = b + 1
    rows_pad = -(-rows // 8) * 8
    cc = jnp.concatenate([c, c_ctx[None, :], jnp.zeros((rows_pad - rows, d), F32)], axis=0)
    mod = _mod_call(cc, w_ada[l], b_ada[l])
    sh1, sc1, g1, sh2, sc2, g2 = (mod[:b, i * d:(i + 1) * d].reshape(b, 1, d) for i in range(6))
    csh1, csc1 = (mod[b:b + 1, i * d:(i + 1) * d].reshape(1, 1, d) for i in range(2))

    w_in_bf = w_in[l].astype(BF16)
    bd = jnp.kron(jnp.eye(QK_WIDTH // HEAD_DIM, dtype=F32), jnp.ones((HEAD_DIM, HEAD_DIM), F32)).astype(BF16)
    qg = jnp.tile(q_norm_g[l], LANES // HEAD_DIM).reshape(1, LANES)
    kg = jnp.tile(k_norm_g[l], LANES // HEAD_DIM).reshape(1, LANES)
    cosf, sinf = _rope_tables(n_lat)
    ones_c, zeros_c = jnp.ones((n_ctx, LANES), F32), jnp.zeros((n_ctx, LANES), F32)
    g1n = norm1_g[l].reshape(1, d)
    tm = _pick_tile(n_lat, 512)
    q_x, k_x, v_x, u_x = _inproj_call(x, sh1, sc1, g1n, w_in_bf, qg, kg, cosf, sinf, bd, tm, "inproj_lat")
    _, k_c, v_c, u_c = _inproj_call(ctx, csh1, csc1, g1n, w_in_bf, qg, kg, ones_c, zeros_c, bd,
                                    _pick_tile(n_ctx, 512), "inproj_ctx")

    e1 = jnp.exp(jnp.sum(lambda_q1[l] * lambda_k1[l]))
    e2 = jnp.exp(jnp.sum(lambda_q2[l] * lambda_k2[l]))
    lam_row = jnp.full((1, LANES), e1 - e2 + lam_init, F32)
    score_bound = math.sqrt(HEAD_DIM) * jnp.max(jnp.abs(q_norm_g[l])) * jnp.max(jnp.abs(k_norm_g[l]))

    def attn(bounded):
        return lambda *ops: _attn_call(*ops, **_attn_cfg(n_lat), out_scale=1.0 - lam_init, bounded=bounded)

    a_x = lax.cond(score_bound <= SCORE_BOUND, attn(True), attn(False),
                   lam_row, q_x, k_x, v_x, k_c, v_c, subln_g[l].reshape(1, LANES))

    win, m_op, wo, lam16 = _s5_weights(ssm_a_re[l], ssm_a_im[l], ssm_log_dt[l], ssm_b_re[l], ssm_b_im[l],
                                       ssm_c_re[l], ssm_c_im[l], ssm_d[l])
    yg = _s5_call(_to_chunk_rows(u_x), _to_chunk_rows(u_c), win, m_op, wo, lam16, b)
    yg = _from_chunk_rows(yg, b)

    wr = jnp.concatenate([w_route_group[l], w_route_expert[l]], axis=1)
    wr = jnp.pad(wr, ((0, 0), (0, LANES - wr.shape[1])))
    wr_hi, wr_lo = _split_bf16(wr)
    br = jnp.pad(jnp.concatenate([b_route_group[l], b_route_expert[l]]), (0, LANES - MOE_GROUPS - N_EXPERTS))
    x1, h2, route = _outproj_call(a_x, yg, x, g1, sh2, sc2, norm2_g[l].reshape(1, d),
                                 w_glu[l].astype(BF16), b_glu[l].reshape(1, -1), w_out[l].astype(BF16),
                                 wr_hi, wr_lo, br.reshape(1, LANES), tm)

    t_all = b * n_lat
    route = route.reshape(t_all, LANES)
    e_ids = route[:, 2:4].astype(jnp.int32)
    tile_expert, n_valid, src, pos = _routing_plan(e_ids, _pick_tile(2 * t_all, MOE_TILE))
    y_sorted = _moe_expert_call(tile_expert, n_valid, src, h2.reshape(t_all, d), w_exp_gate[l].astype(BF16),
                                w_exp_up[l].astype(BF16), w_exp_down[l].astype(BF16))
    tc = _pick_tile(n_lat, COMBINE_TILE)
    pos_tiles = jnp.transpose(pos.reshape(t_all // tc, tc, 2), (0, 2, 1)).reshape(t_all // tc, 2 * tc)
    out = _moe_combine_call(pos_tiles, y_sorted, route, x1.reshape(t_all, d), g2, n_lat)
    return out.reshape(b, n_lat, d)
```

```python
import functools
import math

import jax
import jax.numpy as jnp
from jax import lax
from jax.experimental import pallas as pl
from jax.experimental.pallas import tpu as pltpu

F32 = jnp.float32
BF16 = jnp.bfloat16

LANES = 128
HEADS = 4
HEAD_DIM = 64
QK_WIDTH = HEADS * 2 * HEAD_DIM
V_WIDTH = HEADS * 2 * HEAD_DIM
GRID_W = 64
ROPE_BASE = 10000.0
SSM_GROUP = 16
SSM_STATE = 64
CHUNK = 16
MOE_GROUPS = 4
EXPERTS_PER_GROUP = 8
N_EXPERTS = MOE_GROUPS * EXPERTS_PER_GROUP
RMS_EPS = 1e-6
ATTN_TQ = 512
ATTN_TK = 4096
SCORE_BOUND = 60.0
S5_BATCH_BLOCK = 8
MOE_TILE = 512
COMBINE_TILE = 256
VMEM_LIMIT = 48 * 1024 * 1024


def _cparams(sem):
    return pltpu.CompilerParams(dimension_semantics=sem, vmem_limit_bytes=VMEM_LIMIT)


def _split_bf16(a):
    hi = a.astype(BF16)
    lo = (a - hi.astype(F32)).astype(BF16)
    return hi, lo


def _dot(a, b):
    return jnp.dot(a, b, preferred_element_type=F32)


def _dot3(a, b):
    a_hi, a_lo = _split_bf16(a)
    b_hi, b_lo = _split_bf16(b)
    return _dot(a_hi, b_hi) + (_dot(a_hi, b_lo) + _dot(a_lo, b_hi))


def _mod_kernel(c_ref, w_ref, b_ref, o_ref):
    c = c_ref[...]
    a = c * jax.nn.sigmoid(c)
    o_ref[...] = _dot3(a, w_ref[...]) + b_ref[...]


def _mod_call(cc, w_ada, b_ada):
    rows, d = cc.shape
    n = w_ada.shape[1]
    bn = 1024
    return pl.pallas_call(
        _mod_kernel,
        out_shape=jax.ShapeDtypeStruct((rows, n), F32),
        grid=(n // bn,),
        in_specs=[pl.BlockSpec((rows, d), lambda j: (0, 0)),
                  pl.BlockSpec((d, bn), lambda j: (0, j)),
                  pl.BlockSpec((1, bn), lambda j: (0, j))],
        out_specs=pl.BlockSpec((rows, bn), lambda j: (0, j)),
        compiler_params=_cparams(("arbitrary",)),
        name="mod",
    )(cc, w_ada, b_ada.reshape(1, n))


def _inproj_kernel(x_ref, sh_ref, sc_ref, g_ref, w_ref, qg_ref, kg_ref, cos_ref, sin_ref, bd_ref,
                   q_ref, k_ref, v_ref, u_ref, us_ref):
    x = x_ref[0]
    ms = jnp.mean(x * x, axis=-1, keepdims=True)
    h = (x * lax.rsqrt(ms + RMS_EPS)) * g_ref[...]
    h = h * (1.0 + sc_ref[0]) + sh_ref[0]
    p = _dot(h.astype(BF16), w_ref[...])

    cosf = cos_ref[...]
    sinf = sin_ref[...]
    lane = lax.broadcasted_iota(jnp.int32, cosf.shape, 1)
    first_half = (lane % 32) < 16

    def norm_rope(t, gain, scale):
        ss = _dot((t * t).astype(BF16), bd_ref[...])
        t = t * lax.rsqrt(ss * (1.0 / HEAD_DIM) + RMS_EPS)
        outs = []
        for s in range(QK_WIDTH // LANES):
            ts = t[:, s * LANES:(s + 1) * LANES] * gain
            partner = jnp.where(first_half, pltpu.roll(ts, LANES - 16, 1), pltpu.roll(ts, 16, 1))
            outs.append(((ts * cosf + partner * sinf) * scale).astype(BF16))
        return jnp.concatenate(outs, axis=1)

    q_ref[0] = norm_rope(p[:, :QK_WIDTH], qg_ref[...], HEAD_DIM ** -0.5 * math.log2(math.e))
    k_ref[0] = norm_rope(p[:, QK_WIDTH:2 * QK_WIDTH], kg_ref[...], 1.0)
    v_ref[0] = p[:, 2 * QK_WIDTH:2 * QK_WIDTH + V_WIDTH].astype(BF16)
    for q in range(us_ref.shape[0]):
        lo = 2 * QK_WIDTH + V_WIDTH + q * LANES
        us_ref[q] = p[:, lo:lo + LANES]

    def store_u(pr, k, val):
        u_ref[pr, 0, :, k * LANES:(k + 1) * LANES] = val.astype(BF16)

    _token_major_to_chunk_rows(us_ref, store_u)


PAIR_W = 2 * SSM_GROUP
PAIRS_PER_TILE = LANES // PAIR_W


def _quarter_select(pieces):
    lane = lax.broadcasted_iota(jnp.int32, pieces[0].shape, 1)
    acc = pieces[0]
    for r in range(1, len(pieces)):
        acc = jnp.where(lane // PAIR_W == r, pieces[r], acc)
    return acc


def _token_major_to_chunk_rows(us_ref, store):
    n_chunk = us_ref.shape[1] // CHUNK
    n_pairs = us_ref.shape[0] * PAIRS_PER_TILE
    for k in range(CHUNK // PAIRS_PER_TILE):
        for pr in range(n_pairs):
            q, r_src = divmod(pr, PAIRS_PER_TILE)
            pieces = []
            for r in range(PAIRS_PER_TILE):
                src = us_ref[q, pl.ds(PAIRS_PER_TILE * k + r, n_chunk, stride=CHUNK), :]
                shift = ((r - r_src) % PAIRS_PER_TILE) * PAIR_W
                pieces.append(pltpu.roll(src, shift, 1) if shift else src)
            store(pr, k, _quarter_select(pieces))


def _chunk_rows_to_token_major(load, ys_ref):
    n_tiles, n_chunk = ys_ref.shape[0], ys_ref.shape[1] // CHUNK
    for t in range(CHUNK):
        k, r_src = divmod(t, PAIRS_PER_TILE)
        for q in range(n_tiles):
            pieces = []
            for r in range(PAIRS_PER_TILE):
                shift = ((r - r_src) % PAIRS_PER_TILE) * PAIR_W
                src = load(PAIRS_PER_TILE * q + r, k)
                pieces.append(pltpu.roll(src, shift, 1) if shift else src)
            ys_ref[q, pl.ds(t, n_chunk, stride=CHUNK), :] = _quarter_select(pieces)


def _inproj_call(x, sh, sc, g, w_bf, qg, kg, cosf, sinf, bd, tm, name):
    b, n, d = x.shape
    wn = w_bf.shape[1]
    per_batch = sh.shape[0] > 1
    mod_map = (lambda bi, i: (bi, 0, 0)) if per_batch else (lambda bi, i: (0, 0, 0))
    const2 = lambda bi, i: (0, 0)
    n_pairs = QK_WIDTH // PAIR_W
    outs = [jax.ShapeDtypeStruct((b, n, QK_WIDTH), BF16)] * 3 + [
        jax.ShapeDtypeStruct((n_pairs, b, n // CHUNK, CHUNK * PAIR_W), BF16)]
    tok_spec = pl.BlockSpec((1, tm, QK_WIDTH), lambda bi, i: (bi, i, 0))
    u_spec = pl.BlockSpec((n_pairs, 1, tm // CHUNK, CHUNK * PAIR_W), lambda bi, i: (0, bi, i, 0))
    return pl.pallas_call(
        _inproj_kernel,
        out_shape=outs,
        grid=(b, n // tm),
        in_specs=[pl.BlockSpec((1, tm, d), lambda bi, i: (bi, i, 0)),
                  pl.BlockSpec((1, 1, d), mod_map),
                  pl.BlockSpec((1, 1, d), mod_map),
                  pl.BlockSpec((1, d), const2),
                  pl.BlockSpec((d, wn), const2),
                  pl.BlockSpec((1, LANES), const2),
                  pl.BlockSpec((1, LANES), const2),
                  pl.BlockSpec((tm, LANES), lambda bi, i: (i, 0)),
                  pl.BlockSpec((tm, LANES), lambda bi, i: (i, 0)),
                  pl.BlockSpec((QK_WIDTH, QK_WIDTH), const2)],
        out_specs=[tok_spec] * 3 + [u_spec],
        scratch_shapes=[pltpu.VMEM((QK_WIDTH // LANES, tm, LANES), F32)],
        compiler_params=_cparams(("parallel", "parallel")),
        name=name,
    )(x, sh, sc, g, w_bf, qg, kg, cosf, sinf, bd)


def _attn_kernel(lam_ref, q_ref, kl_ref, vl_ref, kc_ref, vc_ref, sg_ref, o_ref, a1_ref, a2_ref, m1_ref, m2_ref,
                 *, tk, out_scale, bounded):
    q = q_ref[0]
    lane = lax.broadcasted_iota(jnp.int32, q.shape, 1)
    zero = jnp.zeros_like(q)
    qa = jnp.where(lane < HEAD_DIM, q, zero)
    qb = jnp.where(lane >= HEAD_DIM, q, zero)

    a1_ref[...] = jnp.zeros(a1_ref.shape, F32)
    a2_ref[...] = jnp.zeros(a2_ref.shape, F32)
    if not bounded:
        m1_ref[...] = jnp.full(m1_ref.shape, -jnp.inf, F32)
        m2_ref[...] = jnp.full(m2_ref.shape, -jnp.inf, F32)

    def ones_col(rows):
        col = lax.broadcasted_iota(jnp.int32, (rows, LANES), 1)
        return jnp.where(col == 0, 1.0, 0.0).astype(BF16)

    def one_map(qm, kc, va, a_ref, m_ref):
        s = lax.dot_general(qm, kc, (((1,), (1,)), ((), ())), preferred_element_type=F32)
        if bounded:
            a_ref[...] += _dot(jnp.exp2(s).astype(BF16), va)
        else:
            m_prev = m_ref[...]
            m_next = jnp.maximum(m_prev, jnp.max(s, axis=1, keepdims=True))
            p = jnp.exp2(s - m_next[:, :1])
            alpha = jnp.exp2(m_prev - m_next)
            a_ref[...] = jnp.concatenate([alpha, alpha], axis=1) * a_ref[...] + _dot(p.astype(BF16), va)
            m_ref[...] = m_next

    def step(kc, vc, ones):
        va = jnp.concatenate([vc, ones], axis=1)
        one_map(qa, kc, va, a1_ref, m1_ref)
        one_map(qb, kc, va, a2_ref, m2_ref)

    ones_lat = ones_col(tk)

    def lat_body(j, carry):
        off = pl.multiple_of(j * tk, tk)
        step(kl_ref[0, pl.ds(off, tk), :], vl_ref[0, pl.ds(off, tk), :], ones_lat)
        return carry

    lax.fori_loop(0, kl_ref.shape[1] // tk, lat_body, 0)
    step(kc_ref[0], vc_ref[0], ones_col(kc_ref.shape[1]))

    lam = lam_ref[...]
    a1, a2 = a1_ref[...], a2_ref[...]
    o = a1[:, :LANES] / a1[:, LANES:LANES + 1] - lam * (a2[:, :LANES] / a2[:, LANES:LANES + 1])
    ms = jnp.mean(o * o, axis=-1, keepdims=True)
    o = o * lax.rsqrt(ms + RMS_EPS) * sg_ref[...]
    o_ref[0] = (o * out_scale).astype(BF16)


def _attn_cfg(n_lat):
    return dict(tq=_pick_tile(n_lat, ATTN_TQ), tk=_pick_tile(n_lat, ATTN_TK))


def _attn_call(lam_row, q, k_lat, v_lat, k_ctx, v_ctx, sg, tq, tk, out_scale=1.0, bounded=True):
    b, n, _ = q.shape
    nc = k_ctx.shape[1]
    kv_lat = pl.BlockSpec((1, n, LANES), lambda bi, h, i: (bi, 0, h))
    kv_ctx = pl.BlockSpec((1, nc, LANES), lambda bi, h, i: (bi, 0, h))
    q_spec = pl.BlockSpec((1, tq, LANES), lambda bi, h, i: (bi, i, h))
    row = pl.BlockSpec((1, LANES), lambda bi, h, i: (0, 0))
    acc = pltpu.VMEM((tq, 2 * LANES), F32)
    run_max = pltpu.VMEM((tq, LANES), F32)
    return pl.pallas_call(
        functools.partial(_attn_kernel, tk=tk, out_scale=out_scale, bounded=bounded),
        out_shape=jax.ShapeDtypeStruct((b, n, V_WIDTH), BF16),
        grid=(b, HEADS, n // tq),
        in_specs=[row, q_spec, kv_lat, kv_lat, kv_ctx, kv_ctx, row],
        out_specs=q_spec,
        scratch_shapes=[acc, acc, run_max, run_max],
        compiler_params=_cparams(("parallel", "parallel", "arbitrary")),
        name="attn" if bounded else "attn_general",
    )(lam_row, q, k_lat, v_lat, k_ctx, v_ctx, sg)


def _s5_kernel(ul_ref, uc_ref, win_ref, m_ref, wo_ref, lam_ref, o_ref, xl_ref, xc_ref, s_ref, *, nb):
    ul = ul_ref[0]
    n_slab = xl_ref.shape[0]

    def to_slabs(x_ref, x):
        for k in range(n_slab):
            x_ref[k] = x[:, k * LANES:(k + 1) * LANES]

    to_slabs(xl_ref, _dot(ul, win_ref[0]))
    to_slabs(xc_ref, _dot(uc_ref[0], win_ref[0]))
    n_lat = ul.shape[0] // nb
    n_ctx = uc_ref.shape[1] // nb

    lam = lam_ref[0]
    lfr, lfi, lbr, lbi = (jnp.broadcast_to(lam[i:i + 1], (nb, LANES)) for i in range(4))

    def rows(c, n_chunks):
        return pl.ds(c, nb, stride=n_chunks)

    def advance(x_ref, c, n_chunks, slab, ar, ai, sr, si):
        xr = x_ref[slab, rows(c, n_chunks), :]
        xi = x_ref[slab + 1, rows(c, n_chunks), :]
        return ar * sr - ai * si + xr, ar * si + ai * sr + xi

    def ctx_body(i, carry):
        fr, fi, br, bi = carry
        fr, fi = advance(xc_ref, i, n_ctx, 0, lfr, lfi, fr, fi)
        br, bi = advance(xc_ref, n_ctx - 1 - i, n_ctx, 2, lbr, lbi, br, bi)
        return fr, fi, br, bi

    def lat_body(i, carry):
        fr, fi, br, bi = carry
        cb = n_lat - 1 - i
        s_ref[0, rows(i, n_lat), :] = fr
        s_ref[1, rows(i, n_lat), :] = fi
        s_ref[2, rows(cb, n_lat), :] = br
        s_ref[3, rows(cb, n_lat), :] = bi
        fr, fi = advance(xl_ref, i, n_lat, 0, lfr, lfi, fr, fi)
        br, bi = advance(xl_ref, cb, n_lat, 2, lbr, lbi, br, bi)
        return fr, fi, br, bi

    z = jnp.zeros((nb, LANES), F32)
    carry = lax.fori_loop(0, n_ctx, ctx_body, (z, z, z, z))
    lax.fori_loop(0, n_lat, lat_body, carry)

    s_in = jnp.concatenate([s_ref[k].astype(BF16) for k in range(n_slab)], axis=1)
    y = _dot(ul, m_ref[0]) + _dot(s_in, wo_ref[0])
    o_ref[0] = jax.nn.gelu(y).astype(BF16)


def _s5_call(u_lat, u_ctx, win, m, wo, lam16, b):
    npair, rows_lat, kw = u_lat.shape
    nb = S5_BATCH_BLOCK if b % S5_BATCH_BLOCK == 0 else b
    rl = rows_lat // b * nb
    rc = u_ctx.shape[1] // b * nb
    wspec = pl.BlockSpec((1, kw, kw), lambda p, h: (p, 0, 0))
    slabs = lambda r: pltpu.VMEM((kw // LANES, r, LANES), F32)
    return pl.pallas_call(
        functools.partial(_s5_kernel, nb=nb),
        out_shape=jax.ShapeDtypeStruct((npair, rows_lat, kw), BF16),
        grid=(npair, b // nb),
        in_specs=[pl.BlockSpec((1, rl, kw), lambda p, h: (p, h, 0)),
                  pl.BlockSpec((1, rc, kw), lambda p, h: (p, h, 0)),
                  wspec, wspec, wspec,
                  pl.BlockSpec((1, 4, LANES), lambda p, h: (p, 0, 0))],
        out_specs=pl.BlockSpec((1, rl, kw), lambda p, h: (p, h, 0)),
        scratch_shapes=[slabs(rl), slabs(rc), slabs(rl)],
        compiler_params=_cparams(("parallel", "parallel")),
        name="s5",
    )(u_lat, u_ctx, win, m, wo, lam16)


def _s5_weights(a_re, a_im, log_dt, b_re, b_im, c_re, c_im, d_skip):
    hp = lax.Precision.HIGHEST
    g_n, p_n = a_re.shape[1], a_re.shape[2]
    t_n = CHUNK
    dt = jnp.exp(log_dt.astype(F32))[..., None]
    ar, ai = a_re.astype(F32), a_im.astype(F32)
    mag = jnp.exp(ar * dt)
    lr, li = mag * jnp.cos(ai * dt), mag * jnp.sin(ai * dt)
    den = ar * ar + ai * ai
    nr, ni = lr - 1.0, li
    cr = (nr * ar + ni * ai) / den
    ci = (ni * ar - nr * ai) / den
    bbr = cr[..., None] * b_re - ci[..., None] * b_im
    bbi = cr[..., None] * b_im + ci[..., None] * b_re
    n = jnp.arange(t_n + 1, dtype=F32)[:, None, None, None]
    pm = jnp.exp(n * (ar * dt))
    pw_r, pw_i = pm * jnp.cos(n * (ai * dt)), pm * jnp.sin(n * (ai * dt))
    lb_r = pw_r[:t_n, ..., None] * bbr - pw_i[:t_n, ..., None] * bbi
    lb_i = pw_r[:t_n, ..., None] * bbi + pw_i[:t_n, ..., None] * bbr
    cre, cim = c_re.astype(F32), c_im.astype(F32)
    kern = (jnp.einsum('gip,tdgpj->tdgij', cre, lb_r, precision=hp)
            - jnp.einsum('gip,tdgpj->tdgij', cim, lb_i, precision=hp))
    s_idx = jnp.arange(t_n)[:, None]
    t_idx = jnp.arange(t_n)[None, :]
    tau = t_idx - s_idx
    mf = jnp.where((tau >= 0)[..., None, None, None], kern[jnp.clip(tau, 0, t_n - 1), 0], 0.0)
    mb = jnp.where((tau <= 0)[..., None, None, None], kern[jnp.clip(-tau, 0, t_n - 1), 1], 0.0)
    hh = SSM_GROUP
    skip = (jnp.eye(t_n, dtype=F32)[:, :, None, None, None] * jnp.eye(hh, dtype=F32)[None, None, None]
            * d_skip.astype(F32)[None, None, :, :, None])
    m_g = jnp.transpose(mf + mb + skip, (2, 0, 4, 1, 3))

    def to_rows(w):
        return jnp.transpose(w, (1, 0, 3, 2))

    win4 = jnp.stack([to_rows(lb_r[::-1, 0]), to_rows(lb_i[::-1, 0]),
                      to_rows(lb_r[:, 1]), to_rows(lb_i[:, 1])], axis=1)

    def to_cols(zr, zi):
        o_re = cre[None] * zr[:, :, None, :] - cim[None] * zi[:, :, None, :]
        o_im = -(cre[None] * zi[:, :, None, :] + cim[None] * zr[:, :, None, :])
        f = lambda o: jnp.transpose(o, (1, 3, 0, 2))
        return f(o_re), f(o_im)

    of_re, of_im = to_cols(pw_r[1:, 0], pw_i[1:, 0])
    ob_re, ob_im = to_cols(pw_r[1:, 1][::-1], pw_i[1:, 1][::-1])
    wo4 = jnp.stack([of_re, of_im, ob_re, ob_im], axis=1)

    npair = g_n // 2
    eye2 = jnp.eye(2, dtype=F32)
    kw = 2 * t_n * hh
    m_pair = jnp.einsum('ab,pasjti->psajtbi', eye2,
                        m_g.reshape(npair, 2, t_n, hh, t_n, hh)).reshape(npair, kw, kw)
    win_pair = jnp.einsum('ab,paksjq->psajkbq', eye2,
                          win4.reshape(npair, 2, 4, t_n, hh, p_n)).reshape(npair, kw, 8 * p_n)
    wo_pair = jnp.einsum('ab,pakqti->pkaqtbi', eye2,
                         wo4.reshape(npair, 2, 4, p_n, t_n, hh)).reshape(npair, 8 * p_n, kw)
    lam16 = jnp.stack([pw_r[t_n, 0], pw_i[t_n, 0], pw_r[t_n, 1], pw_i[t_n, 1]], axis=1)
    lam16 = jnp.transpose(lam16.reshape(npair, 2, 4, p_n), (0, 2, 1, 3)).reshape(npair, 4, 2 * p_n)
    return win_pair.astype(BF16), m_pair.astype(BF16), wo_pair.astype(BF16), lam16


def _outproj_kernel(a_ref, y_ref, x_ref, g1_ref, sh_ref, sc_ref, n2_ref, wglu_ref, bglu_ref, wout_ref,
                    wr_hi_ref, wr_lo_ref, br_ref, x1_ref, h2_ref, gate_ref, ys_ref):
    _chunk_rows_to_token_major(
        lambda pr, k: y_ref[pr, 0, :, k * LANES:(k + 1) * LANES].astype(F32), ys_ref)
    yf = jnp.concatenate([ys_ref[q] for q in range(ys_ref.shape[0])], axis=1)
    y = yf.astype(BF16)
    z = _dot(y, wglu_ref[...]) + bglu_ref[...]
    s = (yf * jax.nn.sigmoid(z)).astype(BF16)
    half = a_ref.shape[2]
    o = _dot(a_ref[0], wout_ref[:half, :]) + _dot(s, wout_ref[half:, :])
    x1 = x_ref[0] + g1_ref[0] * o
    x1_ref[0] = x1

    ms = jnp.mean(x1 * x1, axis=-1, keepdims=True)
    h = (x1 * lax.rsqrt(ms + RMS_EPS)) * n2_ref[...]
    h = h * (1.0 + sc_ref[0]) + sh_ref[0]
    h2_ref[0] = h

    h_hi, h_lo = _split_bf16(h)
    lg = _dot(h_hi, wr_hi_ref[...]) + (_dot(h_hi, wr_lo_ref[...]) + _dot(h_lo, wr_hi_ref[...])) + br_ref[...]
    lane = lax.broadcasted_iota(jnp.int32, lg.shape, 1)
    neg = jnp.float32(-jnp.inf)
    big = jnp.int32(LANES)

    def top1(vals):
        vmax = jnp.max(vals, axis=1, keepdims=True)
        idx = jnp.min(jnp.where(vals == vmax, lane, big), axis=1, keepdims=True)
        return vmax, idx

    is_grp = lane < MOE_GROUPS
    g_vals = jnp.where(is_grp, lg, neg)
    g_max, g_idx = top1(g_vals)
    p_grp = 1.0 / jnp.sum(jnp.where(is_grp, jnp.exp(g_vals - g_max), 0.0), axis=1, keepdims=True)
    e_lo = MOE_GROUPS + EXPERTS_PER_GROUP * g_idx
    in_grp = (lane >= e_lo) & (lane < e_lo + EXPERTS_PER_GROUP)
    e_vals = jnp.where(in_grp, lg, neg)
    v1, i1 = top1(e_vals)
    v2, i2 = top1(jnp.where(lane == i1, neg, e_vals))
    r = jnp.exp(v2 - v1)
    w1 = p_grp / (1.0 + r)
    w2 = w1 * r
    e1 = (i1 - MOE_GROUPS).astype(F32)
    e2 = (i2 - MOE_GROUPS).astype(F32)
    gate_ref[0] = (jnp.where(lane == 0, w1, 0.0) + jnp.where(lane == 1, w2, 0.0)
                   + jnp.where(lane == 2, e1, 0.0) + jnp.where(lane == 3, e2, 0.0))


def _outproj_call(a, yg, x, g1, sh2, sc2, n2g, wglu, bglu, wout, wr_hi, wr_lo, br, tm):
    b, n, d = x.shape
    half = a.shape[2]
    tok = lambda w: pl.BlockSpec((1, tm, w), lambda bi, i: (bi, i, 0))
    mod = pl.BlockSpec((1, 1, d), lambda bi, i: (bi, 0, 0))
    const = lambda r, c: pl.BlockSpec((r, c), lambda bi, i: (0, 0))
    return pl.pallas_call(
        _outproj_kernel,
        out_shape=[jax.ShapeDtypeStruct((b, n, d), F32), jax.ShapeDtypeStruct((b, n, d), F32),
                   jax.ShapeDtypeStruct((b, n, LANES), F32)],
        grid=(b, n // tm),
        in_specs=[tok(half),
                  pl.BlockSpec((yg.shape[0], 1, tm // CHUNK, yg.shape[3]), lambda bi, i: (0, bi, i, 0)),
                  tok(d), mod, mod, mod, const(1, d),
                  const(half, half), const(1, half), const(d, d),
                  const(d, LANES), const(d, LANES), const(1, LANES)],
        out_specs=[tok(d), tok(d), tok(LANES)],
        scratch_shapes=[pltpu.VMEM((half // LANES, tm, LANES), F32)],
        compiler_params=_cparams(("parallel", "parallel")),
        name="outproj",
    )(a, yg, x, g1, sh2, sc2, n2g, wglu, bglu, wout, wr_hi, wr_lo, br)


def _row_gather_start(idx_smem, slot, src_hbm, dst, sem, nrows):
    def body(r, carry):
        tok = idx_smem[slot, r]
        pltpu.make_async_copy(src_hbm.at[pl.ds(tok, 1), :], dst.at[slot, pl.ds(r, 1), :], sem.at[slot]).start()
        return carry
    lax.fori_loop(0, nrows, body, 0, unroll=8)


def _row_gather_wait(src_hbm, dst, sem, slot, nrows):
    pltpu.make_async_copy(src_hbm.at[pl.ds(0, nrows), :], dst.at[slot], sem.at[slot]).wait()


def _idx_copy(idx_hbm, idx_smem, isem, tile, slot):
    return pltpu.make_async_copy(idx_hbm.at[pl.ds(tile, 1), :], idx_smem.at[pl.ds(slot, 1), :], isem.at[slot])


def _gather_pipeline_step(i, n_tiles, n_valid, idx_hbm, idx_smem, isem, src_hbm, buf, gsem, nrows):
    slot = i % 2
    nxt = 1 - slot

    @pl.when(i == 0)
    def _():
        first = _idx_copy(idx_hbm, idx_smem, isem, 0, 0)
        first.start()
        first.wait()
        _row_gather_start(idx_smem, 0, src_hbm, buf, gsem, nrows)

        @pl.when(n_tiles > 1)
        def _():
            _idx_copy(idx_hbm, idx_smem, isem, 1, 1).start()

    @pl.when(i + 1 < n_tiles)
    def _():
        _idx_copy(idx_hbm, idx_smem, isem, i + 1, nxt).wait()

        @pl.when(i + 1 < n_valid)
        def _():
            _row_gather_start(idx_smem, nxt, src_hbm, buf, gsem, nrows)

        @pl.when(i + 2 < n_tiles)
        def _():
            _idx_copy(idx_hbm, idx_smem, isem, i + 2, slot).start()

    @pl.when(i < n_valid)
    def _():
        _row_gather_wait(src_hbm, buf, gsem, slot, nrows)

    return slot


def _moe_expert_kernel(te_ref, nv_ref, idx_hbm, h_hbm, wg_ref, wu_ref, wd_ref, y_ref,
                       idx_smem, hbuf, gsem, isem):
    i = pl.program_id(0)
    n_valid = nv_ref[0]
    tm = hbuf.shape[1]
    slot = _gather_pipeline_step(i, pl.num_programs(0), n_valid, idx_hbm, idx_smem, isem, h_hbm, hbuf, gsem, tm)

    @pl.when(i < n_valid)
    def _():
        h = hbuf[slot].astype(BF16)
        hid = jax.nn.silu(_dot(h, wg_ref[0])) * _dot(h, wu_ref[0])
        y_ref[...] = _dot(hid.astype(BF16), wd_ref[0])

    @pl.when(i >= n_valid)
    def _():
        y_ref[...] = jnp.zeros(y_ref.shape, F32)


def _moe_expert_call(tile_expert, n_valid, idx, h2, wg, wu, wd):
    n_tiles, tm = idx.shape
    d = h2.shape[1]
    f = wg.shape[2]
    wmap = lambda i, te, nv: (te[i], 0, 0)
    grid_spec = pltpu.PrefetchScalarGridSpec(
        num_scalar_prefetch=2,
        grid=(n_tiles,),
        in_specs=[pl.BlockSpec(memory_space=pl.ANY),
                  pl.BlockSpec(memory_space=pl.ANY),
                  pl.BlockSpec((1, d, f), wmap),
                  pl.BlockSpec((1, d, f), wmap),
                  pl.BlockSpec((1, f, d), wmap)],
        out_specs=pl.BlockSpec((tm, d), lambda i, te, nv: (i, 0)),
        scratch_shapes=[pltpu.SMEM((2, tm), jnp.int32), pltpu.VMEM((2, tm, d), F32),
                        pltpu.SemaphoreType.DMA((2,)), pltpu.SemaphoreType.DMA((2,))])
    return pl.pallas_call(
        _moe_expert_kernel,
        out_shape=jax.ShapeDtypeStruct((n_tiles * tm, d), F32),
        grid_spec=grid_spec,
        compiler_params=_cparams(("arbitrary",)),
        name="moe_experts",
    )(tile_expert, n_valid, idx, h2, wg, wu, wd)


def _moe_combine_kernel(pos_hbm, y_hbm, route_ref, x1_ref, g2_ref, o_ref, idx_smem, ybuf, gsem, isem):
    i = pl.program_id(0)
    n_tiles = pl.num_programs(0)
    tm = x1_ref.shape[0]
    slot = _gather_pipeline_step(i, n_tiles, n_tiles, pos_hbm, idx_smem, isem, y_hbm, ybuf, gsem, 2 * tm)
    route = route_ref[...]
    w1, w2 = route[:, 0:1], route[:, 1:2]
    moe = w1 * ybuf[slot, :tm, :] + w2 * ybuf[slot, tm:, :]
    o_ref[...] = x1_ref[...] + g2_ref[0] * moe


def _moe_combine_call(pos, y_sorted, route, x1, g2, tokens_per_batch):
    n_tiles, two_tm = pos.shape
    tm = two_tm // 2
    t, d = x1.shape
    per_b = tokens_per_batch // tm
    tok = lambda w: pl.BlockSpec((tm, w), lambda i: (i, 0))
    return pl.pallas_call(
        _moe_combine_kernel,
        out_shape=jax.ShapeDtypeStruct((t, d), F32),
        grid=(n_tiles,),
        in_specs=[pl.BlockSpec(memory_space=pl.ANY), pl.BlockSpec(memory_space=pl.ANY),
                  tok(LANES), tok(d), pl.BlockSpec((1, 1, d), lambda i: (i // per_b, 0, 0))],
        out_specs=tok(d),
        scratch_shapes=[pltpu.SMEM((2, two_tm), jnp.int32), pltpu.VMEM((2, two_tm, d), F32),
                        pltpu.SemaphoreType.DMA((2,)), pltpu.SemaphoreType.DMA((2,))],
        compiler_params=_cparams(("arbitrary",)),
        name="moe_combine",
    )(pos, y_sorted, route, x1, g2)


def _routing_plan(e_ids, tm):
    t = e_ids.shape[0]
    flat = e_ids.reshape(-1)
    n_slots = flat.shape[0]
    n_tiles = n_slots // tm + N_EXPERTS
    onehot = (flat[:, None] == jnp.arange(N_EXPERTS, dtype=jnp.int32)[None, :]).astype(jnp.int32)
    csum = jnp.cumsum(onehot, axis=0)
    rank = jnp.take_along_axis(csum, flat[:, None], axis=1)[:, 0] - 1
    counts = csum[-1]
    tiles_e = (counts + tm - 1) // tm
    tile_end = jnp.cumsum(tiles_e)
    tile_start = tile_end - tiles_e
    pos = tile_start[flat] * tm + rank
    n_valid = tile_end[-1]
    tile_ids = jnp.arange(n_tiles, dtype=jnp.int32)
    te = jnp.minimum(jnp.searchsorted(tile_end, tile_ids, side='right'), N_EXPERTS - 1).astype(jnp.int32)
    te = jnp.where(tile_ids < n_valid, te, te[jnp.maximum(n_valid - 1, 0)])
    src = jnp.zeros((n_tiles * tm,), jnp.int32).at[pos].set(jnp.arange(n_slots, dtype=jnp.int32) // 2)
    return te, n_valid.reshape(1).astype(jnp.int32), src.reshape(n_tiles, tm), pos.reshape(t, 2).astype(jnp.int32)


def _rope_tables(n_tokens):
    rows = n_tokens // GRID_W
    row = jnp.broadcast_to(jnp.arange(rows, dtype=F32)[:, None], (rows, GRID_W)).reshape(-1)
    col = jnp.broadcast_to(jnp.arange(GRID_W, dtype=F32)[None, :], (rows, GRID_W)).reshape(-1)
    half = HEAD_DIM // 2
    inv = ROPE_BASE ** (-jnp.arange(0, half, 2, dtype=F32) / half)
    ang = jnp.stack([row[:, None] * inv, col[:, None] * inv], axis=1)
    cos, sin = jnp.cos(ang), jnp.sin(ang)
    cos64 = jnp.concatenate([cos[:, 0], cos[:, 0], cos[:, 1], cos[:, 1]], axis=1)
    sin64 = jnp.concatenate([-sin[:, 0], sin[:, 0], -sin[:, 1], sin[:, 1]], axis=1)
    return jnp.tile(cos64, (1, LANES // HEAD_DIM)), jnp.tile(sin64, (1, LANES // HEAD_DIM))


def _pick_tile(n, target):
    t = min(n, target)
    while n % t:
        t //= 2
    return t


def kernel(x, c, ctx, c_ctx, w_ada, b_ada, norm1_g, w_in, q_norm_g, k_norm_g, lambda_q1, lambda_k1, lambda_q2, lambda_k2, subln_g, ssm_a_re, ssm_a_im, ssm_log_dt, ssm_b_re, ssm_b_im, ssm_c_re, ssm_c_im, ssm_d, w_glu, b_glu, w_out, norm2_g, w_route_group, b_route_group, w_route_expert, b_route_expert, w_exp_gate, w_exp_up, w_exp_down):
    depth = w_ada.shape[0]
    assert depth == 1, "single-layer block: the context stream is never updated"
    b, n_lat, d = x.shape
    n_ctx = ctx.shape[1]
    assert n_lat % CHUNK == 0 and n_ctx % CHUNK == 0 and n_lat % GRID_W == 0
    l = 0
    lam_init = 0.8 - 0.6 * math.exp(-0.3 * l)

    rows = b + 1
    rows_pad = -(-rows // 8) * 8
    cc = jnp.concatenate([c, c_ctx[None, :], jnp.zeros((rows_pad - rows, d), F32)], axis=0)
    mod = _mod_call(cc, w_ada[l], b_ada[l])
    sh1, sc1, g1, sh2, sc2, g2 = (mod[:b, i * d:(i + 1) * d].reshape(b, 1, d) for i in range(6))
    csh1, csc1 = (mod[b:b + 1, i * d:(i + 1) * d].reshape(1, 1, d) for i in range(2))

    w_in_bf = w_in[l].astype(BF16)
    bd = jnp.kron(jnp.eye(QK_WIDTH // HEAD_DIM, dtype=F32), jnp.ones((HEAD_DIM, HEAD_DIM), F32)).astype(BF16)
    qg = jnp.tile(q_norm_g[l], LANES // HEAD_DIM).reshape(1, LANES)
    kg = jnp.tile(k_norm_g[l], LANES // HEAD_DIM).reshape(1, LANES)
    cosf, sinf = _rope_tables(n_lat)
    ones_c, zeros_c = jnp.ones((n_ctx, LANES), F32), jnp.zeros((n_ctx, LANES), F32)
    g1n = norm1_g[l].reshape(1, d)
    tm = _pick_tile(n_lat, 512)
    q_x, k_x, v_x, u_x = _inproj_call(x, sh1, sc1, g1n, w_in_bf, qg, kg, cosf, sinf, bd, tm, "inproj_lat")
    _, k_c, v_c, u_c = _inproj_call(ctx, csh1, csc1, g1n, w_in_bf, qg, kg, ones_c, zeros_c, bd,
                                    _pick_tile(n_ctx, 512), "inproj_ctx")

    e1 = jnp.exp(jnp.sum(lambda_q1[l] * lambda_k1[l]))
    e2 = jnp.exp(jnp.sum(lambda_q2[l] * lambda_k2[l]))
    lam_row = jnp.full((1, LANES), e1 - e2 + lam_init, F32)
    score_bound = math.sqrt(HEAD_DIM) * jnp.max(jnp.abs(q_norm_g[l])) * jnp.max(jnp.abs(k_norm_g[l]))

    def attn(bounded):
        return lambda *ops: _attn_call(*ops, **_attn_cfg(n_lat), out_scale=1.0 - lam_init, bounded=bounded)

    a_x = lax.cond(score_bound <= SCORE_BOUND, attn(True), attn(False),
                   lam_row, q_x, k_x, v_x, k_c, v_c, subln_g[l].reshape(1, LANES))

    win, m_op, wo, lam16 = _s5_weights(ssm_a_re[l], ssm_a_im[l], ssm_log_dt[l], ssm_b_re[l], ssm_b_im[l],
                                       ssm_c_re[l], ssm_c_im[l], ssm_d[l])
    n_pairs, kw = u_x.shape[0], u_x.shape[3]
    yg = _s5_call(u_x.reshape(n_pairs, b * (n_lat // CHUNK), kw), u_c.reshape(n_pairs, b * (n_ctx // CHUNK), kw),
                  win, m_op, wo, lam16, b)
    yg = yg.reshape(n_pairs, b, n_lat // CHUNK, kw)

    wr = jnp.concatenate([w_route_group[l], w_route_expert[l]], axis=1)
    wr = jnp.pad(wr, ((0, 0), (0, LANES - wr.shape[1])))
    wr_hi, wr_lo = _split_bf16(wr)
    br = jnp.pad(jnp.concatenate([b_route_group[l], b_route_expert[l]]), (0, LANES - MOE_GROUPS - N_EXPERTS))
    x1, h2, route = _outproj_call(a_x, yg, x, g1, sh2, sc2, norm2_g[l].reshape(1, d),
                                 w_glu[l].astype(BF16), b_glu[l].reshape(1, -1), w_out[l].astype(BF16),
                                 wr_hi, wr_lo, br.reshape(1, LANES), tm)

    t_all = b * n_lat
    route = route.reshape(t_all, LANES)
    e_ids = route[:, 2:4].astype(jnp.int32)
    tile_expert, n_valid, src, pos = _routing_plan(e_ids, _pick_tile(2 * t_all, MOE_TILE))
    y_sorted = _moe_expert_call(tile_expert, n_valid, src, h2.reshape(t_all, d), w_exp_gate[l].astype(BF16),
                                w_exp_up[l].astype(BF16), w_exp_down[l].astype(BF16))
    tc = _pick_tile(n_lat, COMBINE_TILE)
    pos_tiles = jnp.transpose(pos.reshape(t_all // tc, tc, 2), (0, 2, 1)).reshape(t_all // tc, 2 * tc)
    out = _moe_combine_call(pos_tiles, y_sorted, route, x1.reshape(t_all, d), g2, n_lat)
    return out.reshape(b, n_lat, d)
```

```python
import functools
import math

import jax
import jax.numpy as jnp
from jax import lax
from jax.experimental import pallas as pl
from jax.experimental.pallas import tpu as pltpu

F32 = jnp.float32
BF16 = jnp.bfloat16

LANES = 128
HEADS = 4
HEAD_DIM = 64
QK_WIDTH = HEADS * 2 * HEAD_DIM
V_WIDTH = HEADS * 2 * HEAD_DIM
GRID_W = 64
ROPE_BASE = 10000.0
SSM_GROUP = 16
SSM_STATE = 64
CHUNK = 16
MOE_GROUPS = 4
EXPERTS_PER_GROUP = 8
N_EXPERTS = MOE_GROUPS * EXPERTS_PER_GROUP
RMS_EPS = 1e-6
ATTN_TQ = 512
ATTN_TK = 4096
SCORE_BOUND = 60.0
S5_BATCH_BLOCK = 8
PAIRS_PER_GROUP = EXPERTS_PER_GROUP * (EXPERTS_PER_GROUP - 1) // 2
N_CLASSES = MOE_GROUPS * PAIRS_PER_GROUP
ROW_TILE = 8
MOE_TILE = 256
PUSH_UNROLL = 8
COMBINE_TILE = 512
VMEM_LIMIT = 48 * 1024 * 1024


def _cparams(sem):
    return pltpu.CompilerParams(dimension_semantics=sem, vmem_limit_bytes=VMEM_LIMIT)


def _split_bf16(a):
    hi = a.astype(BF16)
    lo = (a - hi.astype(F32)).astype(BF16)
    return hi, lo


def _dot(a, b):
    return jnp.dot(a, b, preferred_element_type=F32)


def _dot3(a, b):
    a_hi, a_lo = _split_bf16(a)
    b_hi, b_lo = _split_bf16(b)
    return _dot(a_hi, b_hi) + (_dot(a_hi, b_lo) + _dot(a_lo, b_hi))


def _mod_kernel(c_ref, w_ref, b_ref, o_ref):
    c = c_ref[...]
    a = c * jax.nn.sigmoid(c)
    o_ref[...] = _dot3(a, w_ref[...]) + b_ref[...]


def _mod_call(cc, w_ada, b_ada):
    rows, d = cc.shape
    n = w_ada.shape[1]
    bn = 1024
    return pl.pallas_call(
        _mod_kernel,
        out_shape=jax.ShapeDtypeStruct((rows, n), F32),
        grid=(n // bn,),
        in_specs=[pl.BlockSpec((rows, d), lambda j: (0, 0)),
                  pl.BlockSpec((d, bn), lambda j: (0, j)),
                  pl.BlockSpec((1, bn), lambda j: (0, j))],
        out_specs=pl.BlockSpec((rows, bn), lambda j: (0, j)),
        compiler_params=_cparams(("arbitrary",)),
        name="mod",
    )(cc, w_ada, b_ada.reshape(1, n))


def _inproj_kernel(x_ref, sh_ref, sc_ref, g_ref, w_ref, qg_ref, kg_ref, cos_ref, sin_ref, bd_ref,
                   q_ref, k_ref, v_ref, u_ref, us_ref):
    x = x_ref[0]
    ms = jnp.mean(x * x, axis=-1, keepdims=True)
    h = (x * lax.rsqrt(ms + RMS_EPS)) * g_ref[...]
    h = h * (1.0 + sc_ref[0]) + sh_ref[0]
    p = _dot(h.astype(BF16), w_ref[...])

    cosf = cos_ref[...]
    sinf = sin_ref[...]
    lane = lax.broadcasted_iota(jnp.int32, cosf.shape, 1)
    first_half = (lane % 32) < 16

    def norm_rope(t, gain, scale):
        ss = _dot((t * t).astype(BF16), bd_ref[...])
        t = t * lax.rsqrt(ss * (1.0 / HEAD_DIM) + RMS_EPS)
        outs = []
        for s in range(QK_WIDTH // LANES):
            ts = t[:, s * LANES:(s + 1) * LANES] * gain
            partner = jnp.where(first_half, pltpu.roll(ts, LANES - 16, 1), pltpu.roll(ts, 16, 1))
            outs.append(((ts * cosf + partner * sinf) * scale).astype(BF16))
        return jnp.concatenate(outs, axis=1)

    q_ref[0] = norm_rope(p[:, :QK_WIDTH], qg_ref[...], HEAD_DIM ** -0.5 * math.log2(math.e))
    k_ref[0] = norm_rope(p[:, QK_WIDTH:2 * QK_WIDTH], kg_ref[...], 1.0)
    v_ref[0] = p[:, 2 * QK_WIDTH:2 * QK_WIDTH + V_WIDTH].astype(BF16)
    for q in range(us_ref.shape[0]):
        lo = 2 * QK_WIDTH + V_WIDTH + q * LANES
        us_ref[q] = p[:, lo:lo + LANES]

    def store_u(pr, k, val):
        u_ref[pr, 0, :, k * LANES:(k + 1) * LANES] = val.astype(BF16)

    _token_major_to_chunk_rows(us_ref, store_u)


PAIR_W = 2 * SSM_GROUP
PAIRS_PER_TILE = LANES // PAIR_W


def _quarter_select(pieces):
    lane = lax.broadcasted_iota(jnp.int32, pieces[0].shape, 1)
    acc = pieces[0]
    for r in range(1, len(pieces)):
        acc = jnp.where(lane // PAIR_W == r, pieces[r], acc)
    return acc


def _token_major_to_chunk_rows(us_ref, store):
    n_chunk = us_ref.shape[1] // CHUNK
    n_pairs = us_ref.shape[0] * PAIRS_PER_TILE
    for k in range(CHUNK // PAIRS_PER_TILE):
        for pr in range(n_pairs):
            q, r_src = divmod(pr, PAIRS_PER_TILE)
            pieces = []
            for r in range(PAIRS_PER_TILE):
                src = us_ref[q, pl.ds(PAIRS_PER_TILE * k + r, n_chunk, stride=CHUNK), :]
                shift = ((r - r_src) % PAIRS_PER_TILE) * PAIR_W
                pieces.append(pltpu.roll(src, shift, 1) if shift else src)
            store(pr, k, _quarter_select(pieces))


def _chunk_rows_to_token_major(load, ys_ref):
    n_tiles, n_chunk = ys_ref.shape[0], ys_ref.shape[1] // CHUNK
    for t in range(CHUNK):
        k, r_src = divmod(t, PAIRS_PER_TILE)
        for q in range(n_tiles):
            pieces = []
            for r in range(PAIRS_PER_TILE):
                shift = ((r - r_src) % PAIRS_PER_TILE) * PAIR_W
                src = load(PAIRS_PER_TILE * q + r, k)
                pieces.append(pltpu.roll(src, shift, 1) if shift else src)
            ys_ref[q, pl.ds(t, n_chunk, stride=CHUNK), :] = _quarter_select(pieces)


def _inproj_call(x, sh, sc, g, w_bf, qg, kg, cosf, sinf, bd, tm, name):
    b, n, d = x.shape
    wn = w_bf.shape[1]
    per_batch = sh.shape[0] > 1
    mod_map = (lambda bi, i: (bi, 0, 0)) if per_batch else (lambda bi, i: (0, 0, 0))
    const2 = lambda bi, i: (0, 0)
    n_pairs = QK_WIDTH // PAIR_W
    outs = [jax.ShapeDtypeStruct((b, n, QK_WIDTH), BF16)] * 3 + [
        jax.ShapeDtypeStruct((n_pairs, b, n // CHUNK, CHUNK * PAIR_W), BF16)]
    tok_spec = pl.BlockSpec((1, tm, QK_WIDTH), lambda bi, i: (bi, i, 0))
    u_spec = pl.BlockSpec((n_pairs, 1, tm // CHUNK, CHUNK * PAIR_W), lambda bi, i: (0, bi, i, 0))
    return pl.pallas_call(
        _inproj_kernel,
        out_shape=outs,
        grid=(b, n // tm),
        in_specs=[pl.BlockSpec((1, tm, d), lambda bi, i: (bi, i, 0)),
                  pl.BlockSpec((1, 1, d), mod_map),
                  pl.BlockSpec((1, 1, d), mod_map),
                  pl.BlockSpec((1, d), const2),
                  pl.BlockSpec((d, wn), const2),
                  pl.BlockSpec((1, LANES), const2),
                  pl.BlockSpec((1, LANES), const2),
                  pl.BlockSpec((tm, LANES), lambda bi, i: (i, 0)),
                  pl.BlockSpec((tm, LANES), lambda bi, i: (i, 0)),
                  pl.BlockSpec((QK_WIDTH, QK_WIDTH), const2)],
        out_specs=[tok_spec] * 3 + [u_spec],
        scratch_shapes=[pltpu.VMEM((QK_WIDTH // LANES, tm, LANES), F32)],
        compiler_params=_cparams(("parallel", "parallel")),
        name=name,
    )(x, sh, sc, g, w_bf, qg, kg, cosf, sinf, bd)


def _attn_kernel(lam_ref, q_ref, kl_ref, vl_ref, kc_ref, vc_ref, sg_ref, o_ref, a1_ref, a2_ref, m1_ref, m2_ref,
                 *, tk, out_scale, bounded):
    q = q_ref[0]
    lane = lax.broadcasted_iota(jnp.int32, q.shape, 1)
    zero = jnp.zeros_like(q)
    qa = jnp.where(lane < HEAD_DIM, q, zero)
    qb = jnp.where(lane >= HEAD_DIM, q, zero)

    a1_ref[...] = jnp.zeros(a1_ref.shape, F32)
    a2_ref[...] = jnp.zeros(a2_ref.shape, F32)
    if not bounded:
        m1_ref[...] = jnp.full(m1_ref.shape, -jnp.inf, F32)
        m2_ref[...] = jnp.full(m2_ref.shape, -jnp.inf, F32)

    def ones_col(rows):
        col = lax.broadcasted_iota(jnp.int32, (rows, LANES), 1)
        return jnp.where(col == 0, 1.0, 0.0).astype(BF16)

    def one_map(qm, kc, va, a_ref, m_ref):
        s = lax.dot_general(qm, kc, (((1,), (1,)), ((), ())), preferred_element_type=F32)
        if bounded:
            a_ref[...] += _dot(jnp.exp2(s).astype(BF16), va)
        else:
            m_prev = m_ref[...]
            m_next = jnp.maximum(m_prev, jnp.max(s, axis=1, keepdims=True))
            p = jnp.exp2(s - m_next[:, :1])
            alpha = jnp.exp2(m_prev - m_next)
            a_ref[...] = jnp.concatenate([alpha, alpha], axis=1) * a_ref[...] + _dot(p.astype(BF16), va)
            m_ref[...] = m_next

    def step(kc, vc, ones):
        va = jnp.concatenate([vc, ones], axis=1)
        one_map(qa, kc, va, a1_ref, m1_ref)
        one_map(qb, kc, va, a2_ref, m2_ref)

    ones_lat = ones_col(tk)

    def lat_body(j, carry):
        off = pl.multiple_of(j * tk, tk)
        step(kl_ref[0, pl.ds(off, tk), :], vl_ref[0, pl.ds(off, tk), :], ones_lat)
        return carry

    lax.fori_loop(0, kl_ref.shape[1] // tk, lat_body, 0)
    step(kc_ref[0], vc_ref[0], ones_col(kc_ref.shape[1]))

    lam = lam_ref[...]
    a1, a2 = a1_ref[...], a2_ref[...]
    o = a1[:, :LANES] / a1[:, LANES:LANES + 1] - lam * (a2[:, :LANES] / a2[:, LANES:LANES + 1])
    ms = jnp.mean(o * o, axis=-1, keepdims=True)
    o = o * lax.rsqrt(ms + RMS_EPS) * sg_ref[...]
    o_ref[0] = (o * out_scale).astype(BF16)


def _attn_cfg(n_lat):
    return dict(tq=_pick_tile(n_lat, ATTN_TQ), tk=_pick_tile(n_lat, ATTN_TK))


def _attn_call(lam_row, q, k_lat, v_lat, k_ctx, v_ctx, sg, tq, tk, out_scale=1.0, bounded=True):
    b, n, _ = q.shape
    nc = k_ctx.shape[1]
    kv_lat = pl.BlockSpec((1, n, LANES), lambda bi, h, i: (bi, 0, h))
    kv_ctx = pl.BlockSpec((1, nc, LANES), lambda bi, h, i: (bi, 0, h))
    q_spec = pl.BlockSpec((1, tq, LANES), lambda bi, h, i: (bi, i, h))
    row = pl.BlockSpec((1, LANES), lambda bi, h, i: (0, 0))
    acc = pltpu.VMEM((tq, 2 * LANES), F32)
    run_max = pltpu.VMEM((tq, LANES), F32)
    return pl.pallas_call(
        functools.partial(_attn_kernel, tk=tk, out_scale=out_scale, bounded=bounded),
        out_shape=jax.ShapeDtypeStruct((b, n, V_WIDTH), BF16),
        grid=(b, HEADS, n // tq),
        in_specs=[row, q_spec, kv_lat, kv_lat, kv_ctx, kv_ctx, row],
        out_specs=q_spec,
        scratch_shapes=[acc, acc, run_max, run_max],
        compiler_params=_cparams(("parallel", "parallel", "arbitrary")),
        name="attn" if bounded else "attn_general",
    )(lam_row, q, k_lat, v_lat, k_ctx, v_ctx, sg)


def _s5_kernel(ul_ref, uc_ref, win_ref, m_ref, wo_ref, lam_ref, o_ref, xl_ref, xc_ref, s_ref, *, nb):
    ul = ul_ref[0]
    n_slab = xl_ref.shape[0]

    def to_slabs(x_ref, x):
        for k in range(n_slab):
            x_ref[k] = x[:, k * LANES:(k + 1) * LANES]

    to_slabs(xl_ref, _dot(ul, win_ref[0]))
    to_slabs(xc_ref, _dot(uc_ref[0], win_ref[0]))
    n_lat = ul.shape[0] // nb
    n_ctx = uc_ref.shape[1] // nb

    lam = lam_ref[0]
    lfr, lfi, lbr, lbi = (jnp.broadcast_to(lam[i:i + 1], (nb, LANES)) for i in range(4))

    def rows(c, n_chunks):
        return pl.ds(c, nb, stride=n_chunks)

    def advance(x_ref, c, n_chunks, slab, ar, ai, sr, si):
        xr = x_ref[slab, rows(c, n_chunks), :]
        xi = x_ref[slab + 1, rows(c, n_chunks), :]
        return ar * sr - ai * si + xr, ar * si + ai * sr + xi

    def ctx_body(i, carry):
        fr, fi, br, bi = carry
        fr, fi = advance(xc_ref, i, n_ctx, 0, lfr, lfi, fr, fi)
        br, bi = advance(xc_ref, n_ctx - 1 - i, n_ctx, 2, lbr, lbi, br, bi)
        return fr, fi, br, bi

    def lat_body(i, carry):
        fr, fi, br, bi = carry
        cb = n_lat - 1 - i
        s_ref[0, rows(i, n_lat), :] = fr
        s_ref[1, rows(i, n_lat), :] = fi
        s_ref[2, rows(cb, n_lat), :] = br
        s_ref[3, rows(cb, n_lat), :] = bi
        fr, fi = advance(xl_ref, i, n_lat, 0, lfr, lfi, fr, fi)
        br, bi = advance(xl_ref, cb, n_lat, 2, lbr, lbi, br, bi)
        return fr, fi, br, bi

    z = jnp.zeros((nb, LANES), F32)
    carry = lax.fori_loop(0, n_ctx, ctx_body, (z, z, z, z))
    lax.fori_loop(0, n_lat, lat_body, carry)

    s_in = jnp.concatenate([s_ref[k].astype(BF16) for k in range(n_slab)], axis=1)
    y = _dot(ul, m_ref[0]) + _dot(s_in, wo_ref[0])
    o_ref[0] = jax.nn.gelu(y).astype(BF16)


def _s5_call(u_lat, u_ctx, win, m, wo, lam16, b):
    npair, rows_lat, kw = u_lat.shape
    nb = S5_BATCH_BLOCK if b % S5_BATCH_BLOCK == 0 else b
    rl = rows_lat // b * nb
    rc = u_ctx.shape[1] // b * nb
    wspec = pl.BlockSpec((1, kw, kw), lambda p, h: (p, 0, 0))
    slabs = lambda r: pltpu.VMEM((kw // LANES, r, LANES), F32)
    return pl.pallas_call(
        functools.partial(_s5_kernel, nb=nb),
        out_shape=jax.ShapeDtypeStruct((npair, rows_lat, kw), BF16),
        grid=(npair, b // nb),
        in_specs=[pl.BlockSpec((1, rl, kw), lambda p, h: (p, h, 0)),
                  pl.BlockSpec((1, rc, kw), lambda p, h: (p, h, 0)),
                  wspec, wspec, wspec,
                  pl.BlockSpec((1, 4, LANES), lambda p, h: (p, 0, 0))],
        out_specs=pl.BlockSpec((1, rl, kw), lambda p, h: (p, h, 0)),
        scratch_shapes=[slabs(rl), slabs(rc), slabs(rl)],
        compiler_params=_cparams(("parallel", "parallel")),
        name="s5",
    )(u_lat, u_ctx, win, m, wo, lam16)


def _s5_weights(a_re, a_im, log_dt, b_re, b_im, c_re, c_im, d_skip):
    hp = lax.Precision.HIGHEST
    g_n, p_n = a_re.shape[1], a_re.shape[2]
    t_n = CHUNK
    dt = jnp.exp(log_dt.astype(F32))[..., None]
    ar, ai = a_re.astype(F32), a_im.astype(F32)
    mag = jnp.exp(ar * dt)
    lr, li = mag * jnp.cos(ai * dt), mag * jnp.sin(ai * dt)
    den = ar * ar + ai * ai
    nr, ni = lr - 1.0, li
    cr = (nr * ar + ni * ai) / den
    ci = (ni * ar - nr * ai) / den
    bbr = cr[..., None] * b_re - ci[..., None] * b_im
    bbi = cr[..., None] * b_im + ci[..., None] * b_re
    n = jnp.arange(t_n + 1, dtype=F32)[:, None, None, None]
    pm = jnp.exp(n * (ar * dt))
    pw_r, pw_i = pm * jnp.cos(n * (ai * dt)), pm * jnp.sin(n * (ai * dt))
    lb_r = pw_r[:t_n, ..., None] * bbr - pw_i[:t_n, ..., None] * bbi
    lb_i = pw_r[:t_n, ..., None] * bbi + pw_i[:t_n, ..., None] * bbr
    cre, cim = c_re.astype(F32), c_im.astype(F32)
    kern = (jnp.einsum('gip,tdgpj->tdgij', cre, lb_r, precision=hp)
            - jnp.einsum('gip,tdgpj->tdgij', cim, lb_i, precision=hp))
    s_idx = jnp.arange(t_n)[:, None]
    t_idx = jnp.arange(t_n)[None, :]
    tau = t_idx - s_idx
    mf = jnp.where((tau >= 0)[..., None, None, None], kern[jnp.clip(tau, 0, t_n - 1), 0], 0.0)
    mb = jnp.where((tau <= 0)[..., None, None, None], kern[jnp.clip(-tau, 0, t_n - 1), 1], 0.0)
    hh = SSM_GROUP
    skip = (jnp.eye(t_n, dtype=F32)[:, :, None, None, None] * jnp.eye(hh, dtype=F32)[None, None, None]
            * d_skip.astype(F32)[None, None, :, :, None])
    m_g = jnp.transpose(mf + mb + skip, (2, 0, 4, 1, 3))

    def to_rows(w):
        return jnp.transpose(w, (1, 0, 3, 2))

    win4 = jnp.stack([to_rows(lb_r[::-1, 0]), to_rows(lb_i[::-1, 0]),
                      to_rows(lb_r[:, 1]), to_rows(lb_i[:, 1])], axis=1)

    def to_cols(zr, zi):
        o_re = cre[None] * zr[:, :, None, :] - cim[None] * zi[:, :, None, :]
        o_im = -(cre[None] * zi[:, :, None, :] + cim[None] * zr[:, :, None, :])
        f = lambda o: jnp.transpose(o, (1, 3, 0, 2))
        return f(o_re), f(o_im)

    of_re, of_im = to_cols(pw_r[1:, 0], pw_i[1:, 0])
    ob_re, ob_im = to_cols(pw_r[1:, 1][::-1], pw_i[1:, 1][::-1])
    wo4 = jnp.stack([of_re, of_im, ob_re, ob_im], axis=1)

    npair = g_n // 2
    eye2 = jnp.eye(2, dtype=F32)
    kw = 2 * t_n * hh
    m_pair = jnp.einsum('ab,pasjti->psajtbi', eye2,
                        m_g.reshape(npair, 2, t_n, hh, t_n, hh)).reshape(npair, kw, kw)
    win_pair = jnp.einsum('ab,paksjq->psajkbq', eye2,
                          win4.reshape(npair, 2, 4, t_n, hh, p_n)).reshape(npair, kw, 8 * p_n)
    wo_pair = jnp.einsum('ab,pakqti->pkaqtbi', eye2,
                         wo4.reshape(npair, 2, 4, p_n, t_n, hh)).reshape(npair, 8 * p_n, kw)
    lam16 = jnp.stack([pw_r[t_n, 0], pw_i[t_n, 0], pw_r[t_n, 1], pw_i[t_n, 1]], axis=1)
    lam16 = jnp.transpose(lam16.reshape(npair, 2, 4, p_n), (0, 2, 1, 3)).reshape(npair, 4, 2 * p_n)
    return win_pair.astype(BF16), m_pair.astype(BF16), wo_pair.astype(BF16), lam16


def _outproj_kernel(a_ref, y_ref, x_ref, g1_ref, sh_ref, sc_ref, n2_ref, wglu_ref, bglu_ref, wout_ref,
                    wr_hi_ref, wr_lo_ref, br_ref, x1_ref, h2_ref, gate_ref, ys_ref):
    _chunk_rows_to_token_major(
        lambda pr, k: y_ref[pr, 0, :, k * LANES:(k + 1) * LANES].astype(F32), ys_ref)
    yf = jnp.concatenate([ys_ref[q] for q in range(ys_ref.shape[0])], axis=1)
    y = yf.astype(BF16)
    z = _dot(y, wglu_ref[...]) + bglu_ref[...]
    s = (yf * jax.nn.sigmoid(z)).astype(BF16)
    half = a_ref.shape[2]
    o = _dot(a_ref[0], wout_ref[:half, :]) + _dot(s, wout_ref[half:, :])
    x1 = x_ref[0] + g1_ref[0] * o
    x1_ref[0] = x1

    ms = jnp.mean(x1 * x1, axis=-1, keepdims=True)
    h = (x1 * lax.rsqrt(ms + RMS_EPS)) * n2_ref[...]
    h = h * (1.0 + sc_ref[0]) + sh_ref[0]
    for j in range(h.shape[1] // LANES):
        h2_ref[pl.ds(j, h.shape[0], stride=ROW_TILE), :] = h[:, j * LANES:(j + 1) * LANES]

    h_hi, h_lo = _split_bf16(h)
    lg = _dot(h_hi, wr_hi_ref[...]) + (_dot(h_hi, wr_lo_ref[...]) + _dot(h_lo, wr_hi_ref[...])) + br_ref[...]
    lane = lax.broadcasted_iota(jnp.int32, lg.shape, 1)
    neg = jnp.float32(-jnp.inf)
    big = jnp.int32(LANES)

    def top1(vals):
        vmax = jnp.max(vals, axis=1, keepdims=True)
        idx = jnp.min(jnp.where(vals == vmax, lane, big), axis=1, keepdims=True)
        return vmax, idx

    is_grp = lane < MOE_GROUPS
    g_vals = jnp.where(is_grp, lg, neg)
    g_max, g_idx = top1(g_vals)
    p_grp = 1.0 / jnp.sum(jnp.where(is_grp, jnp.exp(g_vals - g_max), 0.0), axis=1, keepdims=True)
    e_lo = MOE_GROUPS + EXPERTS_PER_GROUP * g_idx
    in_grp = (lane >= e_lo) & (lane < e_lo + EXPERTS_PER_GROUP)
    e_vals = jnp.where(in_grp, lg, neg)
    v1, i1 = top1(e_vals)
    v2, i2 = top1(jnp.where(lane == i1, neg, e_vals))
    r = jnp.exp(v2 - v1)
    w1 = p_grp / (1.0 + r)
    w2 = w1 * r
    first_lo = i1 < i2
    a_loc = jnp.where(first_lo, i1, i2) - e_lo
    b_loc = jnp.where(first_lo, i2, i1) - e_lo
    pair = ((a_loc * (2 * EXPERTS_PER_GROUP - 1 - a_loc)) >> 1) + (b_loc - a_loc - 1)
    cls = g_idx * PAIRS_PER_GROUP + pair
    w_lo = jnp.where(first_lo, w1, w2)
    w_hi = jnp.where(first_lo, w2, w1)
    gate_ref[0] = (jnp.where(lane == 0, w_lo, 0.0) + jnp.where(lane == 1, w_hi, 0.0)
                   + jnp.where(lane == 2, cls.astype(F32), 0.0))


def _outproj_call(a, yg, x, g1, sh2, sc2, n2g, wglu, bglu, wout, wr_hi, wr_lo, br, tm):
    b, n, d = x.shape
    half = a.shape[2]
    tok = lambda w: pl.BlockSpec((1, tm, w), lambda bi, i: (bi, i, 0))
    mod = pl.BlockSpec((1, 1, d), lambda bi, i: (bi, 0, 0))
    const = lambda r, c: pl.BlockSpec((r, c), lambda bi, i: (0, 0))
    return pl.pallas_call(
        _outproj_kernel,
        out_shape=[jax.ShapeDtypeStruct((b, n, d), F32), jax.ShapeDtypeStruct((b * n * ROW_TILE, LANES), F32),
                   jax.ShapeDtypeStruct((b, n, LANES), F32)],
        grid=(b, n // tm),
        in_specs=[tok(half),
                  pl.BlockSpec((yg.shape[0], 1, tm // CHUNK, yg.shape[3]), lambda bi, i: (0, bi, i, 0)),
                  tok(d), mod, mod, mod, const(1, d),
                  const(half, half), const(1, half), const(d, d),
                  const(d, LANES), const(d, LANES), const(1, LANES)],
        out_specs=[tok(d), pl.BlockSpec((tm * ROW_TILE, LANES), lambda bi, i: (bi * (n // tm) + i, 0)), tok(LANES)],
        scratch_shapes=[pltpu.VMEM((half // LANES, tm, LANES), F32)],
        compiler_params=_cparams(("parallel", "parallel")),
        name="outproj",
    )(a, yg, x, g1, sh2, sc2, n2g, wglu, bglu, wout, wr_hi, wr_lo, br)


def _token_rows(tok, rows_per_token):
    return pl.ds(pl.multiple_of(tok * rows_per_token, rows_per_token), rows_per_token)


def _moe_expert_kernel(ea_ref, eb_ref, nv_ref, nt_ref, idx_hbm, h_hbm, wga_ref, wua_ref, wda_ref,
                       wgb_ref, wub_ref, wdb_ref, y_hbm, idx_smem, hbuf, ybuf, gsem, isem, psem, *, tm):
    i = pl.program_id(0)
    n_steps = pl.num_programs(0)
    n_tiles = nt_ref[0]
    hrows, yrows = tm * ROW_TILE, tm * 2 * ROW_TILE

    def idx_copy(tile):
        s = tile % 3
        return pltpu.make_async_copy(idx_hbm.at[pl.ds(tile, 1), :], idx_smem.at[pl.ds(s, 1), :], isem.at[s])

    def start_pulls(tile):
        s3, s2 = tile % 3, tile % 2

        def body(r, carry):
            tok = idx_smem[s3, r]
            pltpu.make_async_copy(h_hbm.at[_token_rows(tok, ROW_TILE), :],
                                  hbuf.at[pl.ds(s2 * hrows + r * ROW_TILE, ROW_TILE), :], gsem.at[s2]).start()
            return carry
        lax.fori_loop(0, tm, body, 0, unroll=8)

    def wait_pulls(tile):
        s2 = tile % 2
        pltpu.make_async_copy(h_hbm.at[pl.ds(0, hrows), :], hbuf.at[pl.ds(s2 * hrows, hrows), :], gsem.at[s2]).wait()

    def wait_pushes(tile):
        s2 = tile % 2
        n = nv_ref[tile] * (2 * ROW_TILE)

        @pl.when(n > 0)
        def _():
            pltpu.make_async_copy(ybuf.at[pl.ds(s2 * yrows, n), :], y_hbm.at[pl.ds(0, n), :], psem.at[s2]).wait()

    @pl.when(i == 0)
    def _():
        first = idx_copy(0)
        first.start()
        first.wait()
        start_pulls(0)

        @pl.when(n_steps > 1)
        def _():
            idx_copy(1).start()

    @pl.when(i + 2 < n_steps)
    def _():
        idx_copy(i + 2).start()

    @pl.when(i + 1 < n_steps)
    def _():
        idx_copy(i + 1).wait()

        @pl.when(i + 1 < n_tiles)
        def _():
            start_pulls(i + 1)

    @pl.when(i < n_tiles)
    def _():
        wait_pulls(i)

        @pl.when(i >= 2)
        def _():
            wait_pushes(i - 2)

        s2 = i % 2
        h = jnp.concatenate([hbuf[pl.ds(s2 * hrows + j, tm, stride=ROW_TILE), :] for j in range(ROW_TILE)],
                            axis=1).astype(BF16)

        def expert(wg_ref, wu_ref, wd_ref, row0):
            hid = jax.nn.silu(_dot(h, wg_ref[0])) * _dot(h, wu_ref[0])
            y = _dot(hid.astype(BF16), wd_ref[0])
            for j in range(ROW_TILE):
                ybuf[pl.ds(s2 * yrows + row0 + j, tm, stride=2 * ROW_TILE), :] = y[:, j * LANES:(j + 1) * LANES]

        expert(wga_ref, wua_ref, wda_ref, 0)
        expert(wgb_ref, wub_ref, wdb_ref, ROW_TILE)

        def push(r, carry):
            tok = idx_smem[i % 3, r]
            pltpu.make_async_copy(ybuf.at[pl.ds(s2 * yrows + r * 2 * ROW_TILE, 2 * ROW_TILE), :],
                                  y_hbm.at[_token_rows(tok, 2 * ROW_TILE), :], psem.at[s2]).start()
            return carry

        def push_block(c, carry):
            for u in range(PUSH_UNROLL):
                push(c * PUSH_UNROLL + u, carry)
            return carry

        n_real = nv_ref[i]
        n_blocks = n_real // PUSH_UNROLL
        lax.fori_loop(0, n_blocks, push_block, 0)
        lax.fori_loop(n_blocks * PUSH_UNROLL, n_real, push, 0)

    @pl.when((i == n_tiles) | ((i == n_steps - 1) & (i < n_tiles)))
    def _():
        last = jnp.minimum(i, n_tiles - 1)

        @pl.when(last >= 1)
        def _():
            wait_pushes(last - 1)
        wait_pushes(last)


def _moe_expert_call(ea, eb, nv, nt, idx, h_tiles, wg, wu, wd, n_tokens):
    n_steps, tm = idx.shape
    ne, d, f = wg.shape
    amap = lambda i, ea, eb, nv, nt: (ea[i], 0, 0)
    bmap = lambda i, ea, eb, nv, nt: (eb[i], 0, 0)
    up = lambda m: pl.BlockSpec((1, d, f), m)
    down = lambda m: pl.BlockSpec((1, f, d), m)
    hbm = pl.BlockSpec(memory_space=pl.ANY)
    grid_spec = pltpu.PrefetchScalarGridSpec(
        num_scalar_prefetch=4,
        grid=(n_steps,),
        in_specs=[hbm, hbm, up(amap), up(amap), down(amap), up(bmap), up(bmap), down(bmap)],
        out_specs=hbm,
        scratch_shapes=[pltpu.SMEM((3, tm), jnp.int32),
                        pltpu.VMEM((2 * tm * ROW_TILE, LANES), F32),
                        pltpu.VMEM((2 * tm * 2 * ROW_TILE, LANES), F32),
                        pltpu.SemaphoreType.DMA((2,)), pltpu.SemaphoreType.DMA((3,)), pltpu.SemaphoreType.DMA((2,))])
    return pl.pallas_call(
        functools.partial(_moe_expert_kernel, tm=tm),
        out_shape=jax.ShapeDtypeStruct((n_tokens * 2 * ROW_TILE, LANES), F32),
        grid_spec=grid_spec,
        compiler_params=_cparams(("arbitrary",)),
        name="moe_experts",
    )(ea, eb, nv, nt, idx, h_tiles, wg, wu, wd, wg, wu, wd)


def _moe_combine_kernel(y_ref, route_ref, x1_ref, g2_ref, o_ref):
    tm = x1_ref.shape[0]
    ya = jnp.concatenate([y_ref[pl.ds(j, tm, stride=2 * ROW_TILE), :] for j in range(ROW_TILE)], axis=1)
    yb = jnp.concatenate([y_ref[pl.ds(ROW_TILE + j, tm, stride=2 * ROW_TILE), :] for j in range(ROW_TILE)], axis=1)
    route = route_ref[...]
    moe = route[:, 0:1] * ya + route[:, 1:2] * yb
    o_ref[...] = x1_ref[...] + g2_ref[0] * moe


def _moe_combine_call(y_tiles, route, x1, g2, tokens_per_batch, tm):
    t, d = x1.shape
    per_b = tokens_per_batch // tm
    tok = lambda w: pl.BlockSpec((tm, w), lambda i: (i, 0))
    return pl.pallas_call(
        _moe_combine_kernel,
        out_shape=jax.ShapeDtypeStruct((t, d), F32),
        grid=(t // tm,),
        in_specs=[pl.BlockSpec((tm * 2 * ROW_TILE, LANES), lambda i: (i, 0)), tok(LANES), tok(d),
                  pl.BlockSpec((1, 1, d), lambda i: (i // per_b, 0, 0))],
        out_specs=tok(d),
        compiler_params=_cparams(("parallel",)),
        name="moe_combine",
    )(y_tiles, route, x1, g2)


def _routing_plan(cls, tm):
    t = cls.shape[0]
    n_steps = t // tm + N_CLASSES
    order = jnp.argsort(cls).astype(jnp.int32)
    classes = jnp.arange(N_CLASSES, dtype=jnp.int32)
    counts = jnp.sum((cls[:, None] == classes[None, :]).astype(jnp.int32), axis=0)
    cstart = jnp.cumsum(counts) - counts
    tiles_c = (counts + tm - 1) // tm
    tile_end = jnp.cumsum(tiles_c)
    n_tiles = tile_end[-1]
    tile_ids = jnp.arange(n_steps, dtype=jnp.int32)
    live = tile_ids < n_tiles
    c_of = jnp.sum((tile_end[None, :] <= jnp.minimum(tile_ids, n_tiles - 1)[:, None]).astype(jnp.int32), axis=1)
    k_of = jnp.minimum(tile_ids, n_tiles - 1) - (tile_end - tiles_c)[c_of]
    nv = jnp.where(live, jnp.clip(counts[c_of] - k_of * tm, 0, tm), 0).astype(jnp.int32)
    base = cstart[c_of] + k_of * tm
    idx = order[jnp.minimum(base[:, None] + jnp.arange(tm, dtype=jnp.int32)[None, :], t - 1)]
    grp, pair = c_of // PAIRS_PER_GROUP, c_of % PAIRS_PER_GROUP
    a_tab = jnp.array([a for a in range(EXPERTS_PER_GROUP) for _ in range(a + 1, EXPERTS_PER_GROUP)], jnp.int32)
    b_tab = jnp.array([b for a in range(EXPERTS_PER_GROUP) for b in range(a + 1, EXPERTS_PER_GROUP)], jnp.int32)
    ea = grp * EXPERTS_PER_GROUP + a_tab[pair]
    eb = grp * EXPERTS_PER_GROUP + b_tab[pair]
    return ea.astype(jnp.int32), eb.astype(jnp.int32), nv, n_tiles.reshape(1).astype(jnp.int32), idx


def _rope_tables(n_tokens):
    rows = n_tokens // GRID_W
    row = jnp.broadcast_to(jnp.arange(rows, dtype=F32)[:, None], (rows, GRID_W)).reshape(-1)
    col = jnp.broadcast_to(jnp.arange(GRID_W, dtype=F32)[None, :], (rows, GRID_W)).reshape(-1)
    half = HEAD_DIM // 2
    inv = ROPE_BASE ** (-jnp.arange(0, half, 2, dtype=F32) / half)
    ang = jnp.stack([row[:, None] * inv, col[:, None] * inv], axis=1)
    cos, sin = jnp.cos(ang), jnp.sin(ang)
    cos64 = jnp.concatenate([cos[:, 0], cos[:, 0], cos[:, 1], cos[:, 1]], axis=1)
    sin64 = jnp.concatenate([-sin[:, 0], sin[:, 0], -sin[:, 1], sin[:, 1]], axis=1)
    return jnp.tile(cos64, (1, LANES // HEAD_DIM)), jnp.tile(sin64, (1, LANES // HEAD_DIM))


def _pick_tile(n, target):
    t = min(n, target)
    while n % t:
        t //= 2
    return t


def kernel(x, c, ctx, c_ctx, w_ada, b_ada, norm1_g, w_in, q_norm_g, k_norm_g, lambda_q1, lambda_k1, lambda_q2, lambda_k2, subln_g, ssm_a_re, ssm_a_im, ssm_log_dt, ssm_b_re, ssm_b_im, ssm_c_re, ssm_c_im, ssm_d, w_glu, b_glu, w_out, norm2_g, w_route_group, b_route_group, w_route_expert, b_route_expert, w_exp_gate, w_exp_up, w_exp_down):
    depth = w_ada.shape[0]
    assert depth == 1, "single-layer block: the context stream is never updated"
    b, n_lat, d = x.shape
    n_ctx = ctx.shape[1]
    assert n_lat % CHUNK == 0 and n_ctx % CHUNK == 0 and n_lat % GRID_W == 0
    assert d == ROW_TILE * LANES, "MoE rows are moved as one (8, 128) tile per token"
    l = 0
    lam_init = 0.8 - 0.6 * math.exp(-0.3 * l)

    rows = b + 1
    rows_pad = -(-rows // 8) * 8
    cc = jnp.concatenate([c, c_ctx[None, :], jnp.zeros((rows_pad - rows, d), F32)], axis=0)
    mod = _mod_call(cc, w_ada[l], b_ada[l])
    sh1, sc1, g1, sh2, sc2, g2 = (mod[:b, i * d:(i + 1) * d].reshape(b, 1, d) for i in range(6))
    csh1, csc1 = (mod[b:b + 1, i * d:(i + 1) * d].reshape(1, 1, d) for i in range(2))

    w_in_bf = w_in[l].astype(BF16)
    bd = jnp.kron(jnp.eye(QK_WIDTH // HEAD_DIM, dtype=F32), jnp.ones((HEAD_DIM, HEAD_DIM), F32)).astype(BF16)
    qg = jnp.tile(q_norm_g[l], LANES // HEAD_DIM).reshape(1, LANES)
    kg = jnp.tile(k_norm_g[l], LANES // HEAD_DIM).reshape(1, LANES)
    cosf, sinf = _rope_tables(n_lat)
    ones_c, zeros_c = jnp.ones((n_ctx, LANES), F32), jnp.zeros((n_ctx, LANES), F32)
    g1n = norm1_g[l].reshape(1, d)
    tm = _pick_tile(n_lat, 512)
    q_x, k_x, v_x, u_x = _inproj_call(x, sh1, sc1, g1n, w_in_bf, qg, kg, cosf, sinf, bd, tm, "inproj_lat")
    _, k_c, v_c, u_c = _inproj_call(ctx, csh1, csc1, g1n, w_in_bf, qg, kg, ones_c, zeros_c, bd,
                                    _pick_tile(n_ctx, 512), "inproj_ctx")

    e1 = jnp.exp(jnp.sum(lambda_q1[l] * lambda_k1[l]))
    e2 = jnp.exp(jnp.sum(lambda_q2[l] * lambda_k2[l]))
    lam_row = jnp.full((1, LANES), e1 - e2 + lam_init, F32)
    score_bound = math.sqrt(HEAD_DIM) * jnp.max(jnp.abs(q_norm_g[l])) * jnp.max(jnp.abs(k_norm_g[l]))

    def attn(bounded):
        return lambda *ops: _attn_call(*ops, **_attn_cfg(n_lat), out_scale=1.0 - lam_init, bounded=bounded)

    a_x = lax.cond(score_bound <= SCORE_BOUND, attn(True), attn(False),
                   lam_row, q_x, k_x, v_x, k_c, v_c, subln_g[l].reshape(1, LANES))

    win, m_op, wo, lam16 = _s5_weights(ssm_a_re[l], ssm_a_im[l], ssm_log_dt[l], ssm_b_re[l], ssm_b_im[l],
                                       ssm_c_re[l], ssm_c_im[l], ssm_d[l])
    n_pairs, kw = u_x.shape[0], u_x.shape[3]
    yg = _s5_call(u_x.reshape(n_pairs, b * (n_lat // CHUNK), kw), u_c.reshape(n_pairs, b * (n_ctx // CHUNK), kw),
                  win, m_op, wo, lam16, b)
    yg = yg.reshape(n_pairs, b, n_lat // CHUNK, kw)

    wr = jnp.concatenate([w_route_group[l], w_route_expert[l]], axis=1)
    wr = jnp.pad(wr, ((0, 0), (0, LANES - wr.shape[1])))
    wr_hi, wr_lo = _split_bf16(wr)
    br = jnp.pad(jnp.concatenate([b_route_group[l], b_route_expert[l]]), (0, LANES - MOE_GROUPS - N_EXPERTS))
    x1, h2, route = _outproj_call(a_x, yg, x, g1, sh2, sc2, norm2_g[l].reshape(1, d),
                                 w_glu[l].astype(BF16), b_glu[l].reshape(1, -1), w_out[l].astype(BF16),
                                 wr_hi, wr_lo, br.reshape(1, LANES), tm)

    t_all = b * n_lat
    route = route.reshape(t_all, LANES)
    ea, eb, nv, n_tiles, idx = _routing_plan(route[:, 2].astype(jnp.int32), _pick_tile(t_all, MOE_TILE))
    y_tiles = _moe_expert_call(ea, eb, nv, n_tiles, idx, h2, w_exp_gate[l].astype(BF16),
                               w_exp_up[l].astype(BF16), w_exp_down[l].astype(BF16), t_all)
    out = _moe_combine_call(y_tiles, route, x1.reshape(t_all, d), g2, n_lat, _pick_tile(n_lat, COMBINE_TILE))
    return out.reshape(b, n_lat, d)
```

```python
import functools
import math

import jax
import jax.numpy as jnp
from jax import lax
from jax.experimental import pallas as pl
from jax.experimental.pallas import tpu as pltpu

F32 = jnp.float32
BF16 = jnp.bfloat16

LANES = 128
HEADS = 4
HEAD_DIM = 64
QK_WIDTH = HEADS * 2 * HEAD_DIM
V_WIDTH = HEADS * 2 * HEAD_DIM
GRID_W = 64
ROPE_BASE = 10000.0
SSM_GROUP = 16
SSM_STATE = 64
CHUNK = 16
MOE_GROUPS = 4
EXPERTS_PER_GROUP = 8
N_EXPERTS = MOE_GROUPS * EXPERTS_PER_GROUP
RMS_EPS = 1e-6
ATTN_TQ = 512
ATTN_TK = 4096
SCORE_BOUND = 60.0
SCAN_UNROLL = 8
S5_BATCH_BLOCK = 8
PAIRS_PER_GROUP = EXPERTS_PER_GROUP * (EXPERTS_PER_GROUP - 1) // 2
N_CLASSES = MOE_GROUPS * PAIRS_PER_GROUP
ROW_TILE = 8
MOE_TILE = 256
PUSH_UNROLL = 8
COMBINE_TILE = 512
VMEM_LIMIT = 48 * 1024 * 1024


def _cparams(sem):
    return pltpu.CompilerParams(dimension_semantics=sem, vmem_limit_bytes=VMEM_LIMIT)


def _split_bf16(a):
    hi = a.astype(BF16)
    lo = (a - hi.astype(F32)).astype(BF16)
    return hi, lo


def _dot(a, b):
    return jnp.dot(a, b, preferred_element_type=F32)


def _dot3(a, b):
    a_hi, a_lo = _split_bf16(a)
    b_hi, b_lo = _split_bf16(b)
    return _dot(a_hi, b_hi) + (_dot(a_hi, b_lo) + _dot(a_lo, b_hi))


def _mod_kernel(c_ref, w_ref, b_ref, o_ref):
    c = c_ref[...]
    a = c * jax.nn.sigmoid(c)
    o_ref[...] = _dot3(a, w_ref[...]) + b_ref[...]


def _mod_call(cc, w_ada, b_ada):
    rows, d = cc.shape
    n = w_ada.shape[1]
    bn = 1024
    return pl.pallas_call(
        _mod_kernel,
        out_shape=jax.ShapeDtypeStruct((rows, n), F32),
        grid=(n // bn,),
        in_specs=[pl.BlockSpec((rows, d), lambda j: (0, 0)),
                  pl.BlockSpec((d, bn), lambda j: (0, j)),
                  pl.BlockSpec((1, bn), lambda j: (0, j))],
        out_specs=pl.BlockSpec((rows, bn), lambda j: (0, j)),
        compiler_params=_cparams(("arbitrary",)),
        name="mod",
    )(cc, w_ada, b_ada.reshape(1, n))


def _inproj_kernel(x_ref, sh_ref, sc_ref, g_ref, w_ref, qg_ref, kg_ref, cos_ref, sin_ref, bd_ref,
                   q_ref, k_ref, v_ref, u_ref, us_ref):
    x = x_ref[0]
    ms = jnp.mean(x * x, axis=-1, keepdims=True)
    h = (x * lax.rsqrt(ms + RMS_EPS)) * g_ref[...]
    h = h * (1.0 + sc_ref[0]) + sh_ref[0]
    p = _dot(h.astype(BF16), w_ref[...])

    cosf = cos_ref[...]
    sinf = sin_ref[...]
    lane = lax.broadcasted_iota(jnp.int32, cosf.shape, 1)
    first_half = (lane % 32) < 16

    def norm_rope(t, gain, scale):
        ss = _dot((t * t).astype(BF16), bd_ref[...])
        t = t * lax.rsqrt(ss * (1.0 / HEAD_DIM) + RMS_EPS)
        outs = []
        for s in range(QK_WIDTH // LANES):
            ts = t[:, s * LANES:(s + 1) * LANES] * gain
            partner = jnp.where(first_half, pltpu.roll(ts, LANES - 16, 1), pltpu.roll(ts, 16, 1))
            outs.append(((ts * cosf + partner * sinf) * scale).astype(BF16))
        return jnp.concatenate(outs, axis=1)

    q_ref[0] = norm_rope(p[:, :QK_WIDTH], qg_ref[...], HEAD_DIM ** -0.5 * math.log2(math.e))
    k_ref[0] = norm_rope(p[:, QK_WIDTH:2 * QK_WIDTH], kg_ref[...], 1.0)
    v_ref[0] = p[:, 2 * QK_WIDTH:2 * QK_WIDTH + V_WIDTH].astype(BF16)
    for q in range(us_ref.shape[0]):
        lo = 2 * QK_WIDTH + V_WIDTH + q * LANES
        us_ref[q] = p[:, lo:lo + LANES]

    def store_u(pr, k, val):
        u_ref[pr, 0, :, k * LANES:(k + 1) * LANES] = val.astype(BF16)

    _token_major_to_chunk_rows(us_ref, store_u)


PAIR_W = 2 * SSM_GROUP
PAIRS_PER_TILE = LANES // PAIR_W


def _quarter_select(pieces):
    lane = lax.broadcasted_iota(jnp.int32, pieces[0].shape, 1)
    acc = pieces[0]
    for r in range(1, len(pieces)):
        acc = jnp.where(lane // PAIR_W == r, pieces[r], acc)
    return acc


def _token_major_to_chunk_rows(us_ref, store):
    n_chunk = us_ref.shape[1] // CHUNK
    n_pairs = us_ref.shape[0] * PAIRS_PER_TILE
    for k in range(CHUNK // PAIRS_PER_TILE):
        for pr in range(n_pairs):
            q, r_src = divmod(pr, PAIRS_PER_TILE)
            pieces = []
            for r in range(PAIRS_PER_TILE):
                src = us_ref[q, pl.ds(PAIRS_PER_TILE * k + r, n_chunk, stride=CHUNK), :]
                shift = ((r - r_src) % PAIRS_PER_TILE) * PAIR_W
                pieces.append(pltpu.roll(src, shift, 1) if shift else src)
            store(pr, k, _quarter_select(pieces))


def _chunk_rows_to_token_major(load, ys_ref):
    n_tiles, n_chunk = ys_ref.shape[0], ys_ref.shape[1] // CHUNK
    for t in range(CHUNK):
        k, r_src = divmod(t, PAIRS_PER_TILE)
        for q in range(n_tiles):
            pieces = []
            for r in range(PAIRS_PER_TILE):
                shift = ((r - r_src) % PAIRS_PER_TILE) * PAIR_W
                src = load(PAIRS_PER_TILE * q + r, k)
                pieces.append(pltpu.roll(src, shift, 1) if shift else src)
            ys_ref[q, pl.ds(t, n_chunk, stride=CHUNK), :] = _quarter_select(pieces)


def _inproj_call(x, sh, sc, g, w_bf, qg, kg, cosf, sinf, bd, tm, name):
    b, n, d = x.shape
    wn = w_bf.shape[1]
    per_batch = sh.shape[0] > 1
    mod_map = (lambda bi, i: (bi, 0, 0)) if per_batch else (lambda bi, i: (0, 0, 0))
    const2 = lambda bi, i: (0, 0)
    n_pairs = QK_WIDTH // PAIR_W
    outs = [jax.ShapeDtypeStruct((b, n, QK_WIDTH), BF16)] * 3 + [
        jax.ShapeDtypeStruct((n_pairs, b, n // CHUNK, CHUNK * PAIR_W), BF16)]
    tok_spec = pl.BlockSpec((1, tm, QK_WIDTH), lambda bi, i: (bi, i, 0))
    u_spec = pl.BlockSpec((n_pairs, 1, tm // CHUNK, CHUNK * PAIR_W), lambda bi, i: (0, bi, i, 0))
    return pl.pallas_call(
        _inproj_kernel,
        out_shape=outs,
        grid=(b, n // tm),
        in_specs=[pl.BlockSpec((1, tm, d), lambda bi, i: (bi, i, 0)),
                  pl.BlockSpec((1, 1, d), mod_map),
                  pl.BlockSpec((1, 1, d), mod_map),
                  pl.BlockSpec((1, d), const2),
                  pl.BlockSpec((d, wn), const2),
                  pl.BlockSpec((1, LANES), const2),
                  pl.BlockSpec((1, LANES), const2),
                  pl.BlockSpec((tm, LANES), lambda bi, i: (i, 0)),
                  pl.BlockSpec((tm, LANES), lambda bi, i: (i, 0)),
                  pl.BlockSpec((QK_WIDTH, QK_WIDTH), const2)],
        out_specs=[tok_spec] * 3 + [u_spec],
        scratch_shapes=[pltpu.VMEM((QK_WIDTH // LANES, tm, LANES), F32)],
        compiler_params=_cparams(("parallel", "parallel")),
        name=name,
    )(x, sh, sc, g, w_bf, qg, kg, cosf, sinf, bd)


def _attn_kernel(lam_ref, q_ref, kl_ref, vl_ref, kc_ref, vc_ref, sg_ref, o_ref, a1_ref, a2_ref, m1_ref, m2_ref,
                 *, tk, out_scale, bounded):
    q = q_ref[0]
    lane = lax.broadcasted_iota(jnp.int32, q.shape, 1)
    zero = jnp.zeros_like(q)
    qa = jnp.where(lane < HEAD_DIM, q, zero)
    qb = jnp.where(lane >= HEAD_DIM, q, zero)

    a1_ref[...] = jnp.zeros(a1_ref.shape, F32)
    a2_ref[...] = jnp.zeros(a2_ref.shape, F32)
    if not bounded:
        m1_ref[...] = jnp.full(m1_ref.shape, -jnp.inf, F32)
        m2_ref[...] = jnp.full(m2_ref.shape, -jnp.inf, F32)

    def ones_col(rows):
        col = lax.broadcasted_iota(jnp.int32, (rows, LANES), 1)
        return jnp.where(col == 0, 1.0, 0.0).astype(BF16)

    def one_map(qm, kc, va, a_ref, m_ref):
        s = lax.dot_general(qm, kc, (((1,), (1,)), ((), ())), preferred_element_type=F32)
        if bounded:
            a_ref[...] += _dot(jnp.exp2(s).astype(BF16), va)
        else:
            m_prev = m_ref[...]
            m_next = jnp.maximum(m_prev, jnp.max(s, axis=1, keepdims=True))
            p = jnp.exp2(s - m_next[:, :1])
            alpha = jnp.exp2(m_prev - m_next)
            a_ref[...] = jnp.concatenate([alpha, alpha], axis=1) * a_ref[...] + _dot(p.astype(BF16), va)
            m_ref[...] = m_next

    def step(kc, vc, ones):
        va = jnp.concatenate([vc, ones], axis=1)
        one_map(qa, kc, va, a1_ref, m1_ref)
        one_map(qb, kc, va, a2_ref, m2_ref)

    ones_lat = ones_col(tk)

    def lat_body(j, carry):
        off = pl.multiple_of(j * tk, tk)
        step(kl_ref[0, pl.ds(off, tk), :], vl_ref[0, pl.ds(off, tk), :], ones_lat)
        return carry

    lax.fori_loop(0, kl_ref.shape[1] // tk, lat_body, 0)
    step(kc_ref[0], vc_ref[0], ones_col(kc_ref.shape[1]))

    lam = lam_ref[...]
    a1, a2 = a1_ref[...], a2_ref[...]
    o = a1[:, :LANES] / a1[:, LANES:LANES + 1] - lam * (a2[:, :LANES] / a2[:, LANES:LANES + 1])
    ms = jnp.mean(o * o, axis=-1, keepdims=True)
    o = o * lax.rsqrt(ms + RMS_EPS) * sg_ref[...]
    o_ref[0] = (o * out_scale).astype(BF16)


def _attn_cfg(n_lat):
    return dict(tq=_pick_tile(n_lat, ATTN_TQ), tk=_pick_tile(n_lat, ATTN_TK))


def _attn_call(lam_row, q, k_lat, v_lat, k_ctx, v_ctx, sg, tq, tk, out_scale=1.0, bounded=True):
    b, n, _ = q.shape
    nc = k_ctx.shape[1]
    kv_lat = pl.BlockSpec((1, n, LANES), lambda bi, h, i: (bi, 0, h))
    kv_ctx = pl.BlockSpec((1, nc, LANES), lambda bi, h, i: (bi, 0, h))
    q_spec = pl.BlockSpec((1, tq, LANES), lambda bi, h, i: (bi, i, h))
    row = pl.BlockSpec((1, LANES), lambda bi, h, i: (0, 0))
    acc = pltpu.VMEM((tq, 2 * LANES), F32)
    run_max = pltpu.VMEM((tq, LANES), F32)
    return pl.pallas_call(
        functools.partial(_attn_kernel, tk=tk, out_scale=out_scale, bounded=bounded),
        out_shape=jax.ShapeDtypeStruct((b, n, V_WIDTH), BF16),
        grid=(b, HEADS, n // tq),
        in_specs=[row, q_spec, kv_lat, kv_lat, kv_ctx, kv_ctx, row],
        out_specs=q_spec,
        scratch_shapes=[acc, acc, run_max, run_max],
        compiler_params=_cparams(("parallel", "parallel", "arbitrary")),
        name="attn" if bounded else "attn_general",
    )(lam_row, q, k_lat, v_lat, k_ctx, v_ctx, sg)


def _s5_kernel(ul_ref, uc_ref, win_ref, m_ref, wo_ref, lam_ref, o_ref, xl_ref, xc_ref, s_ref, *, nb):
    ul = ul_ref[0]
    n_slab = xl_ref.shape[0]

    def to_slabs(x_ref, x):
        for k in range(n_slab):
            x_ref[k] = x[:, k * LANES:(k + 1) * LANES]

    to_slabs(xl_ref, _dot(ul, win_ref[0]))
    to_slabs(xc_ref, _dot(uc_ref[0], win_ref[0]))
    n_lat = ul.shape[0] // nb
    n_ctx = uc_ref.shape[1] // nb

    lam = lam_ref[0]
    lfr, lfi, lbr, lbi = (jnp.broadcast_to(lam[i:i + 1], (nb, LANES)) for i in range(4))

    def rows(c, n_chunks):
        return pl.ds(c, nb, stride=n_chunks)

    def advance(x_ref, c, n_chunks, slab, ar, ai, sr, si):
        xr = x_ref[slab, rows(c, n_chunks), :]
        xi = x_ref[slab + 1, rows(c, n_chunks), :]
        return ar * sr - ai * si + xr, ar * si + ai * sr + xi

    def ctx_body(i, carry):
        fr, fi, br, bi = carry
        fr, fi = advance(xc_ref, i, n_ctx, 0, lfr, lfi, fr, fi)
        br, bi = advance(xc_ref, n_ctx - 1 - i, n_ctx, 2, lbr, lbi, br, bi)
        return fr, fi, br, bi

    def lat_body(i, carry):
        fr, fi, br, bi = carry
        cb = n_lat - 1 - i
        s_ref[0, pl.ds(pl.multiple_of(i * nb, nb), nb), :] = fr
        s_ref[1, pl.ds(pl.multiple_of(i * nb, nb), nb), :] = fi
        s_ref[2, pl.ds(pl.multiple_of(cb * nb, nb), nb), :] = br
        s_ref[3, pl.ds(pl.multiple_of(cb * nb, nb), nb), :] = bi
        fr, fi = advance(xl_ref, i, n_lat, 0, lfr, lfi, fr, fi)
        br, bi = advance(xl_ref, cb, n_lat, 2, lbr, lbi, br, bi)
        return fr, fi, br, bi

    z = jnp.zeros((nb, LANES), F32)
    carry = lax.fori_loop(0, n_ctx, ctx_body, (z, z, z, z), unroll=SCAN_UNROLL)
    lax.fori_loop(0, n_lat, lat_body, carry, unroll=SCAN_UNROLL)

    def batch_major(k):
        return jnp.concatenate([s_ref[k, pl.ds(bi, n_lat, stride=nb), :] for bi in range(nb)], axis=0)

    s_in = jnp.concatenate([batch_major(k).astype(BF16) for k in range(n_slab)], axis=1)
    y = _dot(ul, m_ref[0]) + _dot(s_in, wo_ref[0])
    o_ref[0] = jax.nn.gelu(y).astype(BF16)


def _s5_call(u_lat, u_ctx, win, m, wo, lam16, b):
    npair, rows_lat, kw = u_lat.shape
    nb = S5_BATCH_BLOCK if b % S5_BATCH_BLOCK == 0 else b
    rl = rows_lat // b * nb
    rc = u_ctx.shape[1] // b * nb
    wspec = pl.BlockSpec((1, kw, kw), lambda p, h: (p, 0, 0))
    slabs = lambda r: pltpu.VMEM((kw // LANES, r, LANES), F32)
    return pl.pallas_call(
        functools.partial(_s5_kernel, nb=nb),
        out_shape=jax.ShapeDtypeStruct((npair, rows_lat, kw), BF16),
        grid=(npair, b // nb),
        in_specs=[pl.BlockSpec((1, rl, kw), lambda p, h: (p, h, 0)),
                  pl.BlockSpec((1, rc, kw), lambda p, h: (p, h, 0)),
                  wspec, wspec, wspec,
                  pl.BlockSpec((1, 4, LANES), lambda p, h: (p, 0, 0))],
        out_specs=pl.BlockSpec((1, rl, kw), lambda p, h: (p, h, 0)),
        scratch_shapes=[slabs(rl), slabs(rc), slabs(rl)],
        compiler_params=_cparams(("parallel", "parallel")),
        name="s5",
    )(u_lat, u_ctx, win, m, wo, lam16)


def _s5_weights(a_re, a_im, log_dt, b_re, b_im, c_re, c_im, d_skip):
    hp = lax.Precision.HIGHEST
    g_n, p_n = a_re.shape[1], a_re.shape[2]
    t_n = CHUNK
    dt = jnp.exp(log_dt.astype(F32))[..., None]
    ar, ai = a_re.astype(F32), a_im.astype(F32)
    mag = jnp.exp(ar * dt)
    lr, li = mag * jnp.cos(ai * dt), mag * jnp.sin(ai * dt)
    den = ar * ar + ai * ai
    nr, ni = lr - 1.0, li
    cr = (nr * ar + ni * ai) / den
    ci = (ni * ar - nr * ai) / den
    bbr = cr[..., None] * b_re - ci[..., None] * b_im
    bbi = cr[..., None] * b_im + ci[..., None] * b_re
    n = jnp.arange(t_n + 1, dtype=F32)[:, None, None, None]
    pm = jnp.exp(n * (ar * dt))
    pw_r, pw_i = pm * jnp.cos(n * (ai * dt)), pm * jnp.sin(n * (ai * dt))
    lb_r = pw_r[:t_n, ..., None] * bbr - pw_i[:t_n, ..., None] * bbi
    lb_i = pw_r[:t_n, ..., None] * bbi + pw_i[:t_n, ..., None] * bbr
    cre, cim = c_re.astype(F32), c_im.astype(F32)
    kern = (jnp.einsum('gip,tdgpj->tdgij', cre, lb_r, precision=hp)
            - jnp.einsum('gip,tdgpj->tdgij', cim, lb_i, precision=hp))
    hh = SSM_GROUP
    npair = g_n // 2
    eye2 = jnp.eye(2, dtype=F32)
    kw = 2 * t_n * hh

    lag0 = kern[0, 0] + kern[0, 1] + jnp.eye(hh, dtype=F32)[None] * d_skip.astype(F32)[:, :, None]
    by_lag = jnp.concatenate([kern[:0:-1, 1], lag0[None], kern[1:, 0]], axis=0)
    strip = jnp.einsum('ab,pajlq->pajlbq', eye2,
                       jnp.transpose(by_lag, (1, 3, 0, 2)).reshape(npair, 2, hh, 2 * t_n - 1, hh))
    strip = strip.reshape(npair, 2 * hh, (2 * t_n - 1) * 2 * hh)
    m_pair = jnp.stack([strip[:, :, (t_n - 1 - s) * 2 * hh:(t_n - 1 - s) * 2 * hh + kw] for s in range(t_n)],
                       axis=1).reshape(npair, kw, kw)

    def state_lanes(parts, mask_pair):
        x = jnp.stack(parts, axis=2).reshape(npair, 2, parts[0].shape[1], 4, 1, p_n)
        sel = eye2[None, :, None, None, :, None] if mask_pair else jnp.ones((1, 1, 1, 1, 2, 1), F32)
        return (x * sel).reshape(npair, 2, parts[0].shape[1], 8 * p_n)

    gsp = lambda z: jnp.transpose(z, (1, 0, 2))
    pf_r, pf_i, pb_r, pb_i = pw_r[:, 0], pw_i[:, 0], pw_r[:, 1], pw_i[:, 1]
    pa = state_lanes([gsp(pf_r[t_n - 1::-1]), gsp(pf_r[t_n - 1::-1]), gsp(pb_r[:t_n]), gsp(pb_r[:t_n])], True)
    pb = state_lanes([-gsp(pf_i[t_n - 1::-1]), gsp(pf_i[t_n - 1::-1]), -gsp(pb_i[:t_n]), gsp(pb_i[:t_n])], True)
    bt = lambda z: jnp.transpose(z, (0, 2, 1))
    ba = state_lanes([bt(bbr[0]), bt(bbi[0]), bt(bbr[1]), bt(bbi[1])], False)
    bb = state_lanes([bt(bbi[0]), bt(bbr[0]), bt(bbi[1]), bt(bbr[1])], False)
    sa = lambda z: jnp.transpose(z, (0, 2, 1, 3))
    win_pair = (sa(pa)[:, :, :, None, :] * ba[:, None] + sa(pb)[:, :, :, None, :] * bb[:, None]
                ).reshape(npair, kw, 8 * p_n)
    qa = state_lanes([gsp(pf_r[1:]), -gsp(pf_i[1:]), gsp(pb_r[t_n:0:-1]), -gsp(pb_i[t_n:0:-1])], True)
    qb = state_lanes([-gsp(pf_i[1:]), -gsp(pf_r[1:]), -gsp(pb_i[t_n:0:-1]), -gsp(pb_r[t_n:0:-1])], True)
    ca = state_lanes([cre] * 4, False)
    cb = state_lanes([cim] * 4, False)
    wo_t = (sa(qa)[:, :, :, None, :] * ca[:, None] + sa(qb)[:, :, :, None, :] * cb[:, None]
            ).reshape(npair, kw, 8 * p_n)
    wo_pair = jnp.swapaxes(wo_t.astype(BF16), 1, 2)
    lam16 = jnp.stack([pw_r[t_n, 0], pw_i[t_n, 0], pw_r[t_n, 1], pw_i[t_n, 1]], axis=1)
    lam16 = jnp.transpose(lam16.reshape(npair, 2, 4, p_n), (0, 2, 1, 3)).reshape(npair, 4, 2 * p_n)
    return win_pair.astype(BF16), m_pair.astype(BF16), wo_pair, lam16


def _outproj_kernel(a_ref, y_ref, x_ref, g1_ref, sh_ref, sc_ref, n2_ref, wglu_ref, bglu_ref, wout_ref,
                    wr_hi_ref, wr_lo_ref, br_ref, x1_ref, h2_ref, gate_ref, ys_ref):
    _chunk_rows_to_token_major(
        lambda pr, k: y_ref[pr, 0, :, k * LANES:(k + 1) * LANES].astype(F32), ys_ref)
    yf = jnp.concatenate([ys_ref[q] for q in range(ys_ref.shape[0])], axis=1)
    y = yf.astype(BF16)
    z = _dot(y, wglu_ref[...]) + bglu_ref[...]
    s = (yf * jax.nn.sigmoid(z)).astype(BF16)
    half = a_ref.shape[2]
    o = _dot(a_ref[0], wout_ref[:half, :]) + _dot(s, wout_ref[half:, :])
    x1 = x_ref[0] + g1_ref[0] * o
    x1_ref[0] = x1

    ms = jnp.mean(x1 * x1, axis=-1, keepdims=True)
    h = (x1 * lax.rsqrt(ms + RMS_EPS)) * n2_ref[...]
    h = h * (1.0 + sc_ref[0]) + sh_ref[0]
    for j in range(h.shape[1] // LANES):
        h2_ref[pl.ds(j, h.shape[0], stride=ROW_TILE), :] = h[:, j * LANES:(j + 1) * LANES]

    h_hi, h_lo = _split_bf16(h)
    lg = _dot(h_hi, wr_hi_ref[...]) + (_dot(h_hi, wr_lo_ref[...]) + _dot(h_lo, wr_hi_ref[...])) + br_ref[...]
    lane = lax.broadcasted_iota(jnp.int32, lg.shape, 1)
    neg = jnp.float32(-jnp.inf)
    big = jnp.int32(LANES)

    def top1(vals):
        vmax = jnp.max(vals, axis=1, keepdims=True)
        idx = jnp.min(jnp.where(vals == vmax, lane, big), axis=1, keepdims=True)
        return vmax, idx

    is_grp = lane < MOE_GROUPS
    g_vals = jnp.where(is_grp, lg, neg)
    g_max, g_idx = top1(g_vals)
    p_grp = 1.0 / jnp.sum(jnp.where(is_grp, jnp.exp(g_vals - g_max), 0.0), axis=1, keepdims=True)
    e_lo = MOE_GROUPS + EXPERTS_PER_GROUP * g_idx
    in_grp = (lane >= e_lo) & (lane < e_lo + EXPERTS_PER_GROUP)
    e_vals = jnp.where(in_grp, lg, neg)
    v1, i1 = top1(e_vals)
    v2, i2 = top1(jnp.where(lane == i1, neg, e_vals))
    r = jnp.exp(v2 - v1)
    w1 = p_grp / (1.0 + r)
    w2 = w1 * r
    first_lo = i1 < i2
    a_loc = jnp.where(first_lo, i1, i2) - e_lo
    b_loc = jnp.where(first_lo, i2, i1) - e_lo
    pair = ((a_loc * (2 * EXPERTS_PER_GROUP - 1 - a_loc)) >> 1) + (b_loc - a_loc - 1)
    cls = g_idx * PAIRS_PER_GROUP + pair
    w_lo = jnp.where(first_lo, w1, w2)
    w_hi = jnp.where(first_lo, w2, w1)
    gate_ref[0] = (jnp.where(lane == 0, w_lo, 0.0) + jnp.where(lane == 1, w_hi, 0.0)
                   + jnp.where(lane == 2, cls.astype(F32), 0.0))


def _outproj_call(a, yg, x, g1, sh2, sc2, n2g, wglu, bglu, wout, wr_hi, wr_lo, br, tm):
    b, n, d = x.shape
    half = a.shape[2]
    tok = lambda w: pl.BlockSpec((1, tm, w), lambda bi, i: (bi, i, 0))
    mod = pl.BlockSpec((1, 1, d), lambda bi, i: (bi, 0, 0))
    const = lambda r, c: pl.BlockSpec((r, c), lambda bi, i: (0, 0))
    return pl.pallas_call(
        _outproj_kernel,
        out_shape=[jax.ShapeDtypeStruct((b, n, d), F32), jax.ShapeDtypeStruct((b * n * ROW_TILE, LANES), F32),
                   jax.ShapeDtypeStruct((b, n, LANES), F32)],
        grid=(b, n // tm),
        in_specs=[tok(half),
                  pl.BlockSpec((yg.shape[0], 1, tm // CHUNK, yg.shape[3]), lambda bi, i: (0, bi, i, 0)),
                  tok(d), mod, mod, mod, const(1, d),
                  const(half, half), const(1, half), const(d, d),
                  const(d, LANES), const(d, LANES), const(1, LANES)],
        out_specs=[tok(d), pl.BlockSpec((tm * ROW_TILE, LANES), lambda bi, i: (bi * (n // tm) + i, 0)), tok(LANES)],
        scratch_shapes=[pltpu.VMEM((half // LANES, tm, LANES), F32)],
        compiler_params=_cparams(("parallel", "parallel")),
        name="outproj",
    )(a, yg, x, g1, sh2, sc2, n2g, wglu, bglu, wout, wr_hi, wr_lo, br)


def _token_rows(tok, rows_per_token):
    return pl.ds(pl.multiple_of(tok * rows_per_token, rows_per_token), rows_per_token)


def _moe_expert_kernel(ea_ref, eb_ref, nv_ref, nt_ref, idx_hbm, h_hbm, wga_ref, wua_ref, wda_ref,
                       wgb_ref, wub_ref, wdb_ref, y_hbm, idx_smem, hbuf, ybuf, gsem, isem, psem, *, tm):
    i = pl.program_id(0)
    n_steps = pl.num_programs(0)
    n_tiles = nt_ref[0]
    hrows, yrows = tm * ROW_TILE, tm * 2 * ROW_TILE

    def idx_copy(tile):
        s = tile % 3
        return pltpu.make_async_copy(idx_hbm.at[pl.ds(tile, 1), :], idx_smem.at[pl.ds(s, 1), :], isem.at[s])

    def start_pulls(tile):
        s3, s2 = tile % 3, tile % 2

        def body(r, carry):
            tok = idx_smem[s3, r]
            pltpu.make_async_copy(h_hbm.at[_token_rows(tok, ROW_TILE), :],
                                  hbuf.at[pl.ds(s2 * hrows + r * ROW_TILE, ROW_TILE), :], gsem.at[s2]).start()
            return carry
        lax.fori_loop(0, tm, body, 0, unroll=8)

    def wait_pulls(tile):
        s2 = tile % 2
        pltpu.make_async_copy(h_hbm.at[pl.ds(0, hrows), :], hbuf.at[pl.ds(s2 * hrows, hrows), :], gsem.at[s2]).wait()

    def wait_pushes(tile):
        s2 = tile % 2
        n = nv_ref[tile] * (2 * ROW_TILE)

        @pl.when(n > 0)
        def _():
            pltpu.make_async_copy(ybuf.at[pl.ds(s2 * yrows, n), :], y_hbm.at[pl.ds(0, n), :], psem.at[s2]).wait()

    @pl.when(i == 0)
    def _():
        first = idx_copy(0)
        first.start()
        first.wait()
        start_pulls(0)

        @pl.when(n_steps > 1)
        def _():
            idx_copy(1).start()

    @pl.when(i + 2 < n_steps)
    def _():
        idx_copy(i + 2).start()

    @pl.when(i + 1 < n_steps)
    def _():
        idx_copy(i + 1).wait()

        @pl.when(i + 1 < n_tiles)
        def _():
            start_pulls(i + 1)

    @pl.when(i < n_tiles)
    def _():
        wait_pulls(i)

        @pl.when(i >= 2)
        def _():
            wait_pushes(i - 2)

        s2 = i % 2
        h = jnp.concatenate([hbuf[pl.ds(s2 * hrows + j, tm, stride=ROW_TILE), :] for j in range(ROW_TILE)],
                            axis=1).astype(BF16)

        def expert(wg_ref, wu_ref, wd_ref, row0):
            hid = jax.nn.silu(_dot(h, wg_ref[0])) * _dot(h, wu_ref[0])
            y = _dot(hid.astype(BF16), wd_ref[0])
            for j in range(ROW_TILE):
                ybuf[pl.ds(s2 * yrows + row0 + j, tm, stride=2 * ROW_TILE), :] = y[:, j * LANES:(j + 1) * LANES]

        expert(wga_ref, wua_ref, wda_ref, 0)
        expert(wgb_ref, wub_ref, wdb_ref, ROW_TILE)

        def push(r, carry):
            tok = idx_smem[i % 3, r]
            pltpu.make_async_copy(ybuf.at[pl.ds(s2 * yrows + r * 2 * ROW_TILE, 2 * ROW_TILE), :],
                                  y_hbm.at[_token_rows(tok, 2 * ROW_TILE), :], psem.at[s2]).start()
            return carry

        def push_block(c, carry):
            for u in range(PUSH_UNROLL):
                push(c * PUSH_UNROLL + u, carry)
            return carry

        n_real = nv_ref[i]
        n_blocks = n_real // PUSH_UNROLL
        lax.fori_loop(0, n_blocks, push_block, 0)
        lax.fori_loop(n_blocks * PUSH_UNROLL, n_real, push, 0)

    @pl.when((i == n_tiles) | ((i == n_steps - 1) & (i < n_tiles)))
    def _():
        last = jnp.minimum(i, n_tiles - 1)

        @pl.when(last >= 1)
        def _():
            wait_pushes(last - 1)
        wait_pushes(last)


def _moe_expert_call(ea, eb, nv, nt, idx, h_tiles, wg, wu, wd, n_tokens):
    n_steps, tm = idx.shape
    ne, d, f = wg.shape
    amap = lambda i, ea, eb, nv, nt: (ea[i], 0, 0)
    bmap = lambda i, ea, eb, nv, nt: (eb[i], 0, 0)
    up = lambda m: pl.BlockSpec((1, d, f), m)
    down = lambda m: pl.BlockSpec((1, f, d), m)
    hbm = pl.BlockSpec(memory_space=pl.ANY)
    grid_spec = pltpu.PrefetchScalarGridSpec(
        num_scalar_prefetch=4,
        grid=(n_steps,),
        in_specs=[hbm, hbm, up(amap), up(amap), down(amap), up(bmap), up(bmap), down(bmap)],
        out_specs=hbm,
        scratch_shapes=[pltpu.SMEM((3, tm), jnp.int32),
                        pltpu.VMEM((2 * tm * ROW_TILE, LANES), F32),
                        pltpu.VMEM((2 * tm * 2 * ROW_TILE, LANES), F32),
                        pltpu.SemaphoreType.DMA((2,)), pltpu.SemaphoreType.DMA((3,)), pltpu.SemaphoreType.DMA((2,))])
    return pl.pallas_call(
        functools.partial(_moe_expert_kernel, tm=tm),
        out_shape=jax.ShapeDtypeStruct((n_tokens * 2 * ROW_TILE, LANES), F32),
        grid_spec=grid_spec,
        compiler_params=_cparams(("arbitrary",)),
        name="moe_experts",
    )(ea, eb, nv, nt, idx, h_tiles, wg, wu, wd, wg, wu, wd)


def _moe_combine_kernel(y_ref, route_ref, x1_ref, g2_ref, o_ref):
    tm = x1_ref.shape[0]
    ya = jnp.concatenate([y_ref[pl.ds(j, tm, stride=2 * ROW_TILE), :] for j in range(ROW_TILE)], axis=1)
    yb = jnp.concatenate([y_ref[pl.ds(ROW_TILE + j, tm, stride=2 * ROW_TILE), :] for j in range(ROW_TILE)], axis=1)
    route = route_ref[...]
    moe = route[:, 0:1] * ya + route[:, 1:2] * yb
    o_ref[...] = x1_ref[...] + g2_ref[0] * moe


def _moe_combine_call(y_tiles, route, x1, g2, tokens_per_batch, tm):
    t, d = x1.shape
    per_b = tokens_per_batch // tm
    tok = lambda w: pl.BlockSpec((tm, w), lambda i: (i, 0))
    return pl.pallas_call(
        _moe_combine_kernel,
        out_shape=jax.ShapeDtypeStruct((t, d), F32),
        grid=(t // tm,),
        in_specs=[pl.BlockSpec((tm * 2 * ROW_TILE, LANES), lambda i: (i, 0)), tok(LANES), tok(d),
                  pl.BlockSpec((1, 1, d), lambda i: (i // per_b, 0, 0))],
        out_specs=tok(d),
        compiler_params=_cparams(("parallel",)),
        name="moe_combine",
    )(y_tiles, route, x1, g2)


def _routing_plan(cls, tm):
    t = cls.shape[0]
    n_steps = t // tm + N_CLASSES
    order = jnp.argsort(cls).astype(jnp.int32)
    classes = jnp.arange(N_CLASSES, dtype=jnp.int32)
    counts = jnp.sum((cls[:, None] == classes[None, :]).astype(jnp.int32), axis=0)
    cstart = jnp.cumsum(counts) - counts
    tiles_c = (counts + tm - 1) // tm
    tile_end = jnp.cumsum(tiles_c)
    n_tiles = tile_end[-1]
    tile_ids = jnp.arange(n_steps, dtype=jnp.int32)
    live = tile_ids < n_tiles
    c_of = jnp.sum((tile_end[None, :] <= jnp.minimum(tile_ids, n_tiles - 1)[:, None]).astype(jnp.int32), axis=1)
    k_of = jnp.minimum(tile_ids, n_tiles - 1) - (tile_end - tiles_c)[c_of]
    nv = jnp.where(live, jnp.clip(counts[c_of] - k_of * tm, 0, tm), 0).astype(jnp.int32)
    base = cstart[c_of] + k_of * tm
    idx = order[jnp.minimum(base[:, None] + jnp.arange(tm, dtype=jnp.int32)[None, :], t - 1)]
    grp, pair = c_of // PAIRS_PER_GROUP, c_of % PAIRS_PER_GROUP
    a_tab = jnp.array([a for a in range(EXPERTS_PER_GROUP) for _ in range(a + 1, EXPERTS_PER_GROUP)], jnp.int32)
    b_tab = jnp.array([b for a in range(EXPERTS_PER_GROUP) for b in range(a + 1, EXPERTS_PER_GROUP)], jnp.int32)
    ea = grp * EXPERTS_PER_GROUP + a_tab[pair]
    eb = grp * EXPERTS_PER_GROUP + b_tab[pair]
    return ea.astype(jnp.int32), eb.astype(jnp.int32), nv, n_tiles.reshape(1).astype(jnp.int32), idx


def _rope_tables(n_tokens):
    rows = n_tokens // GRID_W
    row = jnp.broadcast_to(jnp.arange(rows, dtype=F32)[:, None], (rows, GRID_W)).reshape(-1)
    col = jnp.broadcast_to(jnp.arange(GRID_W, dtype=F32)[None, :], (rows, GRID_W)).reshape(-1)
    half = HEAD_DIM // 2
    inv = ROPE_BASE ** (-jnp.arange(0, half, 2, dtype=F32) / half)
    ang = jnp.stack([row[:, None] * inv, col[:, None] * inv], axis=1)
    cos, sin = jnp.cos(ang), jnp.sin(ang)
    cos64 = jnp.concatenate([cos[:, 0], cos[:, 0], cos[:, 1], cos[:, 1]], axis=1)
    sin64 = jnp.concatenate([-sin[:, 0], sin[:, 0], -sin[:, 1], sin[:, 1]], axis=1)
    return jnp.tile(cos64, (1, LANES // HEAD_DIM)), jnp.tile(sin64, (1, LANES // HEAD_DIM))


def _pick_tile(n, target):
    t = min(n, target)
    while n % t:
        t //= 2
    return t


def kernel(x, c, ctx, c_ctx, w_ada, b_ada, norm1_g, w_in, q_norm_g, k_norm_g, lambda_q1, lambda_k1, lambda_q2, lambda_k2, subln_g, ssm_a_re, ssm_a_im, ssm_log_dt, ssm_b_re, ssm_b_im, ssm_c_re, ssm_c_im, ssm_d, w_glu, b_glu, w_out, norm2_g, w_route_group, b_route_group, w_route_expert, b_route_expert, w_exp_gate, w_exp_up, w_exp_down):
    depth = w_ada.shape[0]
    assert depth == 1, "single-layer block: the context stream is never updated"
    b, n_lat, d = x.shape
    n_ctx = ctx.shape[1]
    assert n_lat % CHUNK == 0 and n_ctx % CHUNK == 0 and n_lat % GRID_W == 0
    assert d == ROW_TILE * LANES, "MoE rows are moved as one (8, 128) tile per token"
    l = 0
    lam_init = 0.8 - 0.6 * math.exp(-0.3 * l)

    rows = b + 1
    rows_pad = -(-rows // 8) * 8
    cc = jnp.concatenate([c, c_ctx[None, :], jnp.zeros((rows_pad - rows, d), F32)], axis=0)
    mod = _mod_call(cc, w_ada[l], b_ada[l])
    sh1, sc1, g1, sh2, sc2, g2 = (mod[:b, i * d:(i + 1) * d].reshape(b, 1, d) for i in range(6))
    csh1, csc1 = (mod[b:b + 1, i * d:(i + 1) * d].reshape(1, 1, d) for i in range(2))

    w_in_bf = w_in[l].astype(BF16)
    bd = jnp.kron(jnp.eye(QK_WIDTH // HEAD_DIM, dtype=F32), jnp.ones((HEAD_DIM, HEAD_DIM), F32)).astype(BF16)
    qg = jnp.tile(q_norm_g[l], LANES // HEAD_DIM).reshape(1, LANES)
    kg = jnp.tile(k_norm_g[l], LANES // HEAD_DIM).reshape(1, LANES)
    cosf, sinf = _rope_tables(n_lat)
    ones_c, zeros_c = jnp.ones((n_ctx, LANES), F32), jnp.zeros((n_ctx, LANES), F32)
    g1n = norm1_g[l].reshape(1, d)
    tm = _pick_tile(n_lat, 512)
    q_x, k_x, v_x, u_x = _inproj_call(x, sh1, sc1, g1n, w_in_bf, qg, kg, cosf, sinf, bd, tm, "inproj_lat")
    _, k_c, v_c, u_c = _inproj_call(ctx, csh1, csc1, g1n, w_in_bf, qg, kg, ones_c, zeros_c, bd,
                                    _pick_tile(n_ctx, 512), "inproj_ctx")

    e1 = jnp.exp(jnp.sum(lambda_q1[l] * lambda_k1[l]))
    e2 = jnp.exp(jnp.sum(lambda_q2[l] * lambda_k2[l]))
    lam_row = jnp.full((1, LANES), e1 - e2 + lam_init, F32)
    score_bound = math.sqrt(HEAD_DIM) * jnp.max(jnp.abs(q_norm_g[l])) * jnp.max(jnp.abs(k_norm_g[l]))

    def attn(bounded):
        return lambda *ops: _attn_call(*ops, **_attn_cfg(n_lat), out_scale=1.0 - lam_init, bounded=bounded)

    a_x = lax.cond(score_bound <= SCORE_BOUND, attn(True), attn(False),
                   lam_row, q_x, k_x, v_x, k_c, v_c, subln_g[l].reshape(1, LANES))

    win, m_op, wo, lam16 = _s5_weights(ssm_a_re[l], ssm_a_im[l], ssm_log_dt[l], ssm_b_re[l], ssm_b_im[l],
                                       ssm_c_re[l], ssm_c_im[l], ssm_d[l])
    n_pairs, kw = u_x.shape[0], u_x.shape[3]
    yg = _s5_call(u_x.reshape(n_pairs, b * (n_lat // CHUNK), kw), u_c.reshape(n_pairs, b * (n_ctx // CHUNK), kw),
                  win, m_op, wo, lam16, b)
    yg = yg.reshape(n_pairs, b, n_lat // CHUNK, kw)

    wr = jnp.concatenate([w_route_group[l], w_route_expert[l]], axis=1)
    wr = jnp.pad(wr, ((0, 0), (0, LANES - wr.shape[1])))
    wr_hi, wr_lo = _split_bf16(wr)
    br = jnp.pad(jnp.concatenate([b_route_group[l], b_route_expert[l]]), (0, LANES - MOE_GROUPS - N_EXPERTS))
    x1, h2, route = _outproj_call(a_x, yg, x, g1, sh2, sc2, norm2_g[l].reshape(1, d),
                                 w_glu[l].astype(BF16), b_glu[l].reshape(1, -1), w_out[l].astype(BF16),
                                 wr_hi, wr_lo, br.reshape(1, LANES), tm)

    t_all = b * n_lat
    route = route.reshape(t_all, LANES)
    ea, eb, nv, n_tiles, idx = _routing_plan(route[:, 2].astype(jnp.int32), _pick_tile(t_all, MOE_TILE))
    y_tiles = _moe_expert_call(ea, eb, nv, n_tiles, idx, h2, w_exp_gate[l].astype(BF16),
                               w_exp_up[l].astype(BF16), w_exp_down[l].astype(BF16), t_all)
    out = _moe_combine_call(y_tiles, route, x1.reshape(t_all, d), g2, n_lat, _pick_tile(n_lat, COMBINE_TILE))
    return out.reshape(b, n_lat, d)
```

```python
import functools
import math

import jax
import jax.numpy as jnp
from jax import lax
from jax.experimental import pallas as pl
from jax.experimental.pallas import tpu as pltpu

F32 = jnp.float32
BF16 = jnp.bfloat16

LANES = 128
HEADS = 4
HEAD_DIM = 64
QK_WIDTH = HEADS * 2 * HEAD_DIM
V_WIDTH = HEADS * 2 * HEAD_DIM
GRID_W = 64
ROPE_BASE = 10000.0
SSM_GROUP = 16
SSM_STATE = 64
CHUNK = 16
MOE_GROUPS = 4
EXPERTS_PER_GROUP = 8
N_EXPERTS = MOE_GROUPS * EXPERTS_PER_GROUP
RMS_EPS = 1e-6
ATTN_TQ = 512
ATTN_TK = 4096
SCORE_BOUND = 60.0
SCAN_UNROLL = 8
S5_BATCH_BLOCK = 8
PAIRS_PER_GROUP = EXPERTS_PER_GROUP * (EXPERTS_PER_GROUP - 1) // 2
N_CLASSES = MOE_GROUPS * PAIRS_PER_GROUP
ROW_TILE = 8
MOE_TILE = 256
PUSH_UNROLL = 8
COMBINE_TILE = 512
VMEM_LIMIT = 48 * 1024 * 1024


def _cparams(sem):
    return pltpu.CompilerParams(dimension_semantics=sem, vmem_limit_bytes=VMEM_LIMIT)


def _split_bf16(a):
    hi = a.astype(BF16)
    lo = (a - hi.astype(F32)).astype(BF16)
    return hi, lo


def _dot(a, b):
    return jnp.dot(a, b, preferred_element_type=F32)


def _dot3(a, b):
    a_hi, a_lo = _split_bf16(a)
    b_hi, b_lo = _split_bf16(b)
    return _dot(a_hi, b_hi) + (_dot(a_hi, b_lo) + _dot(a_lo, b_hi))


def _mod_kernel(c_ref, w_ref, b_ref, o_ref):
    c = c_ref[...]
    a = c * jax.nn.sigmoid(c)
    o_ref[...] = _dot3(a, w_ref[...]) + b_ref[...]


def _mod_call(cc, w_ada, b_ada):
    rows, d = cc.shape
    n = w_ada.shape[1]
    bn = 1024
    return pl.pallas_call(
        _mod_kernel,
        out_shape=jax.ShapeDtypeStruct((rows, n), F32),
        grid=(n // bn,),
        in_specs=[pl.BlockSpec((rows, d), lambda j: (0, 0)),
                  pl.BlockSpec((d, bn), lambda j: (0, j)),
                  pl.BlockSpec((1, bn), lambda j: (0, j))],
        out_specs=pl.BlockSpec((rows, bn), lambda j: (0, j)),
        compiler_params=_cparams(("arbitrary",)),
        name="mod",
    )(cc, w_ada, b_ada.reshape(1, n))


def _inproj_kernel(x_ref, sh_ref, sc_ref, g_ref, w_ref, qg_ref, kg_ref, cos_ref, sin_ref, bd_ref,
                   q_ref, k_ref, v_ref, u_ref, us_ref):
    x = x_ref[0]
    ms = jnp.mean(x * x, axis=-1, keepdims=True)
    h = (x * lax.rsqrt(ms + RMS_EPS)) * g_ref[...]
    h = h * (1.0 + sc_ref[0]) + sh_ref[0]
    p = _dot(h.astype(BF16), w_ref[...])

    cosf = cos_ref[...]
    sinf = sin_ref[...]
    lane = lax.broadcasted_iota(jnp.int32, cosf.shape, 1)
    first_half = (lane % 32) < 16

    def norm_rope(t, gain, scale):
        ss = _dot((t * t).astype(BF16), bd_ref[...])
        t = t * lax.rsqrt(ss * (1.0 / HEAD_DIM) + RMS_EPS)
        outs = []
        for s in range(QK_WIDTH // LANES):
            ts = t[:, s * LANES:(s + 1) * LANES] * gain
            partner = jnp.where(first_half, pltpu.roll(ts, LANES - 16, 1), pltpu.roll(ts, 16, 1))
            outs.append(((ts * cosf + partner * sinf) * scale).astype(BF16))
        return jnp.concatenate(outs, axis=1)

    q_ref[0] = norm_rope(p[:, :QK_WIDTH], qg_ref[...], HEAD_DIM ** -0.5 * math.log2(math.e))
    k_ref[0] = norm_rope(p[:, QK_WIDTH:2 * QK_WIDTH], kg_ref[...], 1.0)
    v_ref[0] = p[:, 2 * QK_WIDTH:2 * QK_WIDTH + V_WIDTH].astype(BF16)
    for q in range(us_ref.shape[0]):
        lo = 2 * QK_WIDTH + V_WIDTH + q * LANES
        us_ref[q] = p[:, lo:lo + LANES]

    def store_u(pr, k, val):
        u_ref[pr, 0, :, k * LANES:(k + 1) * LANES] = val.astype(BF16)

    _token_major_to_chunk_rows(us_ref, store_u)


PAIR_W = 2 * SSM_GROUP
PAIRS_PER_TILE = LANES // PAIR_W


def _quarter_select(pieces):
    lane = lax.broadcasted_iota(jnp.int32, pieces[0].shape, 1)
    acc = pieces[0]
    for r in range(1, len(pieces)):
        acc = jnp.where(lane // PAIR_W == r, pieces[r], acc)
    return acc


def _token_major_to_chunk_rows(us_ref, store):
    n_chunk = us_ref.shape[1] // CHUNK
    n_pairs = us_ref.shape[0] * PAIRS_PER_TILE
    for k in range(CHUNK // PAIRS_PER_TILE):
        for pr in range(n_pairs):
            q, r_src = divmod(pr, PAIRS_PER_TILE)
            pieces = []
            for r in range(PAIRS_PER_TILE):
                src = us_ref[q, pl.ds(PAIRS_PER_TILE * k + r, n_chunk, stride=CHUNK), :]
                shift = ((r - r_src) % PAIRS_PER_TILE) * PAIR_W
                pieces.append(pltpu.roll(src, shift, 1) if shift else src)
            store(pr, k, _quarter_select(pieces))


def _chunk_rows_to_token_major(load, ys_ref):
    n_tiles, n_chunk = ys_ref.shape[0], ys_ref.shape[1] // CHUNK
    for t in range(CHUNK):
        k, r_src = divmod(t, PAIRS_PER_TILE)
        for q in range(n_tiles):
            pieces = []
            for r in range(PAIRS_PER_TILE):
                shift = ((r - r_src) % PAIRS_PER_TILE) * PAIR_W
                src = load(PAIRS_PER_TILE * q + r, k)
                pieces.append(pltpu.roll(src, shift, 1) if shift else src)
            ys_ref[q, pl.ds(t, n_chunk, stride=CHUNK), :] = _quarter_select(pieces)


def _inproj_call(x, sh, sc, g, w_bf, qg, kg, cosf, sinf, bd, tm, name):
    b, n, d = x.shape
    wn = w_bf.shape[1]
    per_batch = sh.shape[0] > 1
    mod_map = (lambda bi, i: (bi, 0, 0)) if per_batch else (lambda bi, i: (0, 0, 0))
    const2 = lambda bi, i: (0, 0)
    n_pairs = QK_WIDTH // PAIR_W
    outs = [jax.ShapeDtypeStruct((b, n, QK_WIDTH), BF16)] * 3 + [
        jax.ShapeDtypeStruct((n_pairs, b, n // CHUNK, CHUNK * PAIR_W), BF16)]
    tok_spec = pl.BlockSpec((1, tm, QK_WIDTH), lambda bi, i: (bi, i, 0))
    u_spec = pl.BlockSpec((n_pairs, 1, tm // CHUNK, CHUNK * PAIR_W), lambda bi, i: (0, bi, i, 0))
    return pl.pallas_call(
        _inproj_kernel,
        out_shape=outs,
        grid=(b, n // tm),
        in_specs=[pl.BlockSpec((1, tm, d), lambda bi, i: (bi, i, 0)),
                  pl.BlockSpec((1, 1, d), mod_map),
                  pl.BlockSpec((1, 1, d), mod_map),
                  pl.BlockSpec((1, d), const2),
                  pl.BlockSpec((d, wn), const2),
                  pl.BlockSpec((1, LANES), const2),
                  pl.BlockSpec((1, LANES), const2),
                  pl.BlockSpec((tm, LANES), lambda bi, i: (i, 0)),
                  pl.BlockSpec((tm, LANES), lambda bi, i: (i, 0)),
                  pl.BlockSpec((QK_WIDTH, QK_WIDTH), const2)],
        out_specs=[tok_spec] * 3 + [u_spec],
        scratch_shapes=[pltpu.VMEM((QK_WIDTH // LANES, tm, LANES), F32)],
        compiler_params=_cparams(("parallel", "parallel")),
        name=name,
    )(x, sh, sc, g, w_bf, qg, kg, cosf, sinf, bd)


def _attn_kernel(lam_ref, q_ref, kl_ref, vl_ref, kc_ref, vc_ref, sg_ref, o_ref, a1_ref, a2_ref, m1_ref, m2_ref,
                 *, tk, out_scale, bounded):
    q = q_ref[0]
    lane = lax.broadcasted_iota(jnp.int32, q.shape, 1)
    zero = jnp.zeros_like(q)
    qa = jnp.where(lane < HEAD_DIM, q, zero)
    qb = jnp.where(lane >= HEAD_DIM, q, zero)

    a1_ref[...] = jnp.zeros(a1_ref.shape, F32)
    a2_ref[...] = jnp.zeros(a2_ref.shape, F32)
    if not bounded:
        m1_ref[...] = jnp.full(m1_ref.shape, -jnp.inf, F32)
        m2_ref[...] = jnp.full(m2_ref.shape, -jnp.inf, F32)

    def ones_col(rows):
        col = lax.broadcasted_iota(jnp.int32, (rows, LANES), 1)
        return jnp.where(col == 0, 1.0, 0.0).astype(BF16)

    def one_map(qm, kc, va, a_ref, m_ref):
        s = lax.dot_general(qm, kc, (((1,), (1,)), ((), ())), preferred_element_type=F32)
        if bounded:
            a_ref[...] += _dot(jnp.exp2(s).astype(BF16), va)
        else:
            m_prev = m_ref[...]
            m_next = jnp.maximum(m_prev, jnp.max(s, axis=1, keepdims=True))
            p = jnp.exp2(s - m_next[:, :1])
            alpha = jnp.exp2(m_prev - m_next)
            a_ref[...] = jnp.concatenate([alpha, alpha], axis=1) * a_ref[...] + _dot(p.astype(BF16), va)
            m_ref[...] = m_next

    def step(kc, vc, ones):
        va = jnp.concatenate([vc, ones], axis=1)
        one_map(qa, kc, va, a1_ref, m1_ref)
        one_map(qb, kc, va, a2_ref, m2_ref)

    ones_lat = ones_col(tk)

    def lat_body(j, carry):
        off = pl.multiple_of(j * tk, tk)
        step(kl_ref[0, pl.ds(off, tk), :], vl_ref[0, pl.ds(off, tk), :], ones_lat)
        return carry

    lax.fori_loop(0, kl_ref.shape[1] // tk, lat_body, 0)
    step(kc_ref[0], vc_ref[0], ones_col(kc_ref.shape[1]))

    lam = lam_ref[...]
    a1, a2 = a1_ref[...], a2_ref[...]
    o = a1[:, :LANES] / a1[:, LANES:LANES + 1] - lam * (a2[:, :LANES] / a2[:, LANES:LANES + 1])
    ms = jnp.mean(o * o, axis=-1, keepdims=True)
    o = o * lax.rsqrt(ms + RMS_EPS) * sg_ref[...]
    o_ref[0] = (o * out_scale).astype(BF16)


def _attn_cfg(n_lat):
    return dict(tq=_pick_tile(n_lat, ATTN_TQ), tk=_pick_tile(n_lat, ATTN_TK))


def _attn_call(lam_row, q, k_lat, v_lat, k_ctx, v_ctx, sg, tq, tk, out_scale=1.0, bounded=True):
    b, n, _ = q.shape
    nc = k_ctx.shape[1]
    kv_lat = pl.BlockSpec((1, n, LANES), lambda bi, h, i: (bi, 0, h))
    kv_ctx = pl.BlockSpec((1, nc, LANES), lambda bi, h, i: (bi, 0, h))
    q_spec = pl.BlockSpec((1, tq, LANES), lambda bi, h, i: (bi, i, h))
    row = pl.BlockSpec((1, LANES), lambda bi, h, i: (0, 0))
    acc = pltpu.VMEM((tq, 2 * LANES), F32)
    run_max = pltpu.VMEM((tq, LANES), F32)
    return pl.pallas_call(
        functools.partial(_attn_kernel, tk=tk, out_scale=out_scale, bounded=bounded),
        out_shape=jax.ShapeDtypeStruct((b, n, V_WIDTH), BF16),
        grid=(b, HEADS, n // tq),
        in_specs=[row, q_spec, kv_lat, kv_lat, kv_ctx, kv_ctx, row],
        out_specs=q_spec,
        scratch_shapes=[acc, acc, run_max, run_max],
        compiler_params=_cparams(("parallel", "parallel", "arbitrary")),
        name="attn" if bounded else "attn_general",
    )(lam_row, q, k_lat, v_lat, k_ctx, v_ctx, sg)


def _s5_kernel(ul_ref, uc_ref, win_ref, m_ref, wo_ref, lam_ref, o_ref, xl_ref, xc_ref, s_ref, *, nb):
    ul = ul_ref[0]
    n_slab = xl_ref.shape[0]

    def to_slabs(x_ref, x):
        for k in range(n_slab):
            x_ref[k] = x[:, k * LANES:(k + 1) * LANES]

    to_slabs(xl_ref, _dot(ul, win_ref[0]))
    to_slabs(xc_ref, _dot(uc_ref[0], win_ref[0]))
    n_lat = ul.shape[0] // nb
    n_ctx = uc_ref.shape[1] // nb

    lam = lam_ref[0]
    lfr, lfi, lbr, lbi = (jnp.broadcast_to(lam[i:i + 1], (nb, LANES)) for i in range(4))

    def rows(c, n_chunks):
        return pl.ds(c, nb, stride=n_chunks)

    def advance(x_ref, c, n_chunks, slab, ar, ai, sr, si):
        xr = x_ref[slab, rows(c, n_chunks), :]
        xi = x_ref[slab + 1, rows(c, n_chunks), :]
        return ar * sr - ai * si + xr, ar * si + ai * sr + xi

    def ctx_body(i, carry):
        fr, fi, br, bi = carry
        fr, fi = advance(xc_ref, i, n_ctx, 0, lfr, lfi, fr, fi)
        br, bi = advance(xc_ref, n_ctx - 1 - i, n_ctx, 2, lbr, lbi, br, bi)
        return fr, fi, br, bi

    def lat_body(i, carry):
        fr, fi, br, bi = carry
        cb = n_lat - 1 - i
        s_ref[0, pl.ds(pl.multiple_of(i * nb, nb), nb), :] = fr
        s_ref[1, pl.ds(pl.multiple_of(i * nb, nb), nb), :] = fi
        s_ref[2, pl.ds(pl.multiple_of(cb * nb, nb), nb), :] = br
        s_ref[3, pl.ds(pl.multiple_of(cb * nb, nb), nb), :] = bi
        fr, fi = advance(xl_ref, i, n_lat, 0, lfr, lfi, fr, fi)
        br, bi = advance(xl_ref, cb, n_lat, 2, lbr, lbi, br, bi)
        return fr, fi, br, bi

    z = jnp.zeros((nb, LANES), F32)
    carry = lax.fori_loop(0, n_ctx, ctx_body, (z, z, z, z), unroll=SCAN_UNROLL)
    lax.fori_loop(0, n_lat, lat_body, carry, unroll=SCAN_UNROLL)

    def batch_major(k):
        return jnp.concatenate([s_ref[k, pl.ds(bi, n_lat, stride=nb), :] for bi in range(nb)], axis=0)

    s_in = jnp.concatenate([batch_major(k).astype(BF16) for k in range(n_slab)], axis=1)
    y = _dot(ul, m_ref[0]) + _dot(s_in, wo_ref[0])
    o_ref[0] = jax.nn.gelu(y).astype(BF16)


def _s5_call(u_lat, u_ctx, win, m, wo, lam16, b):
    npair, rows_lat, kw = u_lat.shape
    nb = S5_BATCH_BLOCK if b % S5_BATCH_BLOCK == 0 else b
    rl = rows_lat // b * nb
    rc = u_ctx.shape[1] // b * nb
    wspec = pl.BlockSpec((1, kw, kw), lambda p, h: (p, 0, 0))
    slabs = lambda r: pltpu.VMEM((kw // LANES, r, LANES), F32)
    return pl.pallas_call(
        functools.partial(_s5_kernel, nb=nb),
        out_shape=jax.ShapeDtypeStruct((npair, rows_lat, kw), BF16),
        grid=(npair, b // nb),
        in_specs=[pl.BlockSpec((1, rl, kw), lambda p, h: (p, h, 0)),
                  pl.BlockSpec((1, rc, kw), lambda p, h: (p, h, 0)),
                  wspec, wspec, wspec,
                  pl.BlockSpec((1, 4, LANES), lambda p, h: (p, 0, 0))],
        out_specs=pl.BlockSpec((1, rl, kw), lambda p, h: (p, h, 0)),
        scratch_shapes=[slabs(rl), slabs(rc), slabs(rl)],
        compiler_params=_cparams(("parallel", "parallel")),
        name="s5",
    )(u_lat, u_ctx, win, m, wo, lam16)


def _s5_weights(a_re, a_im, log_dt, b_re, b_im, c_re, c_im, d_skip):
    hp = lax.Precision.HIGHEST
    g_n, p_n = a_re.shape[1], a_re.shape[2]
    t_n = CHUNK
    dt = jnp.exp(log_dt.astype(F32))[..., None]
    ar, ai = a_re.astype(F32), a_im.astype(F32)
    mag = jnp.exp(ar * dt)
    lr, li = mag * jnp.cos(ai * dt), mag * jnp.sin(ai * dt)
    den = ar * ar + ai * ai
    nr, ni = lr - 1.0, li
    cr = (nr * ar + ni * ai) / den
    ci = (ni * ar - nr * ai) / den
    bbr = cr[..., None] * b_re - ci[..., None] * b_im
    bbi = cr[..., None] * b_im + ci[..., None] * b_re
    n = jnp.arange(t_n + 1, dtype=F32)[:, None, None, None]
    pm = jnp.exp(n * (ar * dt))
    pw_r, pw_i = pm * jnp.cos(n * (ai * dt)), pm * jnp.sin(n * (ai * dt))
    lb_r = pw_r[:t_n, ..., None] * bbr - pw_i[:t_n, ..., None] * bbi
    lb_i = pw_r[:t_n, ..., None] * bbi + pw_i[:t_n, ..., None] * bbr
    cre, cim = c_re.astype(F32), c_im.astype(F32)
    kern = (jnp.einsum('gip,tdgpj->tdgij', cre, lb_r, precision=hp)
            - jnp.einsum('gip,tdgpj->tdgij', cim, lb_i, precision=hp))
    hh = SSM_GROUP
    npair = g_n // 2
    eye2 = jnp.eye(2, dtype=F32)
    kw = 2 * t_n * hh

    lag0 = kern[0, 0] + kern[0, 1] + jnp.eye(hh, dtype=F32)[None] * d_skip.astype(F32)[:, :, None]
    by_lag = jnp.concatenate([kern[:0:-1, 1], lag0[None], kern[1:, 0]], axis=0)
    strip = jnp.einsum('ab,pajlq->pajlbq', eye2,
                       jnp.transpose(by_lag, (1, 3, 0, 2)).reshape(npair, 2, hh, 2 * t_n - 1, hh))
    strip = strip.reshape(npair, 2 * hh, (2 * t_n - 1) * 2 * hh)
    m_pair = jnp.stack([strip[:, :, (t_n - 1 - s) * 2 * hh:(t_n - 1 - s) * 2 * hh + kw] for s in range(t_n)],
                       axis=1).reshape(npair, kw, kw)

    def state_lanes(parts, mask_pair):
        x = jnp.stack(parts, axis=2).reshape(npair, 2, parts[0].shape[1], 4, 1, p_n)
        sel = eye2[None, :, None, None, :, None] if mask_pair else jnp.ones((1, 1, 1, 1, 2, 1), F32)
        return (x * sel).reshape(npair, 2, parts[0].shape[1], 8 * p_n)

    gsp = lambda z: jnp.transpose(z, (1, 0, 2))
    pf_r, pf_i, pb_r, pb_i = pw_r[:, 0], pw_i[:, 0], pw_r[:, 1], pw_i[:, 1]
    pa = state_lanes([gsp(pf_r[t_n - 1::-1]), gsp(pf_r[t_n - 1::-1]), gsp(pb_r[:t_n]), gsp(pb_r[:t_n])], True)
    pb = state_lanes([-gsp(pf_i[t_n - 1::-1]), gsp(pf_i[t_n - 1::-1]), -gsp(pb_i[:t_n]), gsp(pb_i[:t_n])], True)
    bt = lambda z: jnp.transpose(z, (0, 2, 1))
    ba = state_lanes([bt(bbr[0]), bt(bbi[0]), bt(bbr[1]), bt(bbi[1])], False)
    bb = state_lanes([bt(bbi[0]), bt(bbr[0]), bt(bbi[1]), bt(bbr[1])], False)
    sa = lambda z: jnp.transpose(z, (0, 2, 1, 3))
    win_pair = (sa(pa)[:, :, :, None, :] * ba[:, None] + sa(pb)[:, :, :, None, :] * bb[:, None]
                ).reshape(npair, kw, 8 * p_n)
    qa = state_lanes([gsp(pf_r[1:]), -gsp(pf_i[1:]), gsp(pb_r[t_n:0:-1]), -gsp(pb_i[t_n:0:-1])], True)
    qb = state_lanes([-gsp(pf_i[1:]), -gsp(pf_r[1:]), -gsp(pb_i[t_n:0:-1]), -gsp(pb_r[t_n:0:-1])], True)
    ca = state_lanes([cre] * 4, False)
    cb = state_lanes([cim] * 4, False)
    wo_t = (sa(qa)[:, :, :, None, :] * ca[:, None] + sa(qb)[:, :, :, None, :] * cb[:, None]
            ).reshape(npair, kw, 8 * p_n)
    wo_pair = jnp.swapaxes(wo_t.astype(BF16), 1, 2)
    lam16 = jnp.stack([pw_r[t_n, 0], pw_i[t_n, 0], pw_r[t_n, 1], pw_i[t_n, 1]], axis=1)
    lam16 = jnp.transpose(lam16.reshape(npair, 2, 4, p_n), (0, 2, 1, 3)).reshape(npair, 4, 2 * p_n)
    return win_pair.astype(BF16), m_pair.astype(BF16), wo_pair, lam16


def _outproj_kernel(a_ref, y_ref, x_ref, g1_ref, sh_ref, sc_ref, n2_ref, wglu_ref, bglu_ref, wout_ref,
                    wr_hi_ref, wr_lo_ref, br_ref, x1_ref, h2_ref, gate_ref, ys_ref):
    _chunk_rows_to_token_major(
        lambda pr, k: y_ref[pr, 0, :, k * LANES:(k + 1) * LANES].astype(F32), ys_ref)
    yf = jnp.concatenate([ys_ref[q] for q in range(ys_ref.shape[0])], axis=1)
    y = yf.astype(BF16)
    z = _dot(y, wglu_ref[...]) + bglu_ref[...]
    s = (yf * jax.nn.sigmoid(z)).astype(BF16)
    half = a_ref.shape[2]
    o = _dot(a_ref[0], wout_ref[:half, :]) + _dot(s, wout_ref[half:, :])
    x1 = x_ref[0] + g1_ref[0] * o
    x1_ref[0] = x1

    ms = jnp.mean(x1 * x1, axis=-1, keepdims=True)
    h = (x1 * lax.rsqrt(ms + RMS_EPS)) * n2_ref[...]
    h = h * (1.0 + sc_ref[0]) + sh_ref[0]
    for j in range(h.shape[1] // LANES):
        h2_ref[pl.ds(j, h.shape[0], stride=ROW_TILE), :] = h[:, j * LANES:(j + 1) * LANES]

    h_hi, h_lo = _split_bf16(h)
    lg = _dot(h_hi, wr_hi_ref[...]) + (_dot(h_hi, wr_lo_ref[...]) + _dot(h_lo, wr_hi_ref[...])) + br_ref[...]
    lane = lax.broadcasted_iota(jnp.int32, lg.shape, 1)
    neg = jnp.float32(-jnp.inf)
    big = jnp.int32(LANES)

    def top1(vals):
        vmax = jnp.max(vals, axis=1, keepdims=True)
        idx = jnp.min(jnp.where(vals == vmax, lane, big), axis=1, keepdims=True)
        return vmax, idx

    is_grp = lane < MOE_GROUPS
    g_vals = jnp.where(is_grp, lg, neg)
    g_max, g_idx = top1(g_vals)
    p_grp = 1.0 / jnp.sum(jnp.where(is_grp, jnp.exp(g_vals - g_max), 0.0), axis=1, keepdims=True)
    e_lo = MOE_GROUPS + EXPERTS_PER_GROUP * g_idx
    in_grp = (lane >= e_lo) & (lane < e_lo + EXPERTS_PER_GROUP)
    e_vals = jnp.where(in_grp, lg, neg)
    v1, i1 = top1(e_vals)
    v2, i2 = top1(jnp.where(lane == i1, neg, e_vals))
    r = jnp.exp(v2 - v1)
    w1 = p_grp / (1.0 + r)
    w2 = w1 * r
    first_lo = i1 < i2
    a_loc = jnp.where(first_lo, i1, i2) - e_lo
    b_loc = jnp.where(first_lo, i2, i1) - e_lo
    pair = ((a_loc * (2 * EXPERTS_PER_GROUP - 1 - a_loc)) >> 1) + (b_loc - a_loc - 1)
    cls = g_idx * PAIRS_PER_GROUP + pair
    w_lo = jnp.where(first_lo, w1, w2)
    w_hi = jnp.where(first_lo, w2, w1)
    gate_ref[0] = (jnp.where(lane == 0, w_lo, 0.0) + jnp.where(lane == 1, w_hi, 0.0)
                   + jnp.where(lane == 2, cls.astype(F32), 0.0))


def _outproj_call(a, yg, x, g1, sh2, sc2, n2g, wglu, bglu, wout, wr_hi, wr_lo, br, tm):
    b, n, d = x.shape
    half = a.shape[2]
    tok = lambda w: pl.BlockSpec((1, tm, w), lambda bi, i: (bi, i, 0))
    mod = pl.BlockSpec((1, 1, d), lambda bi, i: (bi, 0, 0))
    const = lambda r, c: pl.BlockSpec((r, c), lambda bi, i: (0, 0))
    return pl.pallas_call(
        _outproj_kernel,
        out_shape=[jax.ShapeDtypeStruct((b, n, d), F32), jax.ShapeDtypeStruct((b * n * ROW_TILE, LANES), F32),
                   jax.ShapeDtypeStruct((b, n, LANES), F32)],
        grid=(b, n // tm),
        in_specs=[tok(half),
                  pl.BlockSpec((yg.shape[0], 1, tm // CHUNK, yg.shape[3]), lambda bi, i: (0, bi, i, 0)),
                  tok(d), mod, mod, mod, const(1, d),
                  const(half, half), const(1, half), const(d, d),
                  const(d, LANES), const(d, LANES), const(1, LANES)],
        out_specs=[tok(d), pl.BlockSpec((tm * ROW_TILE, LANES), lambda bi, i: (bi * (n // tm) + i, 0)), tok(LANES)],
        scratch_shapes=[pltpu.VMEM((half // LANES, tm, LANES), F32)],
        compiler_params=_cparams(("parallel", "parallel")),
        name="outproj",
    )(a, yg, x, g1, sh2, sc2, n2g, wglu, bglu, wout, wr_hi, wr_lo, br)


def _token_rows(tok, rows_per_token):
    return pl.ds(pl.multiple_of(tok * rows_per_token, rows_per_token), rows_per_token)


def _moe_expert_kernel(ea_ref, eb_ref, nv_ref, nt_ref, idx_hbm, h_hbm, wga_ref, wua_ref, wda_ref,
                       wgb_ref, wub_ref, wdb_ref, y_hbm, idx_smem, hbuf, ybuf, gsem, isem, psem, *, tm):
    i = pl.program_id(0)
    n_steps = pl.num_programs(0)
    n_tiles = nt_ref[0]
    hrows, yrows = tm * ROW_TILE, tm * 2 * ROW_TILE

    def idx_copy(tile):
        s = tile % 4
        return pltpu.make_async_copy(idx_hbm.at[pl.ds(tile, 1), :], idx_smem.at[pl.ds(s, 1), :], isem.at[s])

    def pull_row(tile, r):
        tok = idx_smem[tile % 4, r]
        s2 = tile % 2
        return pltpu.make_async_copy(h_hbm.at[_token_rows(tok, ROW_TILE), :],
                                     hbuf.at[pl.ds(s2 * hrows + r * ROW_TILE, ROW_TILE), :], gsem.at[s2])

    def push_row(tile, r):
        tok = idx_smem[tile % 4, r]
        s3 = tile % 3
        return pltpu.make_async_copy(ybuf.at[pl.ds(s3 * yrows + r * 2 * ROW_TILE, 2 * ROW_TILE), :],
                                     y_hbm.at[_token_rows(tok, 2 * ROW_TILE), :], psem.at[s3])

    def wait_pulls(tile):
        s2 = tile % 2
        pltpu.make_async_copy(h_hbm.at[pl.ds(0, hrows), :], hbuf.at[pl.ds(s2 * hrows, hrows), :], gsem.at[s2]).wait()

    def wait_pushes(tile):
        s3 = tile % 3
        n = nv_ref[tile] * (2 * ROW_TILE)

        @pl.when(n > 0)
        def _():
            pltpu.make_async_copy(ybuf.at[pl.ds(s3 * yrows, n), :], y_hbm.at[pl.ds(0, n), :], psem.at[s3]).wait()

    @pl.when(i == 0)
    def _():
        first = idx_copy(0)
        first.start()
        first.wait()

        def body(r, carry):
            pull_row(0, r).start()
            return carry
        lax.fori_loop(0, tm, body, 0, unroll=PUSH_UNROLL)
        idx_copy(1).start()

    @pl.when(i + 2 < n_steps)
    def _():
        idx_copy(i + 2).start()

    @pl.when(i + 1 < n_steps)
    def _():
        idx_copy(i + 1).wait()

    @pl.when(i < n_tiles)
    def _():
        wait_pulls(i)

        @pl.when(i >= 3)
        def _():
            wait_pushes(i - 3)

        s2, s3 = i % 2, i % 3
        h = jnp.concatenate([hbuf[pl.ds(s2 * hrows + j, tm, stride=ROW_TILE), :] for j in range(ROW_TILE)],
                            axis=1).astype(BF16)
        pull_next = i + 1 < n_tiles
        n_push = jnp.where(i >= 1, nv_ref[jnp.maximum(i - 1, 0)], 0)

        def issue_rows(r0, r1):
            for r in range(r0, r1):
                @pl.when(pull_next)
                def _():
                    pull_row(i + 1, r).start()

                @pl.when(r < n_push)
                def _():
                    push_row(i - 1, r).start()

        def expert(wg_ref, wu_ref, wd_ref, row0):
            hid = jax.nn.silu(_dot(h, wg_ref[0])) * _dot(h, wu_ref[0])
            y = _dot(hid.astype(BF16), wd_ref[0])
            for j in range(ROW_TILE):
                ybuf[pl.ds(s3 * yrows + row0 + j, tm, stride=2 * ROW_TILE), :] = y[:, j * LANES:(j + 1) * LANES]

        issue_rows(0, tm // 2)
        expert(wga_ref, wua_ref, wda_ref, 0)
        issue_rows(tm // 2, tm)
        expert(wgb_ref, wub_ref, wdb_ref, ROW_TILE)

    @pl.when(i == n_tiles)
    def _():
        def body(r, carry):
            push_row(i - 1, r).start()
            return carry
        lax.fori_loop(0, nv_ref[i - 1], body, 0)

        for back in (3, 2, 1):
            @pl.when(i >= back)
            def _():
                wait_pushes(i - back)


def _moe_expert_call(ea, eb, nv, nt, idx, h_tiles, wg, wu, wd, n_tokens):
    n_steps, tm = idx.shape
    ne, d, f = wg.shape
    amap = lambda i, ea, eb, nv, nt: (ea[i], 0, 0)
    bmap = lambda i, ea, eb, nv, nt: (eb[i], 0, 0)
    up = lambda m: pl.BlockSpec((1, d, f), m)
    down = lambda m: pl.BlockSpec((1, f, d), m)
    hbm = pl.BlockSpec(memory_space=pl.ANY)
    grid_spec = pltpu.PrefetchScalarGridSpec(
        num_scalar_prefetch=4,
        grid=(n_steps,),
        in_specs=[hbm, hbm, up(amap), up(amap), down(amap), up(bmap), up(bmap), down(bmap)],
        out_specs=hbm,
        scratch_shapes=[pltpu.SMEM((4, tm), jnp.int32),
                        pltpu.VMEM((2 * tm * ROW_TILE, LANES), F32),
                        pltpu.VMEM((3 * tm * 2 * ROW_TILE, LANES), F32),
                        pltpu.SemaphoreType.DMA((2,)), pltpu.SemaphoreType.DMA((4,)), pltpu.SemaphoreType.DMA((3,))])
    return pl.pallas_call(
        functools.partial(_moe_expert_kernel, tm=tm),
        out_shape=jax.ShapeDtypeStruct((n_tokens * 2 * ROW_TILE, LANES), F32),
        grid_spec=grid_spec,
        compiler_params=_cparams(("arbitrary",)),
        name="moe_experts",
    )(ea, eb, nv, nt, idx, h_tiles, wg, wu, wd, wg, wu, wd)


def _moe_combine_kernel(y_ref, route_ref, x1_ref, g2_ref, o_ref):
    tm = x1_ref.shape[0]
    ya = jnp.concatenate([y_ref[pl.ds(j, tm, stride=2 * ROW_TILE), :] for j in range(ROW_TILE)], axis=1)
    yb = jnp.concatenate([y_ref[pl.ds(ROW_TILE + j, tm, stride=2 * ROW_TILE), :] for j in range(ROW_TILE)], axis=1)
    route = route_ref[...]
    moe = route[:, 0:1] * ya + route[:, 1:2] * yb
    o_ref[...] = x1_ref[...] + g2_ref[0] * moe


def _moe_combine_call(y_tiles, route, x1, g2, tokens_per_batch, tm):
    t, d = x1.shape
    per_b = tokens_per_batch // tm
    tok = lambda w: pl.BlockSpec((tm, w), lambda i: (i, 0))
    return pl.pallas_call(
        _moe_combine_kernel,
        out_shape=jax.ShapeDtypeStruct((t, d), F32),
        grid=(t // tm,),
        in_specs=[pl.BlockSpec((tm * 2 * ROW_TILE, LANES), lambda i: (i, 0)), tok(LANES), tok(d),
                  pl.BlockSpec((1, 1, d), lambda i: (i // per_b, 0, 0))],
        out_specs=tok(d),
        compiler_params=_cparams(("parallel",)),
        name="moe_combine",
    )(y_tiles, route, x1, g2)


def _routing_plan(cls, tm):
    t = cls.shape[0]
    n_steps = t // tm + N_CLASSES + 1
    order = jnp.argsort(cls).astype(jnp.int32)
    classes = jnp.arange(N_CLASSES, dtype=jnp.int32)
    counts = jnp.sum((cls[:, None] == classes[None, :]).astype(jnp.int32), axis=0)
    cstart = jnp.cumsum(counts) - counts
    tiles_c = (counts + tm - 1) // tm
    tile_end = jnp.cumsum(tiles_c)
    n_tiles = tile_end[-1]
    tile_ids = jnp.arange(n_steps, dtype=jnp.int32)
    live = tile_ids < n_tiles
    c_of = jnp.sum((tile_end[None, :] <= jnp.minimum(tile_ids, n_tiles - 1)[:, None]).astype(jnp.int32), axis=1)
    k_of = jnp.minimum(tile_ids, n_tiles - 1) - (tile_end - tiles_c)[c_of]
    nv = jnp.where(live, jnp.clip(counts[c_of] - k_of * tm, 0, tm), 0).astype(jnp.int32)
    base = cstart[c_of] + k_of * tm
    idx = order[jnp.minimum(base[:, None] + jnp.arange(tm, dtype=jnp.int32)[None, :], t - 1)]
    grp, pair = c_of // PAIRS_PER_GROUP, c_of % PAIRS_PER_GROUP
    a_tab = jnp.array([a for a in range(EXPERTS_PER_GROUP) for _ in range(a + 1, EXPERTS_PER_GROUP)], jnp.int32)
    b_tab = jnp.array([b for a in range(EXPERTS_PER_GROUP) for b in range(a + 1, EXPERTS_PER_GROUP)], jnp.int32)
    ea = grp * EXPERTS_PER_GROUP + a_tab[pair]
    eb = grp * EXPERTS_PER_GROUP + b_tab[pair]
    return ea.astype(jnp.int32), eb.astype(jnp.int32), nv, n_tiles.reshape(1).astype(jnp.int32), idx


def _rope_tables(n_tokens):
    rows = n_tokens // GRID_W
    row = jnp.broadcast_to(jnp.arange(rows, dtype=F32)[:, None], (rows, GRID_W)).reshape(-1)
    col = jnp.broadcast_to(jnp.arange(GRID_W, dtype=F32)[None, :], (rows, GRID_W)).reshape(-1)
    half = HEAD_DIM // 2
    inv = ROPE_BASE ** (-jnp.arange(0, half, 2, dtype=F32) / half)
    ang = jnp.stack([row[:, None] * inv, col[:, None] * inv], axis=1)
    cos, sin = jnp.cos(ang), jnp.sin(ang)
    cos64 = jnp.concatenate([cos[:, 0], cos[:, 0], cos[:, 1], cos[:, 1]], axis=1)
    sin64 = jnp.concatenate([-sin[:, 0], sin[:, 0], -sin[:, 1], sin[:, 1]], axis=1)
    return jnp.tile(cos64, (1, LANES // HEAD_DIM)), jnp.tile(sin64, (1, LANES // HEAD_DIM))


def _pick_tile(n, target):
    t = min(n, target)
    while n % t:
        t //= 2
    return t


def kernel(x, c, ctx, c_ctx, w_ada, b_ada, norm1_g, w_in, q_norm_g, k_norm_g, lambda_q1, lambda_k1, lambda_q2, lambda_k2, subln_g, ssm_a_re, ssm_a_im, ssm_log_dt, ssm_b_re, ssm_b_im, ssm_c_re, ssm_c_im, ssm_d, w_glu, b_glu, w_out, norm2_g, w_route_group, b_route_group, w_route_expert, b_route_expert, w_exp_gate, w_exp_up, w_exp_down):
    depth = w_ada.shape[0]
    assert depth == 1, "single-layer block: the context stream is never updated"
    b, n_lat, d = x.shape
    n_ctx = ctx.shape[1]
    assert n_lat % CHUNK == 0 and n_ctx % CHUNK == 0 and n_lat % GRID_W == 0
    assert d == ROW_TILE * LANES, "MoE rows are moved as one (8, 128) tile per token"
    l = 0
    lam_init = 0.8 - 0.6 * math.exp(-0.3 * l)

    rows = b + 1
    rows_pad = -(-rows // 8) * 8
    cc = jnp.concatenate([c, c_ctx[None, :], jnp.zeros((rows_pad - rows, d), F32)], axis=0)
    mod = _mod_call(cc, w_ada[l], b_ada[l])
    sh1, sc1, g1, sh2, sc2, g2 = (mod[:b, i * d:(i + 1) * d].reshape(b, 1, d) for i in range(6))
    csh1, csc1 = (mod[b:b + 1, i * d:(i + 1) * d].reshape(1, 1, d) for i in range(2))

    w_in_bf = w_in[l].astype(BF16)
    bd = jnp.kron(jnp.eye(QK_WIDTH // HEAD_DIM, dtype=F32), jnp.ones((HEAD_DIM, HEAD_DIM), F32)).astype(BF16)
    qg = jnp.tile(q_norm_g[l], LANES // HEAD_DIM).reshape(1, LANES)
    kg = jnp.tile(k_norm_g[l], LANES // HEAD_DIM).reshape(1, LANES)
    cosf, sinf = _rope_tables(n_lat)
    ones_c, zeros_c = jnp.ones((n_ctx, LANES), F32), jnp.zeros((n_ctx, LANES), F32)
    g1n = norm1_g[l].reshape(1, d)
    tm = _pick_tile(n_lat, 512)
    q_x, k_x, v_x, u_x = _inproj_call(x, sh1, sc1, g1n, w_in_bf, qg, kg, cosf, sinf, bd, tm, "inproj_lat")
    _, k_c, v_c, u_c = _inproj_call(ctx, csh1, csc1, g1n, w_in_bf, qg, kg, ones_c, zeros_c, bd,
                                    _pick_tile(n_ctx, 512), "inproj_ctx")

    e1 = jnp.exp(jnp.sum(lambda_q1[l] * lambda_k1[l]))
    e2 = jnp.exp(jnp.sum(lambda_q2[l] * lambda_k2[l]))
    lam_row = jnp.full((1, LANES), e1 - e2 + lam_init, F32)
    score_bound = math.sqrt(HEAD_DIM) * jnp.max(jnp.abs(q_norm_g[l])) * jnp.max(jnp.abs(k_norm_g[l]))

    def attn(bounded):
        return lambda *ops: _attn_call(*ops, **_attn_cfg(n_lat), out_scale=1.0 - lam_init, bounded=bounded)

    a_x = lax.cond(score_bound <= SCORE_BOUND, attn(True), attn(False),
                   lam_row, q_x, k_x, v_x, k_c, v_c, subln_g[l].reshape(1, LANES))

    win, m_op, wo, lam16 = _s5_weights(ssm_a_re[l], ssm_a_im[l], ssm_log_dt[l], ssm_b_re[l], ssm_b_im[l],
                                       ssm_c_re[l], ssm_c_im[l], ssm_d[l])
    n_pairs, kw = u_x.shape[0], u_x.shape[3]
    yg = _s5_call(u_x.reshape(n_pairs, b * (n_lat // CHUNK), kw), u_c.reshape(n_pairs, b * (n_ctx // CHUNK), kw),
                  win, m_op, wo, lam16, b)
    yg = yg.reshape(n_pairs, b, n_lat // CHUNK, kw)

    wr = jnp.concatenate([w_route_group[l], w_route_expert[l]], axis=1)
    wr = jnp.pad(wr, ((0, 0), (0, LANES - wr.shape[1])))
    wr_hi, wr_lo = _split_bf16(wr)
    br = jnp.pad(jnp.concatenate([b_route_group[l], b_route_expert[l]]), (0, LANES - MOE_GROUPS - N_EXPERTS))
    x1, h2, route = _outproj_call(a_x, yg, x, g1, sh2, sc2, norm2_g[l].reshape(1, d),
                                 w_glu[l].astype(BF16), b_glu[l].reshape(1, -1), w_out[l].astype(BF16),
                                 wr_hi, wr_lo, br.reshape(1, LANES), tm)

    t_all = b * n_lat
    route = route.reshape(t_all, LANES)
    ea, eb, nv, n_tiles, idx = _routing_plan(route[:, 2].astype(jnp.int32), _pick_tile(t_all, MOE_TILE))
    y_tiles = _moe_expert_call(ea, eb, nv, n_tiles, idx, h2, w_exp_gate[l].astype(BF16),
                               w_exp_up[l].astype(BF16), w_exp_down[l].astype(BF16), t_all)
    out = _moe_combine_call(y_tiles, route, x1.reshape(t_all, d), g2, n_lat, _pick_tile(n_lat, COMBINE_TILE))
    return out.reshape(b, n_lat, d)
```

```python
import functools
import math

import jax
import jax.numpy as jnp
from jax import lax
from jax.experimental import pallas as pl
from jax.experimental.pallas import tpu as pltpu

F32 = jnp.float32
BF16 = jnp.bfloat16

LANES = 128
HEADS = 4
HEAD_DIM = 64
QK_WIDTH = HEADS * 2 * HEAD_DIM
V_WIDTH = HEADS * 2 * HEAD_DIM
GRID_W = 64
ROPE_BASE = 10000.0
SSM_GROUP = 16
SSM_STATE = 64
CHUNK = 16
MOE_GROUPS = 4
EXPERTS_PER_GROUP = 8
N_EXPERTS = MOE_GROUPS * EXPERTS_PER_GROUP
RMS_EPS = 1e-6
ATTN_TQ = 512
ATTN_TK = 4096
SCORE_BOUND = 60.0
SCAN_UNROLL = 8
S5_BATCH_BLOCK = 8
PAIRS_PER_GROUP = EXPERTS_PER_GROUP * (EXPERTS_PER_GROUP - 1) // 2
N_CLASSES = MOE_GROUPS * PAIRS_PER_GROUP
ROW_TILE = 8
MOE_TILE = 256
PUSH_UNROLL = 8
IDX_SLOTS = 5
COMBINE_TILE = 512
VMEM_LIMIT = 48 * 1024 * 1024


def _cparams(sem):
    return pltpu.CompilerParams(dimension_semantics=sem, vmem_limit_bytes=VMEM_LIMIT)


def _split_bf16(a):
    hi = a.astype(BF16)
    lo = (a - hi.astype(F32)).astype(BF16)
    return hi, lo


def _dot(a, b):
    return jnp.dot(a, b, preferred_element_type=F32)


def _dot3(a, b):
    a_hi, a_lo = _split_bf16(a)
    b_hi, b_lo = _split_bf16(b)
    return _dot(a_hi, b_hi) + (_dot(a_hi, b_lo) + _dot(a_lo, b_hi))


def _mod_kernel(c_ref, w_ref, b_ref, o_ref):
    c = c_ref[...]
    a = c * jax.nn.sigmoid(c)
    o_ref[...] = _dot3(a, w_ref[...]) + b_ref[...]


def _mod_call(cc, w_ada, b_ada):
    rows, d = cc.shape
    n = w_ada.shape[1]
    bn = 1024
    return pl.pallas_call(
        _mod_kernel,
        out_shape=jax.ShapeDtypeStruct((rows, n), F32),
        grid=(n // bn,),
        in_specs=[pl.BlockSpec((rows, d), lambda j: (0, 0)),
                  pl.BlockSpec((d, bn), lambda j: (0, j)),
                  pl.BlockSpec((1, bn), lambda j: (0, j))],
        out_specs=pl.BlockSpec((rows, bn), lambda j: (0, j)),
        compiler_params=_cparams(("arbitrary",)),
        name="mod",
    )(cc, w_ada, b_ada.reshape(1, n))


def _inproj_kernel(x_ref, sh_ref, sc_ref, g_ref, w_ref, qg_ref, kg_ref, cos_ref, sin_ref, bd_ref,
                   q_ref, k_ref, v_ref, u_ref, us_ref):
    x = x_ref[0]
    ms = jnp.mean(x * x, axis=-1, keepdims=True)
    h = (x * lax.rsqrt(ms + RMS_EPS)) * g_ref[...]
    h = h * (1.0 + sc_ref[0]) + sh_ref[0]
    p = _dot(h.astype(BF16), w_ref[...])

    cosf = cos_ref[...]
    sinf = sin_ref[...]
    lane = lax.broadcasted_iota(jnp.int32, cosf.shape, 1)
    first_half = (lane % 32) < 16

    def norm_rope(t, gain, scale):
        ss = _dot((t * t).astype(BF16), bd_ref[...])
        t = t * lax.rsqrt(ss * (1.0 / HEAD_DIM) + RMS_EPS)
        outs = []
        for s in range(QK_WIDTH // LANES):
            ts = t[:, s * LANES:(s + 1) * LANES] * gain
            partner = jnp.where(first_half, pltpu.roll(ts, LANES - 16, 1), pltpu.roll(ts, 16, 1))
            outs.append(((ts * cosf + partner * sinf) * scale).astype(BF16))
        return jnp.concatenate(outs, axis=1)

    q_ref[0] = norm_rope(p[:, :QK_WIDTH], qg_ref[...], HEAD_DIM ** -0.5 * math.log2(math.e))
    k_ref[0] = norm_rope(p[:, QK_WIDTH:2 * QK_WIDTH], kg_ref[...], 1.0)
    v_ref[0] = p[:, 2 * QK_WIDTH:2 * QK_WIDTH + V_WIDTH].astype(BF16)
    for q in range(us_ref.shape[0]):
        lo = 2 * QK_WIDTH + V_WIDTH + q * LANES
        us_ref[q] = p[:, lo:lo + LANES]

    def store_u(pr, k, val):
        u_ref[pr, 0, :, k * LANES:(k + 1) * LANES] = val.astype(BF16)

    _token_major_to_chunk_rows(us_ref, store_u)


PAIR_W = 2 * SSM_GROUP
PAIRS_PER_TILE = LANES // PAIR_W


def _quarter_select(pieces):
    lane = lax.broadcasted_iota(jnp.int32, pieces[0].shape, 1)
    acc = pieces[0]
    for r in range(1, len(pieces)):
        acc = jnp.where(lane // PAIR_W == r, pieces[r], acc)
    return acc


def _token_major_to_chunk_rows(us_ref, store):
    n_chunk = us_ref.shape[1] // CHUNK
    n_pairs = us_ref.shape[0] * PAIRS_PER_TILE
    for k in range(CHUNK // PAIRS_PER_TILE):
        for pr in range(n_pairs):
            q, r_src = divmod(pr, PAIRS_PER_TILE)
            pieces = []
            for r in range(PAIRS_PER_TILE):
                src = us_ref[q, pl.ds(PAIRS_PER_TILE * k + r, n_chunk, stride=CHUNK), :]
                shift = ((r - r_src) % PAIRS_PER_TILE) * PAIR_W
                pieces.append(pltpu.roll(src, shift, 1) if shift else src)
            store(pr, k, _quarter_select(pieces))


def _chunk_rows_to_token_major(load, ys_ref):
    n_tiles, n_chunk = ys_ref.shape[0], ys_ref.shape[1] // CHUNK
    for t in range(CHUNK):
        k, r_src = divmod(t, PAIRS_PER_TILE)
        for q in range(n_tiles):
            pieces = []
            for r in range(PAIRS_PER_TILE):
                shift = ((r - r_src) % PAIRS_PER_TILE) * PAIR_W
                src = load(PAIRS_PER_TILE * q + r, k)
                pieces.append(pltpu.roll(src, shift, 1) if shift else src)
            ys_ref[q, pl.ds(t, n_chunk, stride=CHUNK), :] = _quarter_select(pieces)


def _inproj_call(x, sh, sc, g, w_bf, qg, kg, cosf, sinf, bd, tm, name):
    b, n, d = x.shape
    wn = w_bf.shape[1]
    per_batch = sh.shape[0] > 1
    mod_map = (lambda bi, i: (bi, 0, 0)) if per_batch else (lambda bi, i: (0, 0, 0))
    const2 = lambda bi, i: (0, 0)
    n_pairs = QK_WIDTH // PAIR_W
    outs = [jax.ShapeDtypeStruct((b, n, QK_WIDTH), BF16)] * 3 + [
        jax.ShapeDtypeStruct((n_pairs, b, n // CHUNK, CHUNK * PAIR_W), BF16)]
    tok_spec = pl.BlockSpec((1, tm, QK_WIDTH), lambda bi, i: (bi, i, 0))
    u_spec = pl.BlockSpec((n_pairs, 1, tm // CHUNK, CHUNK * PAIR_W), lambda bi, i: (0, bi, i, 0))
    return pl.pallas_call(
        _inproj_kernel,
        out_shape=outs,
        grid=(b, n // tm),
        in_specs=[pl.BlockSpec((1, tm, d), lambda bi, i: (bi, i, 0)),
                  pl.BlockSpec((1, 1, d), mod_map),
                  pl.BlockSpec((1, 1, d), mod_map),
                  pl.BlockSpec((1, d), const2),
                  pl.BlockSpec((d, wn), const2),
                  pl.BlockSpec((1, LANES), const2),
                  pl.BlockSpec((1, LANES), const2),
                  pl.BlockSpec((tm, LANES), lambda bi, i: (i, 0)),
                  pl.BlockSpec((tm, LANES), lambda bi, i: (i, 0)),
                  pl.BlockSpec((QK_WIDTH, QK_WIDTH), const2)],
        out_specs=[tok_spec] * 3 + [u_spec],
        scratch_shapes=[pltpu.VMEM((QK_WIDTH // LANES, tm, LANES), F32)],
        compiler_params=_cparams(("parallel", "parallel")),
        name=name,
    )(x, sh, sc, g, w_bf, qg, kg, cosf, sinf, bd)


def _attn_kernel(lam_ref, q_ref, kl_ref, vl_ref, kc_ref, vc_ref, sg_ref, o_ref, a1_ref, a2_ref, m1_ref, m2_ref,
                 *, tk, out_scale, bounded):
    q = q_ref[0]
    lane = lax.broadcasted_iota(jnp.int32, q.shape, 1)
    zero = jnp.zeros_like(q)
    qa = jnp.where(lane < HEAD_DIM, q, zero)
    qb = jnp.where(lane >= HEAD_DIM, q, zero)

    a1_ref[...] = jnp.zeros(a1_ref.shape, F32)
    a2_ref[...] = jnp.zeros(a2_ref.shape, F32)
    if not bounded:
        m1_ref[...] = jnp.full(m1_ref.shape, -jnp.inf, F32)
        m2_ref[...] = jnp.full(m2_ref.shape, -jnp.inf, F32)

    def ones_col(rows):
        col = lax.broadcasted_iota(jnp.int32, (rows, LANES), 1)
        return jnp.where(col == 0, 1.0, 0.0).astype(BF16)

    def one_map(qm, kc, va, a_ref, m_ref):
        s = lax.dot_general(qm, kc, (((1,), (1,)), ((), ())), preferred_element_type=F32)
        if bounded:
            a_ref[...] += _dot(jnp.exp2(s).astype(BF16), va)
        else:
            m_prev = m_ref[...]
            m_next = jnp.maximum(m_prev, jnp.max(s, axis=1, keepdims=True))
            p = jnp.exp2(s - m_next[:, :1])
            alpha = jnp.exp2(m_prev - m_next)
            a_ref[...] = jnp.concatenate([alpha, alpha], axis=1) * a_ref[...] + _dot(p.astype(BF16), va)
            m_ref[...] = m_next

    def step(kc, vc, ones):
        va = jnp.concatenate([vc, ones], axis=1)
        one_map(qa, kc, va, a1_ref, m1_ref)
        one_map(qb, kc, va, a2_ref, m2_ref)

    ones_lat = ones_col(tk)

    def lat_body(j, carry):
        off = pl.multiple_of(j * tk, tk)
        step(kl_ref[0, pl.ds(off, tk), :], vl_ref[0, pl.ds(off, tk), :], ones_lat)
        return carry

    lax.fori_loop(0, kl_ref.shape[1] // tk, lat_body, 0)
    step(kc_ref[0], vc_ref[0], ones_col(kc_ref.shape[1]))

    lam = lam_ref[...]
    a1, a2 = a1_ref[...], a2_ref[...]
    o = a1[:, :LANES] / a1[:, LANES:LANES + 1] - lam * (a2[:, :LANES] / a2[:, LANES:LANES + 1])
    ms = jnp.mean(o * o, axis=-1, keepdims=True)
    o = o * lax.rsqrt(ms + RMS_EPS) * sg_ref[...]
    o_ref[0] = (o * out_scale).astype(BF16)


def _attn_cfg(n_lat):
    return dict(tq=_pick_tile(n_lat, ATTN_TQ), tk=_pick_tile(n_lat, ATTN_TK))


def _attn_call(lam_row, q, k_lat, v_lat, k_ctx, v_ctx, sg, tq, tk, out_scale=1.0, bounded=True):
    b, n, _ = q.shape
    nc = k_ctx.shape[1]
    kv_lat = pl.BlockSpec((1, n, LANES), lambda bi, h, i: (bi, 0, h))
    kv_ctx = pl.BlockSpec((1, nc, LANES), lambda bi, h, i: (bi, 0, h))
    q_spec = pl.BlockSpec((1, tq, LANES), lambda bi, h, i: (bi, i, h))
    row = pl.BlockSpec((1, LANES), lambda bi, h, i: (0, 0))
    acc = pltpu.VMEM((tq, 2 * LANES), F32)
    run_max = pltpu.VMEM((tq, LANES), F32)
    return pl.pallas_call(
        functools.partial(_attn_kernel, tk=tk, out_scale=out_scale, bounded=bounded),
        out_shape=jax.ShapeDtypeStruct((b, n, V_WIDTH), BF16),
        grid=(b, HEADS, n // tq),
        in_specs=[row, q_spec, kv_lat, kv_lat, kv_ctx, kv_ctx, row],
        out_specs=q_spec,
        scratch_shapes=[acc, acc, run_max, run_max],
        compiler_params=_cparams(("parallel", "parallel", "arbitrary")),
        name="attn" if bounded else "attn_general",
    )(lam_row, q, k_lat, v_lat, k_ctx, v_ctx, sg)


def _s5_kernel(ul_ref, uc_ref, win_ref, m_ref, wo_ref, lam_ref, o_ref, xl_ref, xc_ref, s_ref, *, nb):
    ul = ul_ref[0]
    n_slab = xl_ref.shape[0]

    def to_slabs(x_ref, x):
        for k in range(n_slab):
            x_ref[k] = x[:, k * LANES:(k + 1) * LANES]

    to_slabs(xl_ref, _dot(ul, win_ref[0]))
    to_slabs(xc_ref, _dot(uc_ref[0], win_ref[0]))
    n_lat = ul.shape[0] // nb
    n_ctx = uc_ref.shape[1] // nb

    lam = lam_ref[0]
    lfr, lfi, lbr, lbi = (jnp.broadcast_to(lam[i:i + 1], (nb, LANES)) for i in range(4))

    def rows(c, n_chunks):
        return pl.ds(c, nb, stride=n_chunks)

    def advance(x_ref, c, n_chunks, slab, ar, ai, sr, si):
        xr = x_ref[slab, rows(c, n_chunks), :]
        xi = x_ref[slab + 1, rows(c, n_chunks), :]
        return ar * sr - ai * si + xr, ar * si + ai * sr + xi

    def ctx_body(i, carry):
        fr, fi, br, bi = carry
        fr, fi = advance(xc_ref, i, n_ctx, 0, lfr, lfi, fr, fi)
        br, bi = advance(xc_ref, n_ctx - 1 - i, n_ctx, 2, lbr, lbi, br, bi)
        return fr, fi, br, bi

    def lat_body(i, carry):
        fr, fi, br, bi = carry
        cb = n_lat - 1 - i
        s_ref[0, pl.ds(pl.multiple_of(i * nb, nb), nb), :] = fr
        s_ref[1, pl.ds(pl.multiple_of(i * nb, nb), nb), :] = fi
        s_ref[2, pl.ds(pl.multiple_of(cb * nb, nb), nb), :] = br
        s_ref[3, pl.ds(pl.multiple_of(cb * nb, nb), nb), :] = bi
        fr, fi = advance(xl_ref, i, n_lat, 0, lfr, lfi, fr, fi)
        br, bi = advance(xl_ref, cb, n_lat, 2, lbr, lbi, br, bi)
        return fr, fi, br, bi

    z = jnp.zeros((nb, LANES), F32)
    carry = lax.fori_loop(0, n_ctx, ctx_body, (z, z, z, z), unroll=SCAN_UNROLL)
    lax.fori_loop(0, n_lat, lat_body, carry, unroll=SCAN_UNROLL)

    def batch_major(k):
        return jnp.concatenate([s_ref[k, pl.ds(bi, n_lat, stride=nb), :] for bi in range(nb)], axis=0)

    s_in = jnp.concatenate([batch_major(k).astype(BF16) for k in range(n_slab)], axis=1)
    y = _dot(ul, m_ref[0]) + _dot(s_in, wo_ref[0])
    o_ref[0] = jax.nn.gelu(y).astype(BF16)


def _s5_call(u_lat, u_ctx, win, m, wo, lam16, b):
    npair, rows_lat, kw = u_lat.shape
    nb = S5_BATCH_BLOCK if b % S5_BATCH_BLOCK == 0 else b
    rl = rows_lat // b * nb
    rc = u_ctx.shape[1] // b * nb
    wspec = pl.BlockSpec((1, kw, kw), lambda p, h: (p, 0, 0))
    slabs = lambda r: pltpu.VMEM((kw // LANES, r, LANES), F32)
    return pl.pallas_call(
        functools.partial(_s5_kernel, nb=nb),
        out_shape=jax.ShapeDtypeStruct((npair, rows_lat, kw), BF16),
        grid=(npair, b // nb),
        in_specs=[pl.BlockSpec((1, rl, kw), lambda p, h: (p, h, 0)),
                  pl.BlockSpec((1, rc, kw), lambda p, h: (p, h, 0)),
                  wspec, wspec, wspec,
                  pl.BlockSpec((1, 4, LANES), lambda p, h: (p, 0, 0))],
        out_specs=pl.BlockSpec((1, rl, kw), lambda p, h: (p, h, 0)),
        scratch_shapes=[slabs(rl), slabs(rc), slabs(rl)],
        compiler_params=_cparams(("parallel", "parallel")),
        name="s5",
    )(u_lat, u_ctx, win, m, wo, lam16)


def _s5_weights(a_re, a_im, log_dt, b_re, b_im, c_re, c_im, d_skip):
    hp = lax.Precision.HIGHEST
    g_n, p_n = a_re.shape[1], a_re.shape[2]
    t_n = CHUNK
    dt = jnp.exp(log_dt.astype(F32))[..., None]
    ar, ai = a_re.astype(F32), a_im.astype(F32)
    mag = jnp.exp(ar * dt)
    lr, li = mag * jnp.cos(ai * dt), mag * jnp.sin(ai * dt)
    den = ar * ar + ai * ai
    nr, ni = lr - 1.0, li
    cr = (nr * ar + ni * ai) / den
    ci = (ni * ar - nr * ai) / den
    bbr = cr[..., None] * b_re - ci[..., None] * b_im
    bbi = cr[..., None] * b_im + ci[..., None] * b_re
    n = jnp.arange(t_n + 1, dtype=F32)[:, None, None, None]
    pm = jnp.exp(n * (ar * dt))
    pw_r, pw_i = pm * jnp.cos(n * (ai * dt)), pm * jnp.sin(n * (ai * dt))
    lb_r = pw_r[:t_n, ..., None] * bbr - pw_i[:t_n, ..., None] * bbi
    lb_i = pw_r[:t_n, ..., None] * bbi + pw_i[:t_n, ..., None] * bbr
    cre, cim = c_re.astype(F32), c_im.astype(F32)
    kern = (jnp.einsum('gip,tdgpj->tdgij', cre, lb_r, precision=hp)
            - jnp.einsum('gip,tdgpj->tdgij', cim, lb_i, precision=hp))
    hh = SSM_GROUP
    npair = g_n // 2
    eye2 = jnp.eye(2, dtype=F32)
    kw = 2 * t_n * hh

    lag0 = kern[0, 0] + kern[0, 1] + jnp.eye(hh, dtype=F32)[None] * d_skip.astype(F32)[:, :, None]
    by_lag = jnp.concatenate([kern[:0:-1, 1], lag0[None], kern[1:, 0]], axis=0)
    strip = jnp.einsum('ab,pajlq->pajlbq', eye2,
                       jnp.transpose(by_lag, (1, 3, 0, 2)).reshape(npair, 2, hh, 2 * t_n - 1, hh))
    strip = strip.reshape(npair, 2 * hh, (2 * t_n - 1) * 2 * hh)
    m_pair = jnp.stack([strip[:, :, (t_n - 1 - s) * 2 * hh:(t_n - 1 - s) * 2 * hh + kw] for s in range(t_n)],
                       axis=1).reshape(npair, kw, kw)

    def state_lanes(parts, mask_pair):
        x = jnp.stack(parts, axis=2).reshape(npair, 2, parts[0].shape[1], 4, 1, p_n)
        sel = eye2[None, :, None, None, :, None] if mask_pair else jnp.ones((1, 1, 1, 1, 2, 1), F32)
        return (x * sel).reshape(npair, 2, parts[0].shape[1], 8 * p_n)

    gsp = lambda z: jnp.transpose(z, (1, 0, 2))
    pf_r, pf_i, pb_r, pb_i = pw_r[:, 0], pw_i[:, 0], pw_r[:, 1], pw_i[:, 1]
    pa = state_lanes([gsp(pf_r[t_n - 1::-1]), gsp(pf_r[t_n - 1::-1]), gsp(pb_r[:t_n]), gsp(pb_r[:t_n])], True)
    pb = state_lanes([-gsp(pf_i[t_n - 1::-1]), gsp(pf_i[t_n - 1::-1]), -gsp(pb_i[:t_n]), gsp(pb_i[:t_n])], True)
    bt = lambda z: jnp.transpose(z, (0, 2, 1))
    ba = state_lanes([bt(bbr[0]), bt(bbi[0]), bt(bbr[1]), bt(bbi[1])], False)
    bb = state_lanes([bt(bbi[0]), bt(bbr[0]), bt(bbi[1]), bt(bbr[1])], False)
    sa = lambda z: jnp.transpose(z, (0, 2, 1, 3))
    win_pair = (sa(pa)[:, :, :, None, :] * ba[:, None] + sa(pb)[:, :, :, None, :] * bb[:, None]
                ).reshape(npair, kw, 8 * p_n)
    qa = state_lanes([gsp(pf_r[1:]), -gsp(pf_i[1:]), gsp(pb_r[t_n:0:-1]), -gsp(pb_i[t_n:0:-1])], True)
    qb = state_lanes([-gsp(pf_i[1:]), -gsp(pf_r[1:]), -gsp(pb_i[t_n:0:-1]), -gsp(pb_r[t_n:0:-1])], True)
    ca = state_lanes([cre] * 4, False)
    cb = state_lanes([cim] * 4, False)
    wo_t = (sa(qa)[:, :, :, None, :] * ca[:, None] + sa(qb)[:, :, :, None, :] * cb[:, None]
            ).reshape(npair, kw, 8 * p_n)
    wo_pair = jnp.swapaxes(wo_t.astype(BF16), 1, 2)
    lam16 = jnp.stack([pw_r[t_n, 0], pw_i[t_n, 0], pw_r[t_n, 1], pw_i[t_n, 1]], axis=1)
    lam16 = jnp.transpose(lam16.reshape(npair, 2, 4, p_n), (0, 2, 1, 3)).reshape(npair, 4, 2 * p_n)
    return win_pair.astype(BF16), m_pair.astype(BF16), wo_pair, lam16


def _outproj_kernel(a_ref, y_ref, x_ref, g1_ref, sh_ref, sc_ref, n2_ref, wglu_ref, bglu_ref, wout_ref,
                    wr_hi_ref, wr_lo_ref, br_ref, x1_ref, h2_ref, gate_ref, ys_ref):
    _chunk_rows_to_token_major(
        lambda pr, k: y_ref[pr, 0, :, k * LANES:(k + 1) * LANES].astype(F32), ys_ref)
    yf = jnp.concatenate([ys_ref[q] for q in range(ys_ref.shape[0])], axis=1)
    y = yf.astype(BF16)
    z = _dot(y, wglu_ref[...]) + bglu_ref[...]
    s = (yf * jax.nn.sigmoid(z)).astype(BF16)
    half = a_ref.shape[2]
    o = _dot(a_ref[0], wout_ref[:half, :]) + _dot(s, wout_ref[half:, :])
    x1 = x_ref[0] + g1_ref[0] * o
    x1_ref[0] = x1

    ms = jnp.mean(x1 * x1, axis=-1, keepdims=True)
    h = (x1 * lax.rsqrt(ms + RMS_EPS)) * n2_ref[...]
    h = h * (1.0 + sc_ref[0]) + sh_ref[0]
    for j in range(h.shape[1] // LANES):
        h2_ref[pl.ds(j, h.shape[0], stride=ROW_TILE), :] = h[:, j * LANES:(j + 1) * LANES]

    h_hi, h_lo = _split_bf16(h)
    lg = _dot(h_hi, wr_hi_ref[...]) + (_dot(h_hi, wr_lo_ref[...]) + _dot(h_lo, wr_hi_ref[...])) + br_ref[...]
    lane = lax.broadcasted_iota(jnp.int32, lg.shape, 1)
    neg = jnp.float32(-jnp.inf)
    big = jnp.int32(LANES)

    def top1(vals):
        vmax = jnp.max(vals, axis=1, keepdims=True)
        idx = jnp.min(jnp.where(vals == vmax, lane, big), axis=1, keepdims=True)
        return vmax, idx

    is_grp = lane < MOE_GROUPS
    g_vals = jnp.where(is_grp, lg, neg)
    g_max, g_idx = top1(g_vals)
    p_grp = 1.0 / jnp.sum(jnp.where(is_grp, jnp.exp(g_vals - g_max), 0.0), axis=1, keepdims=True)
    e_lo = MOE_GROUPS + EXPERTS_PER_GROUP * g_idx
    in_grp = (lane >= e_lo) & (lane < e_lo + EXPERTS_PER_GROUP)
    e_vals = jnp.where(in_grp, lg, neg)
    v1, i1 = top1(e_vals)
    v2, i2 = top1(jnp.where(lane == i1, neg, e_vals))
    r = jnp.exp(v2 - v1)
    w1 = p_grp / (1.0 + r)
    w2 = w1 * r
    first_lo = i1 < i2
    a_loc = jnp.where(first_lo, i1, i2) - e_lo
    b_loc = jnp.where(first_lo, i2, i1) - e_lo
    pair = ((a_loc * (2 * EXPERTS_PER_GROUP - 1 - a_loc)) >> 1) + (b_loc - a_loc - 1)
    cls = g_idx * PAIRS_PER_GROUP + pair
    w_lo = jnp.where(first_lo, w1, w2)
    w_hi = jnp.where(first_lo, w2, w1)
    gate_ref[0] = (jnp.where(lane == 0, w_lo, 0.0) + jnp.where(lane == 1, w_hi, 0.0)
                   + jnp.where(lane == 2, cls.astype(F32), 0.0))


def _outproj_call(a, yg, x, g1, sh2, sc2, n2g, wglu, bglu, wout, wr_hi, wr_lo, br, tm):
    b, n, d = x.shape
    half = a.shape[2]
    tok = lambda w: pl.BlockSpec((1, tm, w), lambda bi, i: (bi, i, 0))
    mod = pl.BlockSpec((1, 1, d), lambda bi, i: (bi, 0, 0))
    const = lambda r, c: pl.BlockSpec((r, c), lambda bi, i: (0, 0))
    return pl.pallas_call(
        _outproj_kernel,
        out_shape=[jax.ShapeDtypeStruct((b, n, d), F32), jax.ShapeDtypeStruct((b * n * ROW_TILE, LANES), F32),
                   jax.ShapeDtypeStruct((b, n, LANES), F32)],
        grid=(b, n // tm),
        in_specs=[tok(half),
                  pl.BlockSpec((yg.shape[0], 1, tm // CHUNK, yg.shape[3]), lambda bi, i: (0, bi, i, 0)),
                  tok(d), mod, mod, mod, const(1, d),
                  const(half, half), const(1, half), const(d, d),
                  const(d, LANES), const(d, LANES), const(1, LANES)],
        out_specs=[tok(d), pl.BlockSpec((tm * ROW_TILE, LANES), lambda bi, i: (bi * (n // tm) + i, 0)), tok(LANES)],
        scratch_shapes=[pltpu.VMEM((half // LANES, tm, LANES), F32)],
        compiler_params=_cparams(("parallel", "parallel")),
        name="outproj",
    )(a, yg, x, g1, sh2, sc2, n2g, wglu, bglu, wout, wr_hi, wr_lo, br)


def _token_rows(tok, rows_per_token):
    return pl.ds(pl.multiple_of(tok * rows_per_token, rows_per_token), rows_per_token)


def _moe_expert_kernel(ea_ref, eb_ref, nv_ref, nt_ref, idx_hbm, h_hbm, wga_ref, wua_ref, wda_ref,
                       wgb_ref, wub_ref, wdb_ref, y_hbm, idx_smem, hbuf, ybuf, gsem, isem, psem, *, tm):
    i = pl.program_id(0)
    n_steps = pl.num_programs(0)
    n_tiles = nt_ref[0]
    hrows, yrows = tm * ROW_TILE, tm * 2 * ROW_TILE

    def idx_copy(tile):
        s = tile % IDX_SLOTS
        return pltpu.make_async_copy(idx_hbm.at[pl.ds(tile, 1), :], idx_smem.at[pl.ds(s, 1), :], isem.at[s])

    def pull_row(tile, r):
        tok = idx_smem[tile % IDX_SLOTS, r]
        s2 = tile % 3
        return pltpu.make_async_copy(h_hbm.at[_token_rows(tok, ROW_TILE), :],
                                     hbuf.at[pl.ds(s2 * hrows + r * ROW_TILE, ROW_TILE), :], gsem.at[s2])

    def pull_tile(tile):
        def body(r, carry):
            pull_row(tile, r).start()
            return carry
        lax.fori_loop(0, tm, body, 0, unroll=PUSH_UNROLL)

    def push_row(tile, r):
        tok = idx_smem[tile % IDX_SLOTS, r]
        s3 = tile % 3
        return pltpu.make_async_copy(ybuf.at[pl.ds(s3 * yrows + r * 2 * ROW_TILE, 2 * ROW_TILE), :],
                                     y_hbm.at[_token_rows(tok, 2 * ROW_TILE), :], psem.at[s3])

    def wait_pulls(tile):
        s2 = tile % 3
        pltpu.make_async_copy(h_hbm.at[pl.ds(0, hrows), :], hbuf.at[pl.ds(s2 * hrows, hrows), :], gsem.at[s2]).wait()

    def wait_pushes(tile):
        s3 = tile % 3
        n = nv_ref[tile] * (2 * ROW_TILE)

        @pl.when(n > 0)
        def _():
            pltpu.make_async_copy(ybuf.at[pl.ds(s3 * yrows, n), :], y_hbm.at[pl.ds(0, n), :], psem.at[s3]).wait()

    @pl.when(i == 0)
    def _():
        idx_copy(0).start()
        idx_copy(1).start()
        idx_copy(0).wait()
        idx_copy(1).wait()
        pull_tile(0)

        @pl.when(1 < n_tiles)
        def _():
            pull_tile(1)
        idx_copy(2).start()

    @pl.when(i + 3 < n_steps)
    def _():
        idx_copy(i + 3).start()

    @pl.when(i + 2 < n_steps)
    def _():
        idx_copy(i + 2).wait()

    @pl.when(i < n_tiles)
    def _():
        wait_pulls(i)

        @pl.when(i >= 3)
        def _():
            wait_pushes(i - 3)

        s3 = i % 3
        h = jnp.concatenate([hbuf[pl.ds(s3 * hrows + j, tm, stride=ROW_TILE), :] for j in range(ROW_TILE)],
                            axis=1).astype(BF16)
        pull_next = i + 2 < n_tiles
        n_push = jnp.where(i >= 1, nv_ref[jnp.maximum(i - 1, 0)], 0)

        def issue_rows(r0, r1):
            for r in range(r0, r1):
                @pl.when(pull_next)
                def _():
                    pull_row(i + 2, r).start()

                @pl.when(r < n_push)
                def _():
                    push_row(i - 1, r).start()

        def expert(wg_ref, wu_ref, wd_ref, row0):
            hid = jax.nn.silu(_dot(h, wg_ref[0])) * _dot(h, wu_ref[0])
            y = _dot(hid.astype(BF16), wd_ref[0])
            for j in range(ROW_TILE):
                ybuf[pl.ds(s3 * yrows + row0 + j, tm, stride=2 * ROW_TILE), :] = y[:, j * LANES:(j + 1) * LANES]

        issue_rows(0, tm // 2)
        expert(wga_ref, wua_ref, wda_ref, 0)
        issue_rows(tm // 2, tm)
        expert(wgb_ref, wub_ref, wdb_ref, ROW_TILE)

    @pl.when(i == n_tiles)
    def _():
        def body(r, carry):
            push_row(i - 1, r).start()
            return carry
        lax.fori_loop(0, nv_ref[i - 1], body, 0)

        for back in (3, 2, 1):
            @pl.when(i >= back)
            def _():
                wait_pushes(i - back)


def _moe_expert_call(ea, eb, nv, nt, idx, h_tiles, wg, wu, wd, n_tokens):
    n_steps, tm = idx.shape
    ne, d, f = wg.shape
    amap = lambda i, ea, eb, nv, nt: (ea[i], 0, 0)
    bmap = lambda i, ea, eb, nv, nt: (eb[i], 0, 0)
    up = lambda m: pl.BlockSpec((1, d, f), m)
    down = lambda m: pl.BlockSpec((1, f, d), m)
    hbm = pl.BlockSpec(memory_space=pl.ANY)
    grid_spec = pltpu.PrefetchScalarGridSpec(
        num_scalar_prefetch=4,
        grid=(n_steps,),
        in_specs=[hbm, hbm, up(amap), up(amap), down(amap), up(bmap), up(bmap), down(bmap)],
        out_specs=hbm,
        scratch_shapes=[pltpu.SMEM((IDX_SLOTS, tm), jnp.int32),
                        pltpu.VMEM((3 * tm * ROW_TILE, LANES), F32),
                        pltpu.VMEM((3 * tm * 2 * ROW_TILE, LANES), F32),
                        pltpu.SemaphoreType.DMA((3,)), pltpu.SemaphoreType.DMA((IDX_SLOTS,)),
                        pltpu.SemaphoreType.DMA((3,))])
    return pl.pallas_call(
        functools.partial(_moe_expert_kernel, tm=tm),
        out_shape=jax.ShapeDtypeStruct((n_tokens * 2 * ROW_TILE, LANES), F32),
        grid_spec=grid_spec,
        compiler_params=_cparams(("arbitrary",)),
        name="moe_experts",
    )(ea, eb, nv, nt, idx, h_tiles, wg, wu, wd, wg, wu, wd)


def _moe_combine_kernel(y_ref, route_ref, x1_ref, g2_ref, o_ref):
    tm = x1_ref.shape[0]
    ya = jnp.concatenate([y_ref[pl.ds(j, tm, stride=2 * ROW_TILE), :] for j in range(ROW_TILE)], axis=1)
    yb = jnp.concatenate([y_ref[pl.ds(ROW_TILE + j, tm, stride=2 * ROW_TILE), :] for j in range(ROW_TILE)], axis=1)
    route = route_ref[...]
    moe = route[:, 0:1] * ya + route[:, 1:2] * yb
    o_ref[...] = x1_ref[...] + g2_ref[0] * moe


def _moe_combine_call(y_tiles, route, x1, g2, tokens_per_batch, tm):
    t, d = x1.shape
    per_b = tokens_per_batch // tm
    tok = lambda w: pl.BlockSpec((tm, w), lambda i: (i, 0))
    return pl.pallas_call(
        _moe_combine_kernel,
        out_shape=jax.ShapeDtypeStruct((t, d), F32),
        grid=(t // tm,),
        in_specs=[pl.BlockSpec((tm * 2 * ROW_TILE, LANES), lambda i: (i, 0)), tok(LANES), tok(d),
                  pl.BlockSpec((1, 1, d), lambda i: (i // per_b, 0, 0))],
        out_specs=tok(d),
        compiler_params=_cparams(("parallel",)),
        name="moe_combine",
    )(y_tiles, route, x1, g2)


def _routing_plan(cls, tm):
    t = cls.shape[0]
    n_steps = t // tm + N_CLASSES + 1
    order = jnp.argsort(cls).astype(jnp.int32)
    classes = jnp.arange(N_CLASSES, dtype=jnp.int32)
    counts = jnp.sum((cls[:, None] == classes[None, :]).astype(jnp.int32), axis=0)
    cstart = jnp.cumsum(counts) - counts
    tiles_c = (counts + tm - 1) // tm
    tile_end = jnp.cumsum(tiles_c)
    n_tiles = tile_end[-1]
    tile_ids = jnp.arange(n_steps, dtype=jnp.int32)
    live = tile_ids < n_tiles
    c_of = jnp.sum((tile_end[None, :] <= jnp.minimum(tile_ids, n_tiles - 1)[:, None]).astype(jnp.int32), axis=1)
    k_of = jnp.minimum(tile_ids, n_tiles - 1) - (tile_end - tiles_c)[c_of]
    nv = jnp.where(live, jnp.clip(counts[c_of] - k_of * tm, 0, tm), 0).astype(jnp.int32)
    base = cstart[c_of] + k_of * tm
    idx = order[jnp.minimum(base[:, None] + jnp.arange(tm, dtype=jnp.int32)[None, :], t - 1)]
    grp, pair = c_of // PAIRS_PER_GROUP, c_of % PAIRS_PER_GROUP
    a_tab = jnp.array([a for a in range(EXPERTS_PER_GROUP) for _ in range(a + 1, EXPERTS_PER_GROUP)], jnp.int32)
    b_tab = jnp.array([b for a in range(EXPERTS_PER_GROUP) for b in range(a + 1, EXPERTS_PER_GROUP)], jnp.int32)
    ea = grp * EXPERTS_PER_GROUP + a_tab[pair]
    eb = grp * EXPERTS_PER_GROUP + b_tab[pair]
    return ea.astype(jnp.int32), eb.astype(jnp.int32), nv, n_tiles.reshape(1).astype(jnp.int32), idx


def _rope_tables(n_tokens):
    rows = n_tokens // GRID_W
    row = jnp.broadcast_to(jnp.arange(rows, dtype=F32)[:, None], (rows, GRID_W)).reshape(-1)
    col = jnp.broadcast_to(jnp.arange(GRID_W, dtype=F32)[None, :], (rows, GRID_W)).reshape(-1)
    half = HEAD_DIM // 2
    inv = ROPE_BASE ** (-jnp.arange(0, half, 2, dtype=F32) / half)
    ang = jnp.stack([row[:, None] * inv, col[:, None] * inv], axis=1)
    cos, sin = jnp.cos(ang), jnp.sin(ang)
    cos64 = jnp.concatenate([cos[:, 0], cos[:, 0], cos[:, 1], cos[:, 1]], axis=1)
    sin64 = jnp.concatenate([-sin[:, 0], sin[:, 0], -sin[:, 1], sin[:, 1]], axis=1)
    return jnp.tile(cos64, (1, LANES // HEAD_DIM)), jnp.tile(sin64, (1, LANES // HEAD_DIM))


def _pick_tile(n, target):
    t = min(n, target)
    while n % t:
        t //= 2
    return t


def kernel(x, c, ctx, c_ctx, w_ada, b_ada, norm1_g, w_in, q_norm_g, k_norm_g, lambda_q1, lambda_k1, lambda_q2, lambda_k2, subln_g, ssm_a_re, ssm_a_im, ssm_log_dt, ssm_b_re, ssm_b_im, ssm_c_re, ssm_c_im, ssm_d, w_glu, b_glu, w_out, norm2_g, w_route_group, b_route_group, w_route_expert, b_route_expert, w_exp_gate, w_exp_up, w_exp_down):
    depth = w_ada.shape[0]
    assert depth == 1, "single-layer block: the context stream is never updated"
    b, n_lat, d = x.shape
    n_ctx = ctx.shape[1]
    assert n_lat % CHUNK == 0 and n_ctx % CHUNK == 0 and n_lat % GRID_W == 0
    assert d == ROW_TILE * LANES, "MoE rows are moved as one (8, 128) tile per token"
    l = 0
    lam_init = 0.8 - 0.6 * math.exp(-0.3 * l)

    rows = b + 1
    rows_pad = -(-rows // 8) * 8
    cc = jnp.concatenate([c, c_ctx[None, :], jnp.zeros((rows_pad - rows, d), F32)], axis=0)
    mod = _mod_call(cc, w_ada[l], b_ada[l])
    sh1, sc1, g1, sh2, sc2, g2 = (mod[:b, i * d:(i + 1) * d].reshape(b, 1, d) for i in range(6))
    csh1, csc1 = (mod[b:b + 1, i * d:(i + 1) * d].reshape(1, 1, d) for i in range(2))

    w_in_bf = w_in[l].astype(BF16)
    bd = jnp.kron(jnp.eye(QK_WIDTH // HEAD_DIM, dtype=F32), jnp.ones((HEAD_DIM, HEAD_DIM), F32)).astype(BF16)
    qg = jnp.tile(q_norm_g[l], LANES // HEAD_DIM).reshape(1, LANES)
    kg = jnp.tile(k_norm_g[l], LANES // HEAD_DIM).reshape(1, LANES)
    cosf, sinf = _rope_tables(n_lat)
    ones_c, zeros_c = jnp.ones((n_ctx, LANES), F32), jnp.zeros((n_ctx, LANES), F32)
    g1n = norm1_g[l].reshape(1, d)
    tm = _pick_tile(n_lat, 512)
    q_x, k_x, v_x, u_x = _inproj_call(x, sh1, sc1, g1n, w_in_bf, qg, kg, cosf, sinf, bd, tm, "inproj_lat")
    _, k_c, v_c, u_c = _inproj_call(ctx, csh1, csc1, g1n, w_in_bf, qg, kg, ones_c, zeros_c, bd,
                                    _pick_tile(n_ctx, 512), "inproj_ctx")

    e1 = jnp.exp(jnp.sum(lambda_q1[l] * lambda_k1[l]))
    e2 = jnp.exp(jnp.sum(lambda_q2[l] * lambda_k2[l]))
    lam_row = jnp.full((1, LANES), e1 - e2 + lam_init, F32)
    score_bound = math.sqrt(HEAD_DIM) * jnp.max(jnp.abs(q_norm_g[l])) * jnp.max(jnp.abs(k_norm_g[l]))

    def attn(bounded):
        return lambda *ops: _attn_call(*ops, **_attn_cfg(n_lat), out_scale=1.0 - lam_init, bounded=bounded)

    a_x = lax.cond(score_bound <= SCORE_BOUND, attn(True), attn(False),
                   lam_row, q_x, k_x, v_x, k_c, v_c, subln_g[l].reshape(1, LANES))

    win, m_op, wo, lam16 = _s5_weights(ssm_a_re[l], ssm_a_im[l], ssm_log_dt[l], ssm_b_re[l], ssm_b_im[l],
                                       ssm_c_re[l], ssm_c_im[l], ssm_d[l])
    n_pairs, kw = u_x.shape[0], u_x.shape[3]
    yg = _s5_call(u_x.reshape(n_pairs, b * (n_lat // CHUNK), kw), u_c.reshape(n_pairs, b * (n_ctx // CHUNK), kw),
                  win, m_op, wo, lam16, b)
    yg = yg.reshape(n_pairs, b, n_lat // CHUNK, kw)

    wr = jnp.concatenate([w_route_group[l], w_route_expert[l]], axis=1)
    wr = jnp.pad(wr, ((0, 0), (0, LANES - wr.shape[1])))
    wr_hi, wr_lo = _split_bf16(wr)
    br = jnp.pad(jnp.concatenate([b_route_group[l], b_route_expert[l]]), (0, LANES - MOE_GROUPS - N_EXPERTS))
    x1, h2, route = _outproj_call(a_x, yg, x, g1, sh2, sc2, norm2_g[l].reshape(1, d),
                                 w_glu[l].astype(BF16), b_glu[l].reshape(1, -1), w_out[l].astype(BF16),
                                 wr_hi, wr_lo, br.reshape(1, LANES), tm)

    t_all = b * n_lat
    route = route.reshape(t_all, LANES)
    ea, eb, nv, n_tiles, idx = _routing_plan(route[:, 2].astype(jnp.int32), _pick_tile(t_all, MOE_TILE))
    y_tiles = _moe_expert_call(ea, eb, nv, n_tiles, idx, h2, w_exp_gate[l].astype(BF16),
                               w_exp_up[l].astype(BF16), w_exp_down[l].astype(BF16), t_all)
    out = _moe_combine_call(y_tiles, route, x1.reshape(t_all, d), g2, n_lat, _pick_tile(n_lat, COMBINE_TILE))
    return out.reshape(b, n_lat, d)
```

```python
import functools
import math

import jax
import jax.numpy as jnp
from jax import lax
from jax.experimental import pallas as pl
from jax.experimental.pallas import tpu as pltpu

F32 = jnp.float32
BF16 = jnp.bfloat16

LANES = 128
HEADS = 4
HEAD_DIM = 64
QK_WIDTH = HEADS * 2 * HEAD_DIM
V_WIDTH = HEADS * 2 * HEAD_DIM
GRID_W = 64
ROPE_BASE = 10000.0
SSM_GROUP = 16
SSM_STATE = 64
CHUNK = 16
MOE_GROUPS = 4
EXPERTS_PER_GROUP = 8
N_EXPERTS = MOE_GROUPS * EXPERTS_PER_GROUP
RMS_EPS = 1e-6
ATTN_TQ = 512
ATTN_TK = 4096
SCORE_BOUND = 60.0
SCAN_UNROLL = 8
S5_BATCH_BLOCK = 8
PAIRS_PER_GROUP = EXPERTS_PER_GROUP * (EXPERTS_PER_GROUP - 1) // 2
N_CLASSES = MOE_GROUPS * PAIRS_PER_GROUP
ROW_TILE = 8
MOE_TILE = 256
PUSH_UNROLL = 8
IDX_SLOTS = 5
COMBINE_TILE = 512
VMEM_LIMIT = 48 * 1024 * 1024


def _cparams(sem):
    return pltpu.CompilerParams(dimension_semantics=sem, vmem_limit_bytes=VMEM_LIMIT)


def _split_bf16(a):
    hi = a.astype(BF16)
    lo = (a - hi.astype(F32)).astype(BF16)
    return hi, lo


def _dot(a, b):
    return jnp.dot(a, b, preferred_element_type=F32)


def _dot3(a, b):
    a_hi, a_lo = _split_bf16(a)
    b_hi, b_lo = _split_bf16(b)
    return _dot(a_hi, b_hi) + (_dot(a_hi, b_lo) + _dot(a_lo, b_hi))


def _mod_kernel(c_ref, w_ref, b_ref, o_ref):
    c = c_ref[...]
    a = c * jax.nn.sigmoid(c)
    o_ref[...] = _dot3(a, w_ref[...]) + b_ref[...]


def _mod_call(cc, w_ada, b_ada):
    rows, d = cc.shape
    n = w_ada.shape[1]
    bn = 1024
    return pl.pallas_call(
        _mod_kernel,
        out_shape=jax.ShapeDtypeStruct((rows, n), F32),
        grid=(n // bn,),
        in_specs=[pl.BlockSpec((rows, d), lambda j: (0, 0)),
                  pl.BlockSpec((d, bn), lambda j: (0, j)),
                  pl.BlockSpec((1, bn), lambda j: (0, j))],
        out_specs=pl.BlockSpec((rows, bn), lambda j: (0, j)),
        compiler_params=_cparams(("arbitrary",)),
        name="mod",
    )(cc, w_ada, b_ada.reshape(1, n))


def _inproj_kernel(x_ref, sh_ref, sc_ref, g_ref, w_ref, qg_ref, kg_ref, cos_ref, sin_ref, bd_ref, perm_ref,
                   q_ref, k_ref, v_ref, u_ref):
    x = x_ref[0]
    ms = jnp.mean(x * x, axis=-1, keepdims=True)
    h = (x * lax.rsqrt(ms + RMS_EPS)) * g_ref[...]
    h = h * (1.0 + sc_ref[0]) + sh_ref[0]
    p = _dot(h.astype(BF16), w_ref[...])

    cosf = cos_ref[...]
    sinf = sin_ref[...]
    lane = lax.broadcasted_iota(jnp.int32, cosf.shape, 1)
    first_half = (lane % 32) < 16

    def norm_rope(t, gain, scale):
        ss = _dot((t * t).astype(BF16), bd_ref[...])
        t = t * lax.rsqrt(ss * (1.0 / HEAD_DIM) + RMS_EPS)
        outs = []
        for s in range(QK_WIDTH // LANES):
            ts = t[:, s * LANES:(s + 1) * LANES] * gain
            partner = jnp.where(first_half, pltpu.roll(ts, LANES - 16, 1), pltpu.roll(ts, 16, 1))
            outs.append(((ts * cosf + partner * sinf) * scale).astype(BF16))
        return jnp.concatenate(outs, axis=1)

    q_ref[0] = norm_rope(p[:, :QK_WIDTH], qg_ref[...], HEAD_DIM ** -0.5 * math.log2(math.e))
    k_ref[0] = norm_rope(p[:, QK_WIDTH:2 * QK_WIDTH], kg_ref[...], 1.0)
    v_ref[0] = p[:, 2 * QK_WIDTH:2 * QK_WIDTH + V_WIDTH].astype(BF16)
    u_t = _dot(perm_ref[...], p[:, 2 * QK_WIDTH + V_WIDTH:].astype(BF16))

    def store_u(pr, k, val):
        u_ref[pr, 0, :, k * LANES:(k + 1) * LANES] = val.astype(BF16)

    _step_major_to_chunk_rows(u_t, store_u)


PAIR_W = 2 * SSM_GROUP
PAIRS_PER_TILE = LANES // PAIR_W


def _quarter_select(pieces):
    lane = lax.broadcasted_iota(jnp.int32, pieces[0].shape, 1)
    acc = pieces[0]
    for r in range(1, len(pieces)):
        acc = jnp.where(lane // PAIR_W == r, pieces[r], acc)
    return acc


def _step_perm(tm):
    n_chunk = tm // CHUNK
    src = (jnp.arange(tm) % n_chunk) * CHUNK + jnp.arange(tm) // n_chunk
    return (src[:, None] == jnp.arange(tm)[None, :]).astype(BF16)


def _step_major_to_chunk_rows(u_t, store):
    n_chunk = u_t.shape[0] // CHUNK
    for k in range(CHUNK // PAIRS_PER_TILE):
        for pr in range(u_t.shape[1] // PAIR_W):
            q, r_src = divmod(pr, PAIRS_PER_TILE)
            pieces = []
            for r in range(PAIRS_PER_TILE):
                t = PAIRS_PER_TILE * k + r
                src = u_t[t * n_chunk:(t + 1) * n_chunk, q * LANES:(q + 1) * LANES]
                shift = ((r - r_src) % PAIRS_PER_TILE) * PAIR_W
                pieces.append(pltpu.roll(src, shift, 1) if shift else src)
            store(pr, k, _quarter_select(pieces))


def _chunk_rows_to_token_major(load, ys_ref):
    n_tiles, n_chunk = ys_ref.shape[0], ys_ref.shape[1] // CHUNK
    for t in range(CHUNK):
        k, r_src = divmod(t, PAIRS_PER_TILE)
        for q in range(n_tiles):
            pieces = []
            for r in range(PAIRS_PER_TILE):
                shift = ((r - r_src) % PAIRS_PER_TILE) * PAIR_W
                src = load(PAIRS_PER_TILE * q + r, k)
                pieces.append(pltpu.roll(src, shift, 1) if shift else src)
            ys_ref[q, pl.ds(t, n_chunk, stride=CHUNK), :] = _quarter_select(pieces)


def _inproj_call(x, sh, sc, g, w_bf, qg, kg, cosf, sinf, bd, tm, name):
    b, n, d = x.shape
    wn = w_bf.shape[1]
    per_batch = sh.shape[0] > 1
    mod_map = (lambda bi, i: (bi, 0, 0)) if per_batch else (lambda bi, i: (0, 0, 0))
    const2 = lambda bi, i: (0, 0)
    n_pairs = QK_WIDTH // PAIR_W
    outs = [jax.ShapeDtypeStruct((b, n, QK_WIDTH), BF16)] * 3 + [
        jax.ShapeDtypeStruct((n_pairs, b, n // CHUNK, CHUNK * PAIR_W), BF16)]
    tok_spec = pl.BlockSpec((1, tm, QK_WIDTH), lambda bi, i: (bi, i, 0))
    u_spec = pl.BlockSpec((n_pairs, 1, tm // CHUNK, CHUNK * PAIR_W), lambda bi, i: (0, bi, i, 0))
    return pl.pallas_call(
        _inproj_kernel,
        out_shape=outs,
        grid=(b, n // tm),
        in_specs=[pl.BlockSpec((1, tm, d), lambda bi, i: (bi, i, 0)),
                  pl.BlockSpec((1, 1, d), mod_map),
                  pl.BlockSpec((1, 1, d), mod_map),
                  pl.BlockSpec((1, d), const2),
                  pl.BlockSpec((d, wn), const2),
                  pl.BlockSpec((1, LANES), const2),
                  pl.BlockSpec((1, LANES), const2),
                  pl.BlockSpec((tm, LANES), lambda bi, i: (i, 0)),
                  pl.BlockSpec((tm, LANES), lambda bi, i: (i, 0)),
                  pl.BlockSpec((QK_WIDTH, QK_WIDTH), const2),
                  pl.BlockSpec((tm, tm), const2)],
        out_specs=[tok_spec] * 3 + [u_spec],
        compiler_params=_cparams(("parallel", "parallel")),
        name=name,
    )(x, sh, sc, g, w_bf, qg, kg, cosf, sinf, bd, _step_perm(tm))


def _attn_kernel(lam_ref, q_ref, kl_ref, vl_ref, kc_ref, vc_ref, sg_ref, o_ref, a1_ref, a2_ref, m1_ref, m2_ref,
                 *, tk, out_scale, bounded):
    q = q_ref[0]
    lane = lax.broadcasted_iota(jnp.int32, q.shape, 1)
    zero = jnp.zeros_like(q)
    qa = jnp.where(lane < HEAD_DIM, q, zero)
    qb = jnp.where(lane >= HEAD_DIM, q, zero)

    a1_ref[...] = jnp.zeros(a1_ref.shape, F32)
    a2_ref[...] = jnp.zeros(a2_ref.shape, F32)
    if not bounded:
        m1_ref[...] = jnp.full(m1_ref.shape, -jnp.inf, F32)
        m2_ref[...] = jnp.full(m2_ref.shape, -jnp.inf, F32)

    def ones_col(rows):
        col = lax.broadcasted_iota(jnp.int32, (rows, LANES), 1)
        return jnp.where(col == 0, 1.0, 0.0).astype(BF16)

    def one_map(qm, kc, va, a_ref, m_ref):
        s = lax.dot_general(qm, kc, (((1,), (1,)), ((), ())), preferred_element_type=F32)
        if bounded:
            a_ref[...] += _dot(jnp.exp2(s).astype(BF16), va)
        else:
            m_prev = m_ref[...]
            m_next = jnp.maximum(m_prev, jnp.max(s, axis=1, keepdims=True))
            p = jnp.exp2(s - m_next[:, :1])
            alpha = jnp.exp2(m_prev - m_next)
            a_ref[...] = jnp.concatenate([alpha, alpha], axis=1) * a_ref[...] + _dot(p.astype(BF16), va)
            m_ref[...] = m_next

    def step(kc, vc, ones):
        va = jnp.concatenate([vc, ones], axis=1)
        one_map(qa, kc, va, a1_ref, m1_ref)
        one_map(qb, kc, va, a2_ref, m2_ref)

    ones_lat = ones_col(tk)

    def lat_body(j, carry):
        off = pl.multiple_of(j * tk, tk)
        step(kl_ref[0, pl.ds(off, tk), :], vl_ref[0, pl.ds(off, tk), :], ones_lat)
        return carry

    lax.fori_loop(0, kl_ref.shape[1] // tk, lat_body, 0)
    step(kc_ref[0], vc_ref[0], ones_col(kc_ref.shape[1]))

    lam = lam_ref[...]
    a1, a2 = a1_ref[...], a2_ref[...]
    o = a1[:, :LANES] / a1[:, LANES:LANES + 1] - lam * (a2[:, :LANES] / a2[:, LANES:LANES + 1])
    ms = jnp.mean(o * o, axis=-1, keepdims=True)
    o = o * lax.rsqrt(ms + RMS_EPS) * sg_ref[...]
    o_ref[0] = (o * out_scale).astype(BF16)


def _attn_cfg(n_lat):
    return dict(tq=_pick_tile(n_lat, ATTN_TQ), tk=_pick_tile(n_lat, ATTN_TK))


def _attn_call(lam_row, q, k_lat, v_lat, k_ctx, v_ctx, sg, tq, tk, out_scale=1.0, bounded=True):
    b, n, _ = q.shape
    nc = k_ctx.shape[1]
    kv_lat = pl.BlockSpec((1, n, LANES), lambda bi, h, i: (bi, 0, h))
    kv_ctx = pl.BlockSpec((1, nc, LANES), lambda bi, h, i: (bi, 0, h))
    q_spec = pl.BlockSpec((1, tq, LANES), lambda bi, h, i: (bi, i, h))
    row = pl.BlockSpec((1, LANES), lambda bi, h, i: (0, 0))
    acc = pltpu.VMEM((tq, 2 * LANES), F32)
    run_max = pltpu.VMEM((tq, LANES), F32)
    return pl.pallas_call(
        functools.partial(_attn_kernel, tk=tk, out_scale=out_scale, bounded=bounded),
        out_shape=jax.ShapeDtypeStruct((b, n, V_WIDTH), BF16),
        grid=(b, HEADS, n // tq),
        in_specs=[row, q_spec, kv_lat, kv_lat, kv_ctx, kv_ctx, row],
        out_specs=q_spec,
        scratch_shapes=[acc, acc, run_max, run_max],
        compiler_params=_cparams(("parallel", "parallel", "arbitrary")),
        name="attn" if bounded else "attn_general",
    )(lam_row, q, k_lat, v_lat, k_ctx, v_ctx, sg)


def _s5_kernel(ul_ref, uc_ref, win_ref, m_ref, wo_ref, lam_ref, o_ref, xl_ref, xc_ref, s_ref, *, nb):
    ul = ul_ref[0]
    n_slab = xl_ref.shape[0]

    def to_slabs(x_ref, x):
        for k in range(n_slab):
            x_ref[k] = x[:, k * LANES:(k + 1) * LANES]

    to_slabs(xl_ref, _dot(ul, win_ref[0]))
    to_slabs(xc_ref, _dot(uc_ref[0], win_ref[0]))
    n_lat = ul.shape[0] // nb
    n_ctx = uc_ref.shape[1] // nb

    lam = lam_ref[0]
    lfr, lfi, lbr, lbi = (jnp.broadcast_to(lam[i:i + 1], (nb, LANES)) for i in range(4))

    def rows(c, n_chunks):
        return pl.ds(c, nb, stride=n_chunks)

    def advance(x_ref, c, n_chunks, slab, ar, ai, sr, si):
        xr = x_ref[slab, rows(c, n_chunks), :]
        xi = x_ref[slab + 1, rows(c, n_chunks), :]
        return ar * sr - ai * si + xr, ar * si + ai * sr + xi

    def ctx_body(i, carry):
        fr, fi, br, bi = carry
        fr, fi = advance(xc_ref, i, n_ctx, 0, lfr, lfi, fr, fi)
        br, bi = advance(xc_ref, n_ctx - 1 - i, n_ctx, 2, lbr, lbi, br, bi)
        return fr, fi, br, bi

    def lat_body(i, carry):
        fr, fi, br, bi = carry
        cb = n_lat - 1 - i
        s_ref[0, pl.ds(pl.multiple_of(i * nb, nb), nb), :] = fr
        s_ref[1, pl.ds(pl.multiple_of(i * nb, nb), nb), :] = fi
        s_ref[2, pl.ds(pl.multiple_of(cb * nb, nb), nb), :] = br
        s_ref[3, pl.ds(pl.multiple_of(cb * nb, nb), nb), :] = bi
        fr, fi = advance(xl_ref, i, n_lat, 0, lfr, lfi, fr, fi)
        br, bi = advance(xl_ref, cb, n_lat, 2, lbr, lbi, br, bi)
        return fr, fi, br, bi

    z = jnp.zeros((nb, LANES), F32)
    carry = lax.fori_loop(0, n_ctx, ctx_body, (z, z, z, z), unroll=SCAN_UNROLL)
    lax.fori_loop(0, n_lat, lat_body, carry, unroll=SCAN_UNROLL)

    def batch_major(k):
        return jnp.concatenate([s_ref[k, pl.ds(bi, n_lat, stride=nb), :] for bi in range(nb)], axis=0)

    s_in = jnp.concatenate([batch_major(k).astype(BF16) for k in range(n_slab)], axis=1)
    y = _dot(ul, m_ref[0]) + _dot(s_in, wo_ref[0])
    o_ref[0] = jax.nn.gelu(y).astype(BF16)


def _s5_call(u_lat, u_ctx, win, m, wo, lam16, b):
    npair, rows_lat, kw = u_lat.shape
    nb = S5_BATCH_BLOCK if b % S5_BATCH_BLOCK == 0 else b
    rl = rows_lat // b * nb
    rc = u_ctx.shape[1] // b * nb
    wspec = pl.BlockSpec((1, kw, kw), lambda p, h: (p, 0, 0))
    slabs = lambda r: pltpu.VMEM((kw // LANES, r, LANES), F32)
    return pl.pallas_call(
        functools.partial(_s5_kernel, nb=nb),
        out_shape=jax.ShapeDtypeStruct((npair, rows_lat, kw), BF16),
        grid=(npair, b // nb),
        in_specs=[pl.BlockSpec((1, rl, kw), lambda p, h: (p, h, 0)),
                  pl.BlockSpec((1, rc, kw), lambda p, h: (p, h, 0)),
                  wspec, wspec, wspec,
                  pl.BlockSpec((1, 4, LANES), lambda p, h: (p, 0, 0))],
        out_specs=pl.BlockSpec((1, rl, kw), lambda p, h: (p, h, 0)),
        scratch_shapes=[slabs(rl), slabs(rc), slabs(rl)],
        compiler_params=_cparams(("parallel", "parallel")),
        name="s5",
    )(u_lat, u_ctx, win, m, wo, lam16)


def _s5_weights(a_re, a_im, log_dt, b_re, b_im, c_re, c_im, d_skip):
    hp = lax.Precision.HIGHEST
    g_n, p_n = a_re.shape[1], a_re.shape[2]
    t_n = CHUNK
    dt = jnp.exp(log_dt.astype(F32))[..., None]
    ar, ai = a_re.astype(F32), a_im.astype(F32)
    mag = jnp.exp(ar * dt)
    lr, li = mag * jnp.cos(ai * dt), mag * jnp.sin(ai * dt)
    den = ar * ar + ai * ai
    nr, ni = lr - 1.0, li
    cr = (nr * ar + ni * ai) / den
    ci = (ni * ar - nr * ai) / den
    bbr = cr[..., None] * b_re - ci[..., None] * b_im
    bbi = cr[..., None] * b_im + ci[..., None] * b_re
    n = jnp.arange(t_n + 1, dtype=F32)[:, None, None, None]
    pm = jnp.exp(n * (ar * dt))
    pw_r, pw_i = pm * jnp.cos(n * (ai * dt)), pm * jnp.sin(n * (ai * dt))
    lb_r = pw_r[:t_n, ..., None] * bbr - pw_i[:t_n, ..., None] * bbi
    lb_i = pw_r[:t_n, ..., None] * bbi + pw_i[:t_n, ..., None] * bbr
    cre, cim = c_re.astype(F32), c_im.astype(F32)
    kern = (jnp.einsum('gip,tdgpj->tdgij', cre, lb_r, precision=hp)
            - jnp.einsum('gip,tdgpj->tdgij', cim, lb_i, precision=hp))
    hh = SSM_GROUP
    npair = g_n // 2
    eye2 = jnp.eye(2, dtype=F32)
    kw = 2 * t_n * hh

    lag0 = kern[0, 0] + kern[0, 1] + jnp.eye(hh, dtype=F32)[None] * d_skip.astype(F32)[:, :, None]
    by_lag = jnp.concatenate([kern[:0:-1, 1], lag0[None], kern[1:, 0]], axis=0)
    strip = jnp.einsum('ab,pajlq->pajlbq', eye2,
                       jnp.transpose(by_lag, (1, 3, 0, 2)).reshape(npair, 2, hh, 2 * t_n - 1, hh))
    strip = strip.reshape(npair, 2 * hh, (2 * t_n - 1) * 2 * hh)
    m_pair = jnp.stack([strip[:, :, (t_n - 1 - s) * 2 * hh:(t_n - 1 - s) * 2 * hh + kw] for s in range(t_n)],
                       axis=1).reshape(npair, kw, kw)

    def state_lanes(parts, mask_pair):
        x = jnp.stack(parts, axis=2).reshape(npair, 2, parts[0].shape[1], 4, 1, p_n)
        sel = eye2[None, :, None, None, :, None] if mask_pair else jnp.ones((1, 1, 1, 1, 2, 1), F32)
        return (x * sel).reshape(npair, 2, parts[0].shape[1], 8 * p_n)

    gsp = lambda z: jnp.transpose(z, (1, 0, 2))
    pf_r, pf_i, pb_r, pb_i = pw_r[:, 0], pw_i[:, 0], pw_r[:, 1], pw_i[:, 1]
    pa = state_lanes([gsp(pf_r[t_n - 1::-1]), gsp(pf_r[t_n - 1::-1]), gsp(pb_r[:t_n]), gsp(pb_r[:t_n])], True)
    pb = state_lanes([-gsp(pf_i[t_n - 1::-1]), gsp(pf_i[t_n - 1::-1]), -gsp(pb_i[:t_n]), gsp(pb_i[:t_n])], True)
    bt = lambda z: jnp.transpose(z, (0, 2, 1))
    ba = state_lanes([bt(bbr[0]), bt(bbi[0]), bt(bbr[1]), bt(bbi[1])], False)
    bb = state_lanes([bt(bbi[0]), bt(bbr[0]), bt(bbi[1]), bt(bbr[1])], False)
    sa = lambda z: jnp.transpose(z, (0, 2, 1, 3))
    win_pair = (sa(pa)[:, :, :, None, :] * ba[:, None] + sa(pb)[:, :, :, None, :] * bb[:, None]
                ).reshape(npair, kw, 8 * p_n)
    qa = state_lanes([gsp(pf_r[1:]), -gsp(pf_i[1:]), gsp(pb_r[t_n:0:-1]), -gsp(pb_i[t_n:0:-1])], True)
    qb = state_lanes([-gsp(pf_i[1:]), -gsp(pf_r[1:]), -gsp(pb_i[t_n:0:-1]), -gsp(pb_r[t_n:0:-1])], True)
    ca = state_lanes([cre] * 4, False)
    cb = state_lanes([cim] * 4, False)
    wo_t = (sa(qa)[:, :, :, None, :] * ca[:, None] + sa(qb)[:, :, :, None, :] * cb[:, None]
            ).reshape(npair, kw, 8 * p_n)
    wo_pair = jnp.swapaxes(wo_t.astype(BF16), 1, 2)
    lam16 = jnp.stack([pw_r[t_n, 0], pw_i[t_n, 0], pw_r[t_n, 1], pw_i[t_n, 1]], axis=1)
    lam16 = jnp.transpose(lam16.reshape(npair, 2, 4, p_n), (0, 2, 1, 3)).reshape(npair, 4, 2 * p_n)
    return win_pair.astype(BF16), m_pair.astype(BF16), wo_pair, lam16


def _outproj_kernel(a_ref, y_ref, x_ref, g1_ref, sh_ref, sc_ref, n2_ref, wglu_ref, bglu_ref, wout_ref,
                    wr_hi_ref, wr_lo_ref, br_ref, x1_ref, h2_ref, gate_ref, ys_ref):
    _chunk_rows_to_token_major(
        lambda pr, k: y_ref[pr, 0, :, k * LANES:(k + 1) * LANES].astype(F32), ys_ref)
    yf = jnp.concatenate([ys_ref[q] for q in range(ys_ref.shape[0])], axis=1)
    y = yf.astype(BF16)
    z = _dot(y, wglu_ref[...]) + bglu_ref[...]
    s = (yf * jax.nn.sigmoid(z)).astype(BF16)
    half = a_ref.shape[2]
    o = _dot(a_ref[0], wout_ref[:half, :]) + _dot(s, wout_ref[half:, :])
    x1 = x_ref[0] + g1_ref[0] * o
    x1_ref[0] = x1

    ms = jnp.mean(x1 * x1, axis=-1, keepdims=True)
    h = (x1 * lax.rsqrt(ms + RMS_EPS)) * n2_ref[...]
    h = h * (1.0 + sc_ref[0]) + sh_ref[0]
    for j in range(h.shape[1] // LANES):
        h2_ref[pl.ds(j, h.shape[0], stride=ROW_TILE), :] = h[:, j * LANES:(j + 1) * LANES]

    h_hi, h_lo = _split_bf16(h)
    lg = _dot(h_hi, wr_hi_ref[...]) + (_dot(h_hi, wr_lo_ref[...]) + _dot(h_lo, wr_hi_ref[...])) + br_ref[...]
    lane = lax.broadcasted_iota(jnp.int32, lg.shape, 1)
    neg = jnp.float32(-jnp.inf)
    big = jnp.int32(LANES)

    def top1(vals):
        vmax = jnp.max(vals, axis=1, keepdims=True)
        idx = jnp.min(jnp.where(vals == vmax, lane, big), axis=1, keepdims=True)
        return vmax, idx

    is_grp = lane < MOE_GROUPS
    g_vals = jnp.where(is_grp, lg, neg)
    g_max, g_idx = top1(g_vals)
    p_grp = 1.0 / jnp.sum(jnp.where(is_grp, jnp.exp(g_vals - g_max), 0.0), axis=1, keepdims=True)
    e_lo = MOE_GROUPS + EXPERTS_PER_GROUP * g_idx
    in_grp = (lane >= e_lo) & (lane < e_lo + EXPERTS_PER_GROUP)
    e_vals = jnp.where(in_grp, lg, neg)
    v1, i1 = top1(e_vals)
    v2, i2 = top1(jnp.where(lane == i1, neg, e_vals))
    r = jnp.exp(v2 - v1)
    w1 = p_grp / (1.0 + r)
    w2 = w1 * r
    first_lo = i1 < i2
    a_loc = jnp.where(first_lo, i1, i2) - e_lo
    b_loc = jnp.where(first_lo, i2, i1) - e_lo
    pair = ((a_loc * (2 * EXPERTS_PER_GROUP - 1 - a_loc)) >> 1) + (b_loc - a_loc - 1)
    cls = g_idx * PAIRS_PER_GROUP + pair
    w_lo = jnp.where(first_lo, w1, w2)
    w_hi = jnp.where(first_lo, w2, w1)
    gate_ref[0] = (jnp.where(lane == 0, w_lo, 0.0) + jnp.where(lane == 1, w_hi, 0.0)
                   + jnp.where(lane == 2, cls.astype(F32), 0.0))


def _outproj_call(a, yg, x, g1, sh2, sc2, n2g, wglu, bglu, wout, wr_hi, wr_lo, br, tm):
    b, n, d = x.shape
    half = a.shape[2]
    tok = lambda w: pl.BlockSpec((1, tm, w), lambda bi, i: (bi, i, 0))
    mod = pl.BlockSpec((1, 1, d), lambda bi, i: (bi, 0, 0))
    const = lambda r, c: pl.BlockSpec((r, c), lambda bi, i: (0, 0))
    return pl.pallas_call(
        _outproj_kernel,
        out_shape=[jax.ShapeDtypeStruct((b, n, d), F32), jax.ShapeDtypeStruct((b * n * ROW_TILE, LANES), F32),
                   jax.ShapeDtypeStruct((b, n, LANES), F32)],
        grid=(b, n // tm),
        in_specs=[tok(half),
                  pl.BlockSpec((yg.shape[0], 1, tm // CHUNK, yg.shape[3]), lambda bi, i: (0, bi, i, 0)),
                  tok(d), mod, mod, mod, const(1, d),
                  const(half, half), const(1, half), const(d, d),
                  const(d, LANES), const(d, LANES), const(1, LANES)],
        out_specs=[tok(d), pl.BlockSpec((tm * ROW_TILE, LANES), lambda bi, i: (bi * (n // tm) + i, 0)), tok(LANES)],
        scratch_shapes=[pltpu.VMEM((half // LANES, tm, LANES), F32)],
        compiler_params=_cparams(("parallel", "parallel")),
        name="outproj",
    )(a, yg, x, g1, sh2, sc2, n2g, wglu, bglu, wout, wr_hi, wr_lo, br)


def _token_rows(tok, rows_per_token):
    return pl.ds(pl.multiple_of(tok * rows_per_token, rows_per_token), rows_per_token)


def _moe_expert_kernel(ea_ref, eb_ref, nv_ref, nt_ref, idx_hbm, h_hbm, wga_ref, wua_ref, wda_ref,
                       wgb_ref, wub_ref, wdb_ref, y_hbm, idx_smem, hbuf, ybuf, gsem, isem, psem, *, tm):
    i = pl.program_id(0)
    n_steps = pl.num_programs(0)
    n_tiles = nt_ref[0]
    hrows, yrows = tm * ROW_TILE, tm * 2 * ROW_TILE

    def idx_copy(tile):
        s = tile % IDX_SLOTS
        return pltpu.make_async_copy(idx_hbm.at[pl.ds(tile, 1), :], idx_smem.at[pl.ds(s, 1), :], isem.at[s])

    def pull_row(tile, r):
        tok = idx_smem[tile % IDX_SLOTS, r]
        s2 = tile % 3
        return pltpu.make_async_copy(h_hbm.at[_token_rows(tok, ROW_TILE), :],
                                     hbuf.at[pl.ds(s2 * hrows + r * ROW_TILE, ROW_TILE), :], gsem.at[s2])

    def pull_tile(tile):
        def body(r, carry):
            pull_row(tile, r).start()
            return carry
        lax.fori_loop(0, tm, body, 0, unroll=PUSH_UNROLL)

    def push_row(tile, r):
        tok = idx_smem[tile % IDX_SLOTS, r]
        s3 = tile % 3
        return pltpu.make_async_copy(ybuf.at[pl.ds(s3 * yrows + r * 2 * ROW_TILE, 2 * ROW_TILE), :],
                                     y_hbm.at[_token_rows(tok, 2 * ROW_TILE), :], psem.at[s3])

    def wait_pulls(tile):
        s2 = tile % 3
        pltpu.make_async_copy(h_hbm.at[pl.ds(0, hrows), :], hbuf.at[pl.ds(s2 * hrows, hrows), :], gsem.at[s2]).wait()

    def wait_pushes(tile):
        s3 = tile % 3
        n = nv_ref[tile] * (2 * ROW_TILE)

        @pl.when(n > 0)
        def _():
            pltpu.make_async_copy(ybuf.at[pl.ds(s3 * yrows, n), :], y_hbm.at[pl.ds(0, n), :], psem.at[s3]).wait()

    @pl.when(i == 0)
    def _():
        idx_copy(0).start()
        idx_copy(1).start()
        idx_copy(0).wait()
        idx_copy(1).wait()
        pull_tile(0)

        @pl.when(1 < n_tiles)
        def _():
            pull_tile(1)
        idx_copy(2).start()

    @pl.when(i + 3 < n_steps)
    def _():
        idx_copy(i + 3).start()

    @pl.when(i + 2 < n_steps)
    def _():
        idx_copy(i + 2).wait()

    @pl.when(i < n_tiles)
    def _():
        wait_pulls(i)

        @pl.when(i >= 3)
        def _():
            wait_pushes(i - 3)

        s3 = i % 3
        h = jnp.concatenate([hbuf[pl.ds(s3 * hrows + j, tm, stride=ROW_TILE), :] for j in range(ROW_TILE)],
                            axis=1).astype(BF16)
        pull_next = i + 2 < n_tiles
        n_push = jnp.where(i >= 1, nv_ref[jnp.maximum(i - 1, 0)], 0)

        def issue_rows(r0, r1):
            for r in range(r0, r1):
                @pl.when(pull_next)
                def _():
                    pull_row(i + 2, r).start()

                @pl.when(r < n_push)
                def _():
                    push_row(i - 1, r).start()

        def expert(wg_ref, wu_ref, wd_ref, row0):
            hid = jax.nn.silu(_dot(h, wg_ref[0])) * _dot(h, wu_ref[0])
            y = _dot(hid.astype(BF16), wd_ref[0])
            for j in range(ROW_TILE):
                ybuf[pl.ds(s3 * yrows + row0 + j, tm, stride=2 * ROW_TILE), :] = y[:, j * LANES:(j + 1) * LANES]

        issue_rows(0, tm // 2)
        expert(wga_ref, wua_ref, wda_ref, 0)
        issue_rows(tm // 2, tm)
        expert(wgb_ref, wub_ref, wdb_ref, ROW_TILE)

    @pl.when(i == n_tiles)
    def _():
        def body(r, carry):
            push_row(i - 1, r).start()
            return carry
        lax.fori_loop(0, nv_ref[i - 1], body, 0)

        for back in (3, 2, 1):
            @pl.when(i >= back)
            def _():
                wait_pushes(i - back)


def _moe_expert_call(ea, eb, nv, nt, idx, h_tiles, wg, wu, wd, n_tokens):
    n_steps, tm = idx.shape
    ne, d, f = wg.shape
    amap = lambda i, ea, eb, nv, nt: (ea[i], 0, 0)
    bmap = lambda i, ea, eb, nv, nt: (eb[i], 0, 0)
    up = lambda m: pl.BlockSpec((1, d, f), m)
    down = lambda m: pl.BlockSpec((1, f, d), m)
    hbm = pl.BlockSpec(memory_space=pl.ANY)
    grid_spec = pltpu.PrefetchScalarGridSpec(
        num_scalar_prefetch=4,
        grid=(n_steps,),
        in_specs=[hbm, hbm, up(amap), up(amap), down(amap), up(bmap), up(bmap), down(bmap)],
        out_specs=hbm,
        scratch_shapes=[pltpu.SMEM((IDX_SLOTS, tm), jnp.int32),
                        pltpu.VMEM((3 * tm * ROW_TILE, LANES), F32),
                        pltpu.VMEM((3 * tm * 2 * ROW_TILE, LANES), F32),
                        pltpu.SemaphoreType.DMA((3,)), pltpu.SemaphoreType.DMA((IDX_SLOTS,)),
                        pltpu.SemaphoreType.DMA((3,))])
    return pl.pallas_call(
        functools.partial(_moe_expert_kernel, tm=tm),
        out_shape=jax.ShapeDtypeStruct((n_tokens * 2 * ROW_TILE, LANES), F32),
        grid_spec=grid_spec,
        compiler_params=_cparams(("arbitrary",)),
        name="moe_experts",
    )(ea, eb, nv, nt, idx, h_tiles, wg, wu, wd, wg, wu, wd)


def _moe_combine_kernel(y_ref, route_ref, x1_ref, g2_ref, o_ref):
    tm = x1_ref.shape[0]
    ya = jnp.concatenate([y_ref[pl.ds(j, tm, stride=2 * ROW_TILE), :] for j in range(ROW_TILE)], axis=1)
    yb = jnp.concatenate([y_ref[pl.ds(ROW_TILE + j, tm, stride=2 * ROW_TILE), :] for j in range(ROW_TILE)], axis=1)
    route = route_ref[...]
    moe = route[:, 0:1] * ya + route[:, 1:2] * yb
    o_ref[...] = x1_ref[...] + g2_ref[0] * moe


def _moe_combine_call(y_tiles, route, x1, g2, tokens_per_batch, tm):
    t, d = x1.shape
    per_b = tokens_per_batch // tm
    tok = lambda w: pl.BlockSpec((tm, w), lambda i: (i, 0))
    return pl.pallas_call(
        _moe_combine_kernel,
        out_shape=jax.ShapeDtypeStruct((t, d), F32),
        grid=(t // tm,),
        in_specs=[pl.BlockSpec((tm * 2 * ROW_TILE, LANES), lambda i: (i, 0)), tok(LANES), tok(d),
                  pl.BlockSpec((1, 1, d), lambda i: (i // per_b, 0, 0))],
        out_specs=tok(d),
        compiler_params=_cparams(("parallel",)),
        name="moe_combine",
    )(y_tiles, route, x1, g2)


def _routing_plan(cls, tm):
    t = cls.shape[0]
    n_steps = t // tm + N_CLASSES + 1
    order = jnp.argsort(cls).astype(jnp.int32)
    classes = jnp.arange(N_CLASSES, dtype=jnp.int32)
    counts = jnp.sum((cls[:, None] == classes[None, :]).astype(jnp.int32), axis=0)
    cstart = jnp.cumsum(counts) - counts
    tiles_c = (counts + tm - 1) // tm
    tile_end = jnp.cumsum(tiles_c)
    n_tiles = tile_end[-1]
    tile_ids = jnp.arange(n_steps, dtype=jnp.int32)
    live = tile_ids < n_tiles
    c_of = jnp.sum((tile_end[None, :] <= jnp.minimum(tile_ids, n_tiles - 1)[:, None]).astype(jnp.int32), axis=1)
    k_of = jnp.minimum(tile_ids, n_tiles - 1) - (tile_end - tiles_c)[c_of]
    nv = jnp.where(live, jnp.clip(counts[c_of] - k_of * tm, 0, tm), 0).astype(jnp.int32)
    base = cstart[c_of] + k_of * tm
    idx = order[jnp.minimum(base[:, None] + jnp.arange(tm, dtype=jnp.int32)[None, :], t - 1)]
    grp, pair = c_of // PAIRS_PER_GROUP, c_of % PAIRS_PER_GROUP
    a_tab = jnp.array([a for a in range(EXPERTS_PER_GROUP) for _ in range(a + 1, EXPERTS_PER_GROUP)], jnp.int32)
    b_tab = jnp.array([b for a in range(EXPERTS_PER_GROUP) for b in range(a + 1, EXPERTS_PER_GROUP)], jnp.int32)
    ea = grp * EXPERTS_PER_GROUP + a_tab[pair]
    eb = grp * EXPERTS_PER_GROUP + b_tab[pair]
    return ea.astype(jnp.int32), eb.astype(jnp.int32), nv, n_tiles.reshape(1).astype(jnp.int32), idx


def _rope_tables(n_tokens):
    rows = n_tokens // GRID_W
    row = jnp.broadcast_to(jnp.arange(rows, dtype=F32)[:, None], (rows, GRID_W)).reshape(-1)
    col = jnp.broadcast_to(jnp.arange(GRID_W, dtype=F32)[None, :], (rows, GRID_W)).reshape(-1)
    half = HEAD_DIM // 2
    inv = ROPE_BASE ** (-jnp.arange(0, half, 2, dtype=F32) / half)
    ang = jnp.stack([row[:, None] * inv, col[:, None] * inv], axis=1)
    cos, sin = jnp.cos(ang), jnp.sin(ang)
    cos64 = jnp.concatenate([cos[:, 0], cos[:, 0], cos[:, 1], cos[:, 1]], axis=1)
    sin64 = jnp.concatenate([-sin[:, 0], sin[:, 0], -sin[:, 1], sin[:, 1]], axis=1)
    return jnp.tile(cos64, (1, LANES // HEAD_DIM)), jnp.tile(sin64, (1, LANES // HEAD_DIM))


def _pick_tile(n, target):
    t = min(n, target)
    while n % t:
        t //= 2
    return t


def kernel(x, c, ctx, c_ctx, w_ada, b_ada, norm1_g, w_in, q_norm_g, k_norm_g, lambda_q1, lambda_k1, lambda_q2, lambda_k2, subln_g, ssm_a_re, ssm_a_im, ssm_log_dt, ssm_b_re, ssm_b_im, ssm_c_re, ssm_c_im, ssm_d, w_glu, b_glu, w_out, norm2_g, w_route_group, b_route_group, w_route_expert, b_route_expert, w_exp_gate, w_exp_up, w_exp_down):
    depth = w_ada.shape[0]
    assert depth == 1, "single-layer block: the context stream is never updated"
    b, n_lat, d = x.shape
    n_ctx = ctx.shape[1]
    assert n_lat % CHUNK == 0 and n_ctx % CHUNK == 0 and n_lat % GRID_W == 0
    assert d == ROW_TILE * LANES, "MoE rows are moved as one (8, 128) tile per token"
    l = 0
    lam_init = 0.8 - 0.6 * math.exp(-0.3 * l)

    rows = b + 1
    rows_pad = -(-rows // 8) * 8
    cc = jnp.concatenate([c, c_ctx[None, :], jnp.zeros((rows_pad - rows, d), F32)], axis=0)
    mod = _mod_call(cc, w_ada[l], b_ada[l])
    sh1, sc1, g1, sh2, sc2, g2 = (mod[:b, i * d:(i + 1) * d].reshape(b, 1, d) for i in range(6))
    csh1, csc1 = (mod[b:b + 1, i * d:(i + 1) * d].reshape(1, 1, d) for i in range(2))

    w_in_bf = w_in[l].astype(BF16)
    bd = jnp.kron(jnp.eye(QK_WIDTH // HEAD_DIM, dtype=F32), jnp.ones((HEAD_DIM, HEAD_DIM), F32)).astype(BF16)
    qg = jnp.tile(q_norm_g[l], LANES // HEAD_DIM).reshape(1, LANES)
    kg = jnp.tile(k_norm_g[l], LANES // HEAD_DIM).reshape(1, LANES)
    cosf, sinf = _rope_tables(n_lat)
    ones_c, zeros_c = jnp.ones((n_ctx, LANES), F32), jnp.zeros((n_ctx, LANES), F32)
    g1n = norm1_g[l].reshape(1, d)
    tm = _pick_tile(n_lat, 512)
    q_x, k_x, v_x, u_x = _inproj_call(x, sh1, sc1, g1n, w_in_bf, qg, kg, cosf, sinf, bd, tm, "inproj_lat")
    _, k_c, v_c, u_c = _inproj_call(ctx, csh1, csc1, g1n, w_in_bf, qg, kg, ones_c, zeros_c, bd,
                                    _pick_tile(n_ctx, 512), "inproj_ctx")

    e1 = jnp.exp(jnp.sum(lambda_q1[l] * lambda_k1[l]))
    e2 = jnp.exp(jnp.sum(lambda_q2[l] * lambda_k2[l]))
    lam_row = jnp.full((1, LANES), e1 - e2 + lam_init, F32)
    score_bound = math.sqrt(HEAD_DIM) * jnp.max(jnp.abs(q_norm_g[l])) * jnp.max(jnp.abs(k_norm_g[l]))

    def attn(bounded):
        return lambda *ops: _attn_call(*ops, **_attn_cfg(n_lat), out_scale=1.0 - lam_init, bounded=bounded)

    a_x = lax.cond(score_bound <= SCORE_BOUND, attn(True), attn(False),
                   lam_row, q_x, k_x, v_x, k_c, v_c, subln_g[l].reshape(1, LANES))

    win, m_op, wo, lam16 = _s5_weights(ssm_a_re[l], ssm_a_im[l], ssm_log_dt[l], ssm_b_re[l], ssm_b_im[l],
                                       ssm_c_re[l], ssm_c_im[l], ssm_d[l])
    n_pairs, kw = u_x.shape[0], u_x.shape[3]
    yg = _s5_call(u_x.reshape(n_pairs, b * (n_lat // CHUNK), kw), u_c.reshape(n_pairs, b * (n_ctx // CHUNK), kw),
                  win, m_op, wo, lam16, b)
    yg = yg.reshape(n_pairs, b, n_lat // CHUNK, kw)

    wr = jnp.concatenate([w_route_group[l], w_route_expert[l]], axis=1)
    wr = jnp.pad(wr, ((0, 0), (0, LANES - wr.shape[1])))
    wr_hi, wr_lo = _split_bf16(wr)
    br = jnp.pad(jnp.concatenate([b_route_group[l], b_route_expert[l]]), (0, LANES - MOE_GROUPS - N_EXPERTS))
    x1, h2, route = _outproj_call(a_x, yg, x, g1, sh2, sc2, norm2_g[l].reshape(1, d),
                                 w_glu[l].astype(BF16), b_glu[l].reshape(1, -1), w_out[l].astype(BF16),
                                 wr_hi, wr_lo, br.reshape(1, LANES), tm)

    t_all = b * n_lat
    route = route.reshape(t_all, LANES)
    ea, eb, nv, n_tiles, idx = _routing_plan(route[:, 2].astype(jnp.int32), _pick_tile(t_all, MOE_TILE))
    y_tiles = _moe_expert_call(ea, eb, nv, n_tiles, idx, h2, w_exp_gate[l].astype(BF16),
                               w_exp_up[l].astype(BF16), w_exp_down[l].astype(BF16), t_all)
    out = _moe_combine_call(y_tiles, route, x1.reshape(t_all, d), g2, n_lat, _pick_tile(n_lat, COMBINE_TILE))
    return out.reshape(b, n_lat, d)
```

```python
import functools
import math

import jax
import jax.numpy as jnp
from jax import lax
from jax.experimental import pallas as pl
from jax.experimental.pallas import tpu as pltpu

F32 = jnp.float32
BF16 = jnp.bfloat16

LANES = 128
MXU_TILE = 256
HEADS = 4
HEAD_DIM = 64
QK_WIDTH = HEADS * 2 * HEAD_DIM
V_WIDTH = HEADS * 2 * HEAD_DIM
GRID_W = 64
ROPE_BASE = 10000.0
SSM_GROUP = 16
SSM_STATE = 64
CHUNK = 16
MOE_GROUPS = 4
EXPERTS_PER_GROUP = 8
N_EXPERTS = MOE_GROUPS * EXPERTS_PER_GROUP
RMS_EPS = 1e-6
INPROJ_TILE = 512
OUTPROJ_TILE = 1024
ATTN_TQ = 512
ATTN_TK = 4096
SCORE_BOUND = 60.0
SCAN_UNROLL = 8
S5_BATCH_BLOCK = 8
PAIRS_PER_GROUP = EXPERTS_PER_GROUP * (EXPERTS_PER_GROUP - 1) // 2
N_CLASSES = MOE_GROUPS * PAIRS_PER_GROUP
ROW_TILE = 8
MOE_TILE = 256
PUSH_UNROLL = 8
IDX_SLOTS = 5
COMBINE_TILE = 512
VMEM_LIMIT = 48 * 1024 * 1024


def _cparams(sem):
    return pltpu.CompilerParams(dimension_semantics=sem, vmem_limit_bytes=VMEM_LIMIT)


def _split_bf16(a):
    hi = a.astype(BF16)
    lo = (a - hi.astype(F32)).astype(BF16)
    return hi, lo


def _dot(a, b):
    return jnp.dot(a, b, preferred_element_type=F32)


def _dot3(a, b):
    a_hi, a_lo = _split_bf16(a)
    b_hi, b_lo = _split_bf16(b)
    return _dot(a_hi, b_hi) + (_dot(a_hi, b_lo) + _dot(a_lo, b_hi))


def _mod_kernel(c_ref, w_ref, b_ref, o_ref):
    c = c_ref[...]
    a = c * jax.nn.sigmoid(c)
    o_ref[...] = _dot3(a, w_ref[...]) + b_ref[...]


def _mod_call(cc, w_ada, b_ada):
    rows, d = cc.shape
    n = w_ada.shape[1]
    bn = 1024
    return pl.pallas_call(
        _mod_kernel,
        out_shape=jax.ShapeDtypeStruct((rows, n), F32),
        grid=(n // bn,),
        in_specs=[pl.BlockSpec((rows, d), lambda j: (0, 0)),
                  pl.BlockSpec((d, bn), lambda j: (0, j)),
                  pl.BlockSpec((1, bn), lambda j: (0, j))],
        out_specs=pl.BlockSpec((rows, bn), lambda j: (0, j)),
        compiler_params=_cparams(("arbitrary",)),
        name="mod",
    )(cc, w_ada, b_ada.reshape(1, n))


def _inproj_kernel(x_ref, sh_ref, sc_ref, g_ref, w_ref, qg_ref, kg_ref, cos_ref, sin_ref, bd_ref, perm_ref,
                   q_ref, k_ref, v_ref, u_ref):
    x = x_ref[0]
    ms = jnp.mean(x * x, axis=-1, keepdims=True)
    h = (x * lax.rsqrt(ms + RMS_EPS)) * g_ref[...]
    h = h * (1.0 + sc_ref[0]) + sh_ref[0]
    p = _dot(h.astype(BF16), w_ref[...])

    cosf = cos_ref[...]
    sinf = sin_ref[...]
    lane = lax.broadcasted_iota(jnp.int32, cosf.shape, 1)
    first_half = (lane % 32) < 16

    def norm_rope(t, gain, scale):
        sq = (t * t).astype(BF16)
        half = bd_ref.shape[0]
        ss = jnp.concatenate([_dot(sq[:, c:c + half], bd_ref[...]) for c in range(0, QK_WIDTH, half)], axis=1)
        t = t * lax.rsqrt(ss * (1.0 / HEAD_DIM) + RMS_EPS)
        outs = []
        for s in range(QK_WIDTH // LANES):
            ts = t[:, s * LANES:(s + 1) * LANES] * gain
            partner = jnp.where(first_half, pltpu.roll(ts, LANES - 16, 1), pltpu.roll(ts, 16, 1))
            outs.append(((ts * cosf + partner * sinf) * scale).astype(BF16))
        return jnp.concatenate(outs, axis=1)

    q_ref[0] = norm_rope(p[:, :QK_WIDTH], qg_ref[...], HEAD_DIM ** -0.5 * math.log2(math.e))
    k_ref[0] = norm_rope(p[:, QK_WIDTH:2 * QK_WIDTH], kg_ref[...], 1.0)
    v_ref[0] = p[:, 2 * QK_WIDTH:2 * QK_WIDTH + V_WIDTH].astype(BF16)
    u_t = _dot(perm_ref[...], p[:, 2 * QK_WIDTH + V_WIDTH:].astype(BF16))

    def store_u(pr, k, val):
        u_ref[pr, 0, :, k * LANES:(k + 1) * LANES] = val.astype(BF16)

    _step_major_to_chunk_rows(u_t, store_u)


PAIR_W = 2 * SSM_GROUP
PAIRS_PER_TILE = LANES // PAIR_W


def _quarter_select(pieces):
    lane = lax.broadcasted_iota(jnp.int32, pieces[0].shape, 1)
    acc = pieces[0]
    for r in range(1, len(pieces)):
        acc = jnp.where(lane // PAIR_W == r, pieces[r], acc)
    return acc


def _step_perm(tm):
    n_chunk = tm // CHUNK
    src = (jnp.arange(tm) % n_chunk) * CHUNK + jnp.arange(tm) // n_chunk
    return (src[:, None] == jnp.arange(tm)[None, :]).astype(BF16)


def _step_major_to_chunk_rows(u_t, store):
    n_chunk = u_t.shape[0] // CHUNK
    for k in range(CHUNK // PAIRS_PER_TILE):
        for pr in range(u_t.shape[1] // PAIR_W):
            q, r_src = divmod(pr, PAIRS_PER_TILE)
            pieces = []
            for r in range(PAIRS_PER_TILE):
                t = PAIRS_PER_TILE * k + r
                src = u_t[t * n_chunk:(t + 1) * n_chunk, q * LANES:(q + 1) * LANES]
                shift = ((r - r_src) % PAIRS_PER_TILE) * PAIR_W
                pieces.append(pltpu.roll(src, shift, 1) if shift else src)
            store(pr, k, _quarter_select(pieces))


def _chunk_rows_to_token_major(load, ys_ref):
    n_tiles, n_chunk = ys_ref.shape[0], ys_ref.shape[1] // CHUNK
    for t in range(CHUNK):
        k, r_src = divmod(t, PAIRS_PER_TILE)
        for q in range(n_tiles):
            pieces = []
            for r in range(PAIRS_PER_TILE):
                shift = ((r - r_src) % PAIRS_PER_TILE) * PAIR_W
                src = load(PAIRS_PER_TILE * q + r, k)
                pieces.append(pltpu.roll(src, shift, 1) if shift else src)
            ys_ref[q, pl.ds(t, n_chunk, stride=CHUNK), :] = _quarter_select(pieces)


def _inproj_call(x, sh, sc, g, w_bf, qg, kg, cosf, sinf, bd, tm, name):
    b, n, d = x.shape
    wn = w_bf.shape[1]
    per_batch = sh.shape[0] > 1
    mod_map = (lambda bi, i: (bi, 0, 0)) if per_batch else (lambda bi, i: (0, 0, 0))
    const2 = lambda bi, i: (0, 0)
    n_pairs = QK_WIDTH // PAIR_W
    outs = [jax.ShapeDtypeStruct((b, n, QK_WIDTH), BF16)] * 3 + [
        jax.ShapeDtypeStruct((n_pairs, b, n // CHUNK, CHUNK * PAIR_W), BF16)]
    tok_spec = pl.BlockSpec((1, tm, QK_WIDTH), lambda bi, i: (bi, i, 0))
    u_spec = pl.BlockSpec((n_pairs, 1, tm // CHUNK, CHUNK * PAIR_W), lambda bi, i: (0, bi, i, 0))
    return pl.pallas_call(
        _inproj_kernel,
        out_shape=outs,
        grid=(b, n // tm),
        in_specs=[pl.BlockSpec((1, tm, d), lambda bi, i: (bi, i, 0)),
                  pl.BlockSpec((1, 1, d), mod_map),
                  pl.BlockSpec((1, 1, d), mod_map),
                  pl.BlockSpec((1, d), const2),
                  pl.BlockSpec((d, wn), const2),
                  pl.BlockSpec((1, LANES), const2),
                  pl.BlockSpec((1, LANES), const2),
                  pl.BlockSpec((tm, LANES), lambda bi, i: (i, 0)),
                  pl.BlockSpec((tm, LANES), lambda bi, i: (i, 0)),
                  pl.BlockSpec(bd.shape, const2),
                  pl.BlockSpec((tm, tm), const2)],
        out_specs=[tok_spec] * 3 + [u_spec],
        compiler_params=_cparams(("parallel", "parallel")),
        name=name,
    )(x, sh, sc, g, w_bf, qg, kg, cosf, sinf, bd, _step_perm(tm))


def _attn_kernel(lam_ref, q_ref, kl_ref, vl_ref, kc_ref, vc_ref, sg_ref, o_ref, a1_ref, a2_ref, m1_ref, m2_ref,
                 *, tk, out_scale, bounded):
    q = q_ref[0]
    lane = lax.broadcasted_iota(jnp.int32, q.shape, 1)
    zero = jnp.zeros_like(q)
    qa = jnp.where(lane < HEAD_DIM, q, zero)
    qb = jnp.where(lane >= HEAD_DIM, q, zero)

    a1_ref[...] = jnp.zeros(a1_ref.shape, F32)
    a2_ref[...] = jnp.zeros(a2_ref.shape, F32)
    if not bounded:
        m1_ref[...] = jnp.full(m1_ref.shape, -jnp.inf, F32)
        m2_ref[...] = jnp.full(m2_ref.shape, -jnp.inf, F32)

    def ones_col(rows):
        col = lax.broadcasted_iota(jnp.int32, (rows, LANES), 1)
        return jnp.where(col == 0, 1.0, 0.0).astype(BF16)

    def one_map(qm, kc, va, a_ref, m_ref):
        s = lax.dot_general(qm, kc, (((1,), (1,)), ((), ())), preferred_element_type=F32)
        if bounded:
            a_ref[...] += _dot(jnp.exp2(s).astype(BF16), va)
        else:
            m_prev = m_ref[...]
            m_next = jnp.maximum(m_prev, jnp.max(s, axis=1, keepdims=True))
            p = jnp.exp2(s - m_next[:, :1])
            alpha = jnp.exp2(m_prev - m_next)
            a_ref[...] = jnp.concatenate([alpha, alpha], axis=1) * a_ref[...] + _dot(p.astype(BF16), va)
            m_ref[...] = m_next

    def step(kc, vc, ones):
        va = jnp.concatenate([vc, ones], axis=1)
        one_map(qa, kc, va, a1_ref, m1_ref)
        one_map(qb, kc, va, a2_ref, m2_ref)

    ones_lat = ones_col(tk)

    def lat_body(j, carry):
        off = pl.multiple_of(j * tk, tk)
        step(kl_ref[0, pl.ds(off, tk), :], vl_ref[0, pl.ds(off, tk), :], ones_lat)
        return carry

    lax.fori_loop(0, kl_ref.shape[1] // tk, lat_body, 0)
    step(kc_ref[0], vc_ref[0], ones_col(kc_ref.shape[1]))

    lam = lam_ref[...]
    a1, a2 = a1_ref[...], a2_ref[...]
    o = a1[:, :LANES] / a1[:, LANES:LANES + 1] - lam * (a2[:, :LANES] / a2[:, LANES:LANES + 1])
    ms = jnp.mean(o * o, axis=-1, keepdims=True)
    o = o * lax.rsqrt(ms + RMS_EPS) * sg_ref[...]
    o_ref[0] = (o * out_scale).astype(BF16)


def _attn_cfg(n_lat):
    return dict(tq=_pick_tile(n_lat, ATTN_TQ), tk=_pick_tile(n_lat, ATTN_TK))


def _attn_call(lam_row, q, k_lat, v_lat, k_ctx, v_ctx, sg, tq, tk, out_scale=1.0, bounded=True):
    b, n, _ = q.shape
    nc = k_ctx.shape[1]
    kv_lat = pl.BlockSpec((1, n, LANES), lambda bi, h, i: (bi, 0, h))
    kv_ctx = pl.BlockSpec((1, nc, LANES), lambda bi, h, i: (bi, 0, h))
    q_spec = pl.BlockSpec((1, tq, LANES), lambda bi, h, i: (bi, i, h))
    row = pl.BlockSpec((1, LANES), lambda bi, h, i: (0, 0))
    acc = pltpu.VMEM((tq, 2 * LANES), F32)
    run_max = pltpu.VMEM((tq, LANES), F32)
    return pl.pallas_call(
        functools.partial(_attn_kernel, tk=tk, out_scale=out_scale, bounded=bounded),
        out_shape=jax.ShapeDtypeStruct((b, n, V_WIDTH), BF16),
        grid=(b, HEADS, n // tq),
        in_specs=[row, q_spec, kv_lat, kv_lat, kv_ctx, kv_ctx, row],
        out_specs=q_spec,
        scratch_shapes=[acc, acc, run_max, run_max],
        compiler_params=_cparams(("parallel", "parallel", "arbitrary")),
        name="attn" if bounded else "attn_general",
    )(lam_row, q, k_lat, v_lat, k_ctx, v_ctx, sg)


def _s5_kernel(ul_ref, uc_ref, win_ref, m_ref, wo_ref, lam_ref, o_ref, xl_ref, xc_ref, s_ref, *, nb):
    ul = ul_ref[0]
    n_slab = xl_ref.shape[0]

    def to_slabs(x_ref, x):
        for k in range(n_slab):
            x_ref[k] = x[:, k * LANES:(k + 1) * LANES]

    to_slabs(xl_ref, _dot(ul, win_ref[0]))
    to_slabs(xc_ref, _dot(uc_ref[0], win_ref[0]))
    n_lat = ul.shape[0] // nb
    n_ctx = uc_ref.shape[1] // nb

    lam = lam_ref[0]
    lfr, lfi, lbr, lbi = (jnp.broadcast_to(lam[i:i + 1], (nb, LANES)) for i in range(4))

    def rows(c, n_chunks):
        return pl.ds(c, nb, stride=n_chunks)

    def advance(x_ref, c, n_chunks, slab, ar, ai, sr, si):
        xr = x_ref[slab, rows(c, n_chunks), :]
        xi = x_ref[slab + 1, rows(c, n_chunks), :]
        return ar * sr - ai * si + xr, ar * si + ai * sr + xi

    def ctx_body(i, carry):
        fr, fi, br, bi = carry
        fr, fi = advance(xc_ref, i, n_ctx, 0, lfr, lfi, fr, fi)
        br, bi = advance(xc_ref, n_ctx - 1 - i, n_ctx, 2, lbr, lbi, br, bi)
        return fr, fi, br, bi

    def lat_body(i, carry):
        fr, fi, br, bi = carry
        cb = n_lat - 1 - i
        s_ref[0, pl.ds(pl.multiple_of(i * nb, nb), nb), :] = fr
        s_ref[1, pl.ds(pl.multiple_of(i * nb, nb), nb), :] = fi
        s_ref[2, pl.ds(pl.multiple_of(cb * nb, nb), nb), :] = br
        s_ref[3, pl.ds(pl.multiple_of(cb * nb, nb), nb), :] = bi
        fr, fi = advance(xl_ref, i, n_lat, 0, lfr, lfi, fr, fi)
        br, bi = advance(xl_ref, cb, n_lat, 2, lbr, lbi, br, bi)
        return fr, fi, br, bi

    z = jnp.zeros((nb, LANES), F32)
    carry = lax.fori_loop(0, n_ctx, ctx_body, (z, z, z, z), unroll=SCAN_UNROLL)
    lax.fori_loop(0, n_lat, lat_body, carry, unroll=SCAN_UNROLL)

    def batch_major(k):
        return jnp.concatenate([s_ref[k, pl.ds(bi, n_lat, stride=nb), :] for bi in range(nb)], axis=0)

    s_in = jnp.concatenate([batch_major(k).astype(BF16) for k in range(n_slab)], axis=1)
    y = _dot(ul, m_ref[0]) + _dot(s_in, wo_ref[0])
    o_ref[0] = jax.nn.gelu(y).astype(BF16)


def _s5_call(u_lat, u_ctx, win, m, wo, lam16, b):
    npair, rows_lat, kw = u_lat.shape
    nb = S5_BATCH_BLOCK if b % S5_BATCH_BLOCK == 0 else b
    rl = rows_lat // b * nb
    rc = u_ctx.shape[1] // b * nb
    wspec = pl.BlockSpec((1, kw, kw), lambda p, h: (p, 0, 0))
    slabs = lambda r: pltpu.VMEM((kw // LANES, r, LANES), F32)
    return pl.pallas_call(
        functools.partial(_s5_kernel, nb=nb),
        out_shape=jax.ShapeDtypeStruct((npair, rows_lat, kw), BF16),
        grid=(npair, b // nb),
        in_specs=[pl.BlockSpec((1, rl, kw), lambda p, h: (p, h, 0)),
                  pl.BlockSpec((1, rc, kw), lambda p, h: (p, h, 0)),
                  wspec, wspec, wspec,
                  pl.BlockSpec((1, 4, LANES), lambda p, h: (p, 0, 0))],
        out_specs=pl.BlockSpec((1, rl, kw), lambda p, h: (p, h, 0)),
        scratch_shapes=[slabs(rl), slabs(rc), slabs(rl)],
        compiler_params=_cparams(("parallel", "parallel")),
        name="s5",
    )(u_lat, u_ctx, win, m, wo, lam16)


def _s5_weights(a_re, a_im, log_dt, b_re, b_im, c_re, c_im, d_skip):
    hp = lax.Precision.HIGHEST
    g_n, p_n = a_re.shape[1], a_re.shape[2]
    t_n = CHUNK
    dt = jnp.exp(log_dt.astype(F32))[..., None]
    ar, ai = a_re.astype(F32), a_im.astype(F32)
    mag = jnp.exp(ar * dt)
    lr, li = mag * jnp.cos(ai * dt), mag * jnp.sin(ai * dt)
    den = ar * ar + ai * ai
    nr, ni = lr - 1.0, li
    cr = (nr * ar + ni * ai) / den
    ci = (ni * ar - nr * ai) / den
    bbr = cr[..., None] * b_re - ci[..., None] * b_im
    bbi = cr[..., None] * b_im + ci[..., None] * b_re
    n = jnp.arange(t_n + 1, dtype=F32)[:, None, None, None]
    pm = jnp.exp(n * (ar * dt))
    pw_r, pw_i = pm * jnp.cos(n * (ai * dt)), pm * jnp.sin(n * (ai * dt))
    lb_r = pw_r[:t_n, ..., None] * bbr - pw_i[:t_n, ..., None] * bbi
    lb_i = pw_r[:t_n, ..., None] * bbi + pw_i[:t_n, ..., None] * bbr
    cre, cim = c_re.astype(F32), c_im.astype(F32)
    kern = (jnp.einsum('gip,tdgpj->tdgij', cre, lb_r, precision=hp)
            - jnp.einsum('gip,tdgpj->tdgij', cim, lb_i, precision=hp))
    hh = SSM_GROUP
    npair = g_n // 2
    eye2 = jnp.eye(2, dtype=F32)
    kw = 2 * t_n * hh

    lag0 = kern[0, 0] + kern[0, 1] + jnp.eye(hh, dtype=F32)[None] * d_skip.astype(F32)[:, :, None]
    by_lag = jnp.concatenate([kern[:0:-1, 1], lag0[None], kern[1:, 0]], axis=0)
    strip = jnp.einsum('ab,pajlq->pajlbq', eye2,
                       jnp.transpose(by_lag, (1, 3, 0, 2)).reshape(npair, 2, hh, 2 * t_n - 1, hh))
    strip = strip.reshape(npair, 2 * hh, (2 * t_n - 1) * 2 * hh)
    m_pair = jnp.stack([strip[:, :, (t_n - 1 - s) * 2 * hh:(t_n - 1 - s) * 2 * hh + kw] for s in range(t_n)],
                       axis=1).reshape(npair, kw, kw)

    def state_lanes(parts, mask_pair):
        x = jnp.stack(parts, axis=2).reshape(npair, 2, parts[0].shape[1], 4, 1, p_n)
        sel = eye2[None, :, None, None, :, None] if mask_pair else jnp.ones((1, 1, 1, 1, 2, 1), F32)
        return (x * sel).reshape(npair, 2, parts[0].shape[1], 8 * p_n)

    gsp = lambda z: jnp.transpose(z, (1, 0, 2))
    pf_r, pf_i, pb_r, pb_i = pw_r[:, 0], pw_i[:, 0], pw_r[:, 1], pw_i[:, 1]
    pa = state_lanes([gsp(pf_r[t_n - 1::-1]), gsp(pf_r[t_n - 1::-1]), gsp(pb_r[:t_n]), gsp(pb_r[:t_n])], True)
    pb = state_lanes([-gsp(pf_i[t_n - 1::-1]), gsp(pf_i[t_n - 1::-1]), -gsp(pb_i[:t_n]), gsp(pb_i[:t_n])], True)
    bt = lambda z: jnp.transpose(z, (0, 2, 1))
    ba = state_lanes([bt(bbr[0]), bt(bbi[0]), bt(bbr[1]), bt(bbi[1])], False)
    bb = state_lanes([bt(bbi[0]), bt(bbr[0]), bt(bbi[1]), bt(bbr[1])], False)
    sa = lambda z: jnp.transpose(z, (0, 2, 1, 3))
    win_pair = (sa(pa)[:, :, :, None, :] * ba[:, None] + sa(pb)[:, :, :, None, :] * bb[:, None]
                ).reshape(npair, kw, 8 * p_n)
    qa = state_lanes([gsp(pf_r[1:]), -gsp(pf_i[1:]), gsp(pb_r[t_n:0:-1]), -gsp(pb_i[t_n:0:-1])], True)
    qb = state_lanes([-gsp(pf_i[1:]), -gsp(pf_r[1:]), -gsp(pb_i[t_n:0:-1]), -gsp(pb_r[t_n:0:-1])], True)
    ca = state_lanes([cre] * 4, False)
    cb = state_lanes([cim] * 4, False)
    wo_t = (sa(qa)[:, :, :, None, :] * ca[:, None] + sa(qb)[:, :, :, None, :] * cb[:, None]
            ).reshape(npair, kw, 8 * p_n)
    wo_pair = jnp.swapaxes(wo_t.astype(BF16), 1, 2)
    lam16 = jnp.stack([pw_r[t_n, 0], pw_i[t_n, 0], pw_r[t_n, 1], pw_i[t_n, 1]], axis=1)
    lam16 = jnp.transpose(lam16.reshape(npair, 2, 4, p_n), (0, 2, 1, 3)).reshape(npair, 4, 2 * p_n)
    return win_pair.astype(BF16), m_pair.astype(BF16), wo_pair, lam16


def _outproj_kernel(a_ref, y_ref, x_ref, g1_ref, sh_ref, sc_ref, n2_ref, wglu_ref, bglu_ref, wout_ref,
                    wr_hi_ref, wr_lo_ref, br_ref, x1_ref, h2_ref, gate_ref, ys_ref):
    _chunk_rows_to_token_major(
        lambda pr, k: y_ref[pr, 0, :, k * LANES:(k + 1) * LANES].astype(F32), ys_ref)
    yf = jnp.concatenate([ys_ref[q] for q in range(ys_ref.shape[0])], axis=1)
    y = yf.astype(BF16)
    z = _dot(y, wglu_ref[...]) + bglu_ref[...]
    s = (yf * jax.nn.sigmoid(z)).astype(BF16)
    half = a_ref.shape[2]
    o = _dot(a_ref[0], wout_ref[:half, :]) + _dot(s, wout_ref[half:, :])
    x1 = x_ref[0] + g1_ref[0] * o
    x1_ref[0] = x1

    ms = jnp.mean(x1 * x1, axis=-1, keepdims=True)
    h = (x1 * lax.rsqrt(ms + RMS_EPS)) * n2_ref[...]
    h = h * (1.0 + sc_ref[0]) + sh_ref[0]
    for j in range(h.shape[1] // LANES):
        h2_ref[pl.ds(j, h.shape[0], stride=ROW_TILE), :] = h[:, j * LANES:(j + 1) * LANES]

    h_hi, h_lo = _split_bf16(h)
    hi_both = _dot(h_hi, jnp.concatenate([wr_hi_ref[...], wr_lo_ref[...]], axis=1))
    lg = hi_both[:, :LANES] + (hi_both[:, LANES:] + _dot(h_lo, wr_hi_ref[...])) + br_ref[...]
    lane = lax.broadcasted_iota(jnp.int32, lg.shape, 1)
    neg = jnp.float32(-jnp.inf)
    big = jnp.int32(LANES)

    def top1(vals):
        vmax = jnp.max(vals, axis=1, keepdims=True)
        idx = jnp.min(jnp.where(vals == vmax, lane, big), axis=1, keepdims=True)
        return vmax, idx

    is_grp = lane < MOE_GROUPS
    g_vals = jnp.where(is_grp, lg, neg)
    g_max, g_idx = top1(g_vals)
    p_grp = 1.0 / jnp.sum(jnp.where(is_grp, jnp.exp(g_vals - g_max), 0.0), axis=1, keepdims=True)
    e_lo = MOE_GROUPS + EXPERTS_PER_GROUP * g_idx
    in_grp = (lane >= e_lo) & (lane < e_lo + EXPERTS_PER_GROUP)
    e_vals = jnp.where(in_grp, lg, neg)
    v1, i1 = top1(e_vals)
    v2, i2 = top1(jnp.where(lane == i1, neg, e_vals))
    r = jnp.exp(v2 - v1)
    w1 = p_grp / (1.0 + r)
    w2 = w1 * r
    first_lo = i1 < i2
    a_loc = jnp.where(first_lo, i1, i2) - e_lo
    b_loc = jnp.where(first_lo, i2, i1) - e_lo
    pair = ((a_loc * (2 * EXPERTS_PER_GROUP - 1 - a_loc)) >> 1) + (b_loc - a_loc - 1)
    cls = g_idx * PAIRS_PER_GROUP + pair
    w_lo = jnp.where(first_lo, w1, w2)
    w_hi = jnp.where(first_lo, w2, w1)
    gate_ref[0] = (jnp.where(lane == 0, w_lo, 0.0) + jnp.where(lane == 1, w_hi, 0.0)
                   + jnp.where(lane == 2, cls.astype(F32), 0.0))


def _outproj_call(a, yg, x, g1, sh2, sc2, n2g, wglu, bglu, wout, wr_hi, wr_lo, br, tm):
    b, n, d = x.shape
    half = a.shape[2]
    tok = lambda w: pl.BlockSpec((1, tm, w), lambda bi, i: (bi, i, 0))
    mod = pl.BlockSpec((1, 1, d), lambda bi, i: (bi, 0, 0))
    const = lambda r, c: pl.BlockSpec((r, c), lambda bi, i: (0, 0))
    return pl.pallas_call(
        _outproj_kernel,
        out_shape=[jax.ShapeDtypeStruct((b, n, d), F32), jax.ShapeDtypeStruct((b * n * ROW_TILE, LANES), F32),
                   jax.ShapeDtypeStruct((b, n, LANES), F32)],
        grid=(b, n // tm),
        in_specs=[tok(half),
                  pl.BlockSpec((yg.shape[0], 1, tm // CHUNK, yg.shape[3]), lambda bi, i: (0, bi, i, 0)),
                  tok(d), mod, mod, mod, const(1, d),
                  const(half, half), const(1, half), const(d, d),
                  const(d, LANES), const(d, LANES), const(1, LANES)],
        out_specs=[tok(d), pl.BlockSpec((tm * ROW_TILE, LANES), lambda bi, i: (bi * (n // tm) + i, 0)), tok(LANES)],
        scratch_shapes=[pltpu.VMEM((half // LANES, tm, LANES), F32)],
        compiler_params=_cparams(("parallel", "parallel")),
        name="outproj",
    )(a, yg, x, g1, sh2, sc2, n2g, wglu, bglu, wout, wr_hi, wr_lo, br)


def _token_rows(tok, rows_per_token):
    return pl.ds(pl.multiple_of(tok * rows_per_token, rows_per_token), rows_per_token)


def _moe_expert_kernel(ea_ref, eb_ref, nv_ref, nt_ref, idx_hbm, h_hbm, wga_ref, wua_ref, wda_ref,
                       wgb_ref, wub_ref, wdb_ref, y_hbm, idx_smem, hbuf, ybuf, gsem, isem, psem, *, tm):
    i = pl.program_id(0)
    n_steps = pl.num_programs(0)
    n_tiles = nt_ref[0]
    hrows, yrows = tm * ROW_TILE, tm * 2 * ROW_TILE

    def idx_copy(tile):
        s = tile % IDX_SLOTS
        return pltpu.make_async_copy(idx_hbm.at[pl.ds(tile, 1), :], idx_smem.at[pl.ds(s, 1), :], isem.at[s])

    def pull_row(tile, r):
        tok = idx_smem[tile % IDX_SLOTS, r]
        s2 = tile % 3
        return pltpu.make_async_copy(h_hbm.at[_token_rows(tok, ROW_TILE), :],
                                     hbuf.at[pl.ds(s2 * hrows + r * ROW_TILE, ROW_TILE), :], gsem.at[s2])

    def pull_tile(tile):
        def body(r, carry):
            pull_row(tile, r).start()
            return carry
        lax.fori_loop(0, tm, body, 0, unroll=PUSH_UNROLL)

    def push_row(tile, r):
        tok = idx_smem[tile % IDX_SLOTS, r]
        s3 = tile % 3
        return pltpu.make_async_copy(ybuf.at[pl.ds(s3 * yrows + r * 2 * ROW_TILE, 2 * ROW_TILE), :],
                                     y_hbm.at[_token_rows(tok, 2 * ROW_TILE), :], psem.at[s3])

    def wait_pulls(tile):
        s2 = tile % 3
        pltpu.make_async_copy(h_hbm.at[pl.ds(0, hrows), :], hbuf.at[pl.ds(s2 * hrows, hrows), :], gsem.at[s2]).wait()

    def wait_pushes(tile):
        s3 = tile % 3
        n = nv_ref[tile] * (2 * ROW_TILE)

        @pl.when(n > 0)
        def _():
            pltpu.make_async_copy(ybuf.at[pl.ds(s3 * yrows, n), :], y_hbm.at[pl.ds(0, n), :], psem.at[s3]).wait()

    @pl.when(i == 0)
    def _():
        idx_copy(0).start()
        idx_copy(1).start()
        idx_copy(0).wait()
        idx_copy(1).wait()
        pull_tile(0)

        @pl.when(1 < n_tiles)
        def _():
            pull_tile(1)
        idx_copy(2).start()

    @pl.when(i + 3 < n_steps)
    def _():
        idx_copy(i + 3).start()

    @pl.when(i + 2 < n_steps)
    def _():
        idx_copy(i + 2).wait()

    @pl.when(i < n_tiles)
    def _():
        wait_pulls(i)

        @pl.when(i >= 3)
        def _():
            wait_pushes(i - 3)

        s3 = i % 3
        h = jnp.concatenate([hbuf[pl.ds(s3 * hrows + j, tm, stride=ROW_TILE), :] for j in range(ROW_TILE)],
                            axis=1).astype(BF16)
        pull_next = i + 2 < n_tiles
        n_push = jnp.where(i >= 1, nv_ref[jnp.maximum(i - 1, 0)], 0)

        def issue_rows(r0, r1):
            for r in range(r0, r1):
                @pl.when(pull_next)
                def _():
                    pull_row(i + 2, r).start()

                @pl.when(r < n_push)
                def _():
                    push_row(i - 1, r).start()

        def expert(wg_ref, wu_ref, wd_ref, row0):
            hid = jax.nn.silu(_dot(h, wg_ref[0])) * _dot(h, wu_ref[0])
            y = _dot(hid.astype(BF16), wd_ref[0])
            for j in range(ROW_TILE):
                ybuf[pl.ds(s3 * yrows + row0 + j, tm, stride=2 * ROW_TILE), :] = y[:, j * LANES:(j + 1) * LANES]

        issue_rows(0, tm // 2)
        expert(wga_ref, wua_ref, wda_ref, 0)
        issue_rows(tm // 2, tm)
        expert(wgb_ref, wub_ref, wdb_ref, ROW_TILE)

    @pl.when(i == n_tiles)
    def _():
        def body(r, carry):
            push_row(i - 1, r).start()
            return carry
        lax.fori_loop(0, nv_ref[i - 1], body, 0)

        for back in (3, 2, 1):
            @pl.when(i >= back)
            def _():
                wait_pushes(i - back)


def _moe_expert_call(ea, eb, nv, nt, idx, h_tiles, wg, wu, wd, n_tokens):
    n_steps, tm = idx.shape
    ne, d, f = wg.shape
    amap = lambda i, ea, eb, nv, nt: (ea[i], 0, 0)
    bmap = lambda i, ea, eb, nv, nt: (eb[i], 0, 0)
    up = lambda m: pl.BlockSpec((1, d, f), m)
    down = lambda m: pl.BlockSpec((1, f, d), m)
    hbm = pl.BlockSpec(memory_space=pl.ANY)
    grid_spec = pltpu.PrefetchScalarGridSpec(
        num_scalar_prefetch=4,
        grid=(n_steps,),
        in_specs=[hbm, hbm, up(amap), up(amap), down(amap), up(bmap), up(bmap), down(bmap)],
        out_specs=hbm,
        scratch_shapes=[pltpu.SMEM((IDX_SLOTS, tm), jnp.int32),
                        pltpu.VMEM((3 * tm * ROW_TILE, LANES), F32),
                        pltpu.VMEM((3 * tm * 2 * ROW_TILE, LANES), F32),
                        pltpu.SemaphoreType.DMA((3,)), pltpu.SemaphoreType.DMA((IDX_SLOTS,)),
                        pltpu.SemaphoreType.DMA((3,))])
    return pl.pallas_call(
        functools.partial(_moe_expert_kernel, tm=tm),
        out_shape=jax.ShapeDtypeStruct((n_tokens * 2 * ROW_TILE, LANES), F32),
        grid_spec=grid_spec,
        compiler_params=_cparams(("arbitrary",)),
        name="moe_experts",
    )(ea, eb, nv, nt, idx, h_tiles, wg, wu, wd, wg, wu, wd)


def _moe_combine_kernel(y_ref, route_ref, x1_ref, g2_ref, o_ref):
    tm = x1_ref.shape[0]
    ya = jnp.concatenate([y_ref[pl.ds(j, tm, stride=2 * ROW_TILE), :] for j in range(ROW_TILE)], axis=1)
    yb = jnp.concatenate([y_ref[pl.ds(ROW_TILE + j, tm, stride=2 * ROW_TILE), :] for j in range(ROW_TILE)], axis=1)
    route = route_ref[...]
    moe = route[:, 0:1] * ya + route[:, 1:2] * yb
    o_ref[...] = x1_ref[...] + g2_ref[0] * moe


def _moe_combine_call(y_tiles, route, x1, g2, tokens_per_batch, tm):
    t, d = x1.shape
    per_b = tokens_per_batch // tm
    tok = lambda w: pl.BlockSpec((tm, w), lambda i: (i, 0))
    return pl.pallas_call(
        _moe_combine_kernel,
        out_shape=jax.ShapeDtypeStruct((t, d), F32),
        grid=(t // tm,),
        in_specs=[pl.BlockSpec((tm * 2 * ROW_TILE, LANES), lambda i: (i, 0)), tok(LANES), tok(d),
                  pl.BlockSpec((1, 1, d), lambda i: (i // per_b, 0, 0))],
        out_specs=tok(d),
        compiler_params=_cparams(("parallel",)),
        name="moe_combine",
    )(y_tiles, route, x1, g2)


def _routing_plan(cls, tm):
    t = cls.shape[0]
    n_steps = t // tm + N_CLASSES + 1
    order = jnp.argsort(cls).astype(jnp.int32)
    classes = jnp.arange(N_CLASSES, dtype=jnp.int32)
    counts = jnp.sum((cls[:, None] == classes[None, :]).astype(jnp.int32), axis=0)
    cstart = jnp.cumsum(counts) - counts
    tiles_c = (counts + tm - 1) // tm
    tile_end = jnp.cumsum(tiles_c)
    n_tiles = tile_end[-1]
    tile_ids = jnp.arange(n_steps, dtype=jnp.int32)
    live = tile_ids < n_tiles
    c_of = jnp.sum((tile_end[None, :] <= jnp.minimum(tile_ids, n_tiles - 1)[:, None]).astype(jnp.int32), axis=1)
    k_of = jnp.minimum(tile_ids, n_tiles - 1) - (tile_end - tiles_c)[c_of]
    nv = jnp.where(live, jnp.clip(counts[c_of] - k_of * tm, 0, tm), 0).astype(jnp.int32)
    base = cstart[c_of] + k_of * tm
    idx = order[jnp.minimum(base[:, None] + jnp.arange(tm, dtype=jnp.int32)[None, :], t - 1)]
    grp, pair = c_of // PAIRS_PER_GROUP, c_of % PAIRS_PER_GROUP
    a_tab = jnp.array([a for a in range(EXPERTS_PER_GROUP) for _ in range(a + 1, EXPERTS_PER_GROUP)], jnp.int32)
    b_tab = jnp.array([b for a in range(EXPERTS_PER_GROUP) for b in range(a + 1, EXPERTS_PER_GROUP)], jnp.int32)
    ea = grp * EXPERTS_PER_GROUP + a_tab[pair]
    eb = grp * EXPERTS_PER_GROUP + b_tab[pair]
    return ea.astype(jnp.int32), eb.astype(jnp.int32), nv, n_tiles.reshape(1).astype(jnp.int32), idx


def _rope_tables(n_tokens):
    rows = n_tokens // GRID_W
    row = jnp.broadcast_to(jnp.arange(rows, dtype=F32)[:, None], (rows, GRID_W)).reshape(-1)
    col = jnp.broadcast_to(jnp.arange(GRID_W, dtype=F32)[None, :], (rows, GRID_W)).reshape(-1)
    half = HEAD_DIM // 2
    inv = ROPE_BASE ** (-jnp.arange(0, half, 2, dtype=F32) / half)
    ang = jnp.stack([row[:, None] * inv, col[:, None] * inv], axis=1)
    cos, sin = jnp.cos(ang), jnp.sin(ang)
    cos64 = jnp.concatenate([cos[:, 0], cos[:, 0], cos[:, 1], cos[:, 1]], axis=1)
    sin64 = jnp.concatenate([-sin[:, 0], sin[:, 0], -sin[:, 1], sin[:, 1]], axis=1)
    return jnp.tile(cos64, (1, LANES // HEAD_DIM)), jnp.tile(sin64, (1, LANES // HEAD_DIM))


def _pick_tile(n, target):
    t = min(n, target)
    while n % t:
        t //= 2
    return t


def kernel(x, c, ctx, c_ctx, w_ada, b_ada, norm1_g, w_in, q_norm_g, k_norm_g, lambda_q1, lambda_k1, lambda_q2, lambda_k2, subln_g, ssm_a_re, ssm_a_im, ssm_log_dt, ssm_b_re, ssm_b_im, ssm_c_re, ssm_c_im, ssm_d, w_glu, b_glu, w_out, norm2_g, w_route_group, b_route_group, w_route_expert, b_route_expert, w_exp_gate, w_exp_up, w_exp_down):
    depth = w_ada.shape[0]
    assert depth == 1, "single-layer block: the context stream is never updated"
    b, n_lat, d = x.shape
    n_ctx = ctx.shape[1]
    assert n_lat % CHUNK == 0 and n_ctx % CHUNK == 0 and n_lat % GRID_W == 0
    assert d == ROW_TILE * LANES, "MoE rows are moved as one (8, 128) tile per token"
    l = 0
    lam_init = 0.8 - 0.6 * math.exp(-0.3 * l)

    rows = b + 1
    rows_pad = -(-rows // 8) * 8
    cc = jnp.concatenate([c, c_ctx[None, :], jnp.zeros((rows_pad - rows, d), F32)], axis=0)
    mod = _mod_call(cc, w_ada[l], b_ada[l])
    sh1, sc1, g1, sh2, sc2, g2 = (mod[:b, i * d:(i + 1) * d].reshape(b, 1, d) for i in range(6))
    csh1, csc1 = (mod[b:b + 1, i * d:(i + 1) * d].reshape(1, 1, d) for i in range(2))

    w_in_bf = w_in[l].astype(BF16)
    bd = jnp.kron(jnp.eye(MXU_TILE // HEAD_DIM, dtype=F32), jnp.ones((HEAD_DIM, HEAD_DIM), F32)).astype(BF16)
    qg = jnp.tile(q_norm_g[l], LANES // HEAD_DIM).reshape(1, LANES)
    kg = jnp.tile(k_norm_g[l], LANES // HEAD_DIM).reshape(1, LANES)
    cosf, sinf = _rope_tables(n_lat)
    ones_c, zeros_c = jnp.ones((n_ctx, LANES), F32), jnp.zeros((n_ctx, LANES), F32)
    g1n = norm1_g[l].reshape(1, d)
    tm = _pick_tile(n_lat, INPROJ_TILE)
    q_x, k_x, v_x, u_x = _inproj_call(x, sh1, sc1, g1n, w_in_bf, qg, kg, cosf, sinf, bd, tm, "inproj_lat")
    _, k_c, v_c, u_c = _inproj_call(ctx, csh1, csc1, g1n, w_in_bf, qg, kg, ones_c, zeros_c, bd,
                                    _pick_tile(n_ctx, INPROJ_TILE), "inproj_ctx")

    e1 = jnp.exp(jnp.sum(lambda_q1[l] * lambda_k1[l]))
    e2 = jnp.exp(jnp.sum(lambda_q2[l] * lambda_k2[l]))
    lam_row = jnp.full((1, LANES), e1 - e2 + lam_init, F32)
    score_bound = math.sqrt(HEAD_DIM) * jnp.max(jnp.abs(q_norm_g[l])) * jnp.max(jnp.abs(k_norm_g[l]))

    def attn(bounded):
        return lambda *ops: _attn_call(*ops, **_attn_cfg(n_lat), out_scale=1.0 - lam_init, bounded=bounded)

    a_x = lax.cond(score_bound <= SCORE_BOUND, attn(True), attn(False),
                   lam_row, q_x, k_x, v_x, k_c, v_c, subln_g[l].reshape(1, LANES))

    win, m_op, wo, lam16 = _s5_weights(ssm_a_re[l], ssm_a_im[l], ssm_log_dt[l], ssm_b_re[l], ssm_b_im[l],
                                       ssm_c_re[l], ssm_c_im[l], ssm_d[l])
    n_pairs, kw = u_x.shape[0], u_x.shape[3]
    yg = _s5_call(u_x.reshape(n_pairs, b * (n_lat // CHUNK), kw), u_c.reshape(n_pairs, b * (n_ctx // CHUNK), kw),
                  win, m_op, wo, lam16, b)
    yg = yg.reshape(n_pairs, b, n_lat // CHUNK, kw)

    wr = jnp.concatenate([w_route_group[l], w_route_expert[l]], axis=1)
    wr = jnp.pad(wr, ((0, 0), (0, LANES - wr.shape[1])))
    wr_hi, wr_lo = _split_bf16(wr)
    br = jnp.pad(jnp.concatenate([b_route_group[l], b_route_expert[l]]), (0, LANES - MOE_GROUPS - N_EXPERTS))
    x1, h2, route = _outproj_call(a_x, yg, x, g1, sh2, sc2, norm2_g[l].reshape(1, d),
                                 w_glu[l].astype(BF16), b_glu[l].reshape(1, -1), w_out[l].astype(BF16),
                                 wr_hi, wr_lo, br.reshape(1, LANES), _pick_tile(n_lat, OUTPROJ_TILE))

    t_all = b * n_lat
    route = route.reshape(t_all, LANES)
    ea, eb, nv, n_tiles, idx = _routing_plan(route[:, 2].astype(jnp.int32), _pick_tile(t_all, MOE_TILE))
    y_tiles = _moe_expert_call(ea, eb, nv, n_tiles, idx, h2, w_exp_gate[l].astype(BF16),
                               w_exp_up[l].astype(BF16), w_exp_down[l].astype(BF16), t_all)
    out = _moe_combine_call(y_tiles, route, x1.reshape(t_all, d), g2, n_lat, _pick_tile(n_lat, COMBINE_TILE))
    return out.reshape(b, n_lat, d)
```

```python
import functools
import math

import jax
import jax.numpy as jnp
from jax import lax
from jax.experimental import pallas as pl
from jax.experimental.pallas import tpu as pltpu

F32 = jnp.float32
BF16 = jnp.bfloat16

LANES = 128
MXU_TILE = 256
HEADS = 4
HEAD_DIM = 64
QK_WIDTH = HEADS * 2 * HEAD_DIM
V_WIDTH = HEADS * 2 * HEAD_DIM
GRID_W = 64
ROPE_BASE = 10000.0
SSM_GROUP = 16
SSM_STATE = 64
CHUNK = 16
MOE_GROUPS = 4
EXPERTS_PER_GROUP = 8
N_EXPERTS = MOE_GROUPS * EXPERTS_PER_GROUP
RMS_EPS = 1e-6
INPROJ_TILE = 1024
OUTPROJ_TILE = 1024
ATTN_TQ = 512
ATTN_TK = 4096
SCORE_BOUND = 60.0
SCAN_UNROLL = 8
S5_BATCH_BLOCK = 8
PAIRS_PER_GROUP = EXPERTS_PER_GROUP * (EXPERTS_PER_GROUP - 1) // 2
N_CLASSES = MOE_GROUPS * PAIRS_PER_GROUP
ROW_TILE = 8
MOE_TILE = 256
PUSH_UNROLL = 8
IDX_SLOTS = 5
COMBINE_TILE = 512
VMEM_LIMIT = 48 * 1024 * 1024


def _cparams(sem):
    return pltpu.CompilerParams(dimension_semantics=sem, vmem_limit_bytes=VMEM_LIMIT)


def _split_bf16(a):
    hi = a.astype(BF16)
    lo = (a - hi.astype(F32)).astype(BF16)
    return hi, lo


def _dot(a, b):
    return jnp.dot(a, b, preferred_element_type=F32)


def _dot3(a, b):
    a_hi, a_lo = _split_bf16(a)
    b_hi, b_lo = _split_bf16(b)
    return _dot(a_hi, b_hi) + (_dot(a_hi, b_lo) + _dot(a_lo, b_hi))


def _mod_kernel(c_ref, w_ref, b_ref, o_ref):
    c = c_ref[...]
    a = c * jax.nn.sigmoid(c)
    o_ref[...] = _dot3(a, w_ref[...]) + b_ref[...]


def _mod_call(cc, w_ada, b_ada):
    rows, d = cc.shape
    n = w_ada.shape[1]
    bn = 1024
    return pl.pallas_call(
        _mod_kernel,
        out_shape=jax.ShapeDtypeStruct((rows, n), F32),
        grid=(n // bn,),
        in_specs=[pl.BlockSpec((rows, d), lambda j: (0, 0)),
                  pl.BlockSpec((d, bn), lambda j: (0, j)),
                  pl.BlockSpec((1, bn), lambda j: (0, j))],
        out_specs=pl.BlockSpec((rows, bn), lambda j: (0, j)),
        compiler_params=_cparams(("arbitrary",)),
        name="mod",
    )(cc, w_ada, b_ada.reshape(1, n))


def _inproj_kernel(x_ref, sh_ref, sc_ref, g_ref, w_ref, qg_ref, kg_ref, cos_ref, sin_ref, bd_ref, perm_ref,
                   q_ref, k_ref, v_ref, u_ref):
    x = x_ref[0]
    ms = jnp.mean(x * x, axis=-1, keepdims=True)
    h = (x * lax.rsqrt(ms + RMS_EPS)) * g_ref[...]
    h = h * (1.0 + sc_ref[0]) + sh_ref[0]
    p = _dot(h.astype(BF16), w_ref[...])

    cosf = cos_ref[...]
    sinf = sin_ref[...]
    lane = lax.broadcasted_iota(jnp.int32, cosf.shape, 1)
    first_half = (lane % 32) < 16

    def norm_rope(t, gain, scale):
        sq = (t * t).astype(BF16)
        half = bd_ref.shape[0]
        ss = jnp.concatenate([_dot(sq[:, c:c + half], bd_ref[...]) for c in range(0, QK_WIDTH, half)], axis=1)
        t = t * lax.rsqrt(ss * (1.0 / HEAD_DIM) + RMS_EPS)
        outs = []
        for s in range(QK_WIDTH // LANES):
            ts = t[:, s * LANES:(s + 1) * LANES] * gain
            partner = jnp.where(first_half, pltpu.roll(ts, LANES - 16, 1), pltpu.roll(ts, 16, 1))
            outs.append(((ts * cosf + partner * sinf) * scale).astype(BF16))
        return jnp.concatenate(outs, axis=1)

    q_ref[0] = norm_rope(p[:, :QK_WIDTH], qg_ref[...], HEAD_DIM ** -0.5 * math.log2(math.e))
    k_ref[0] = norm_rope(p[:, QK_WIDTH:2 * QK_WIDTH], kg_ref[...], 1.0)
    v_ref[0] = p[:, 2 * QK_WIDTH:2 * QK_WIDTH + V_WIDTH].astype(BF16)
    u_t = _dot(perm_ref[...], p[:, 2 * QK_WIDTH + V_WIDTH:].astype(BF16))

    def store_u(pr, k, val):
        u_ref[pr, 0, :, k * LANES:(k + 1) * LANES] = val.astype(BF16)

    _step_major_to_chunk_rows(u_t, store_u)


PAIR_W = 2 * SSM_GROUP
PAIRS_PER_TILE = LANES // PAIR_W


def _quarter_select(pieces):
    lane = lax.broadcasted_iota(jnp.int32, pieces[0].shape, 1)
    acc = pieces[0]
    for r in range(1, len(pieces)):
        acc = jnp.where(lane // PAIR_W == r, pieces[r], acc)
    return acc


def _step_perm(tm):
    n_chunk = tm // CHUNK
    src = (jnp.arange(tm) % n_chunk) * CHUNK + jnp.arange(tm) // n_chunk
    return (src[:, None] == jnp.arange(tm)[None, :]).astype(BF16)


def _step_major_to_chunk_rows(u_t, store):
    n_chunk = u_t.shape[0] // CHUNK
    for k in range(CHUNK // PAIRS_PER_TILE):
        for pr in range(u_t.shape[1] // PAIR_W):
            q, r_src = divmod(pr, PAIRS_PER_TILE)
            pieces = []
            for r in range(PAIRS_PER_TILE):
                t = PAIRS_PER_TILE * k + r
                src = u_t[t * n_chunk:(t + 1) * n_chunk, q * LANES:(q + 1) * LANES]
                shift = ((r - r_src) % PAIRS_PER_TILE) * PAIR_W
                pieces.append(pltpu.roll(src, shift, 1) if shift else src)
            store(pr, k, _quarter_select(pieces))


def _chunk_rows_to_token_major(load, ys_ref):
    n_tiles, n_chunk = ys_ref.shape[0], ys_ref.shape[1] // CHUNK
    for t in range(CHUNK):
        k, r_src = divmod(t, PAIRS_PER_TILE)
        for q in range(n_tiles):
            pieces = []
            for r in range(PAIRS_PER_TILE):
                shift = ((r - r_src) % PAIRS_PER_TILE) * PAIR_W
                src = load(PAIRS_PER_TILE * q + r, k)
                pieces.append(pltpu.roll(src, shift, 1) if shift else src)
            ys_ref[q, pl.ds(t, n_chunk, stride=CHUNK), :] = _quarter_select(pieces)


def _inproj_call(x, sh, sc, g, w_bf, qg, kg, cosf, sinf, bd, tm, name):
    b, n, d = x.shape
    wn = w_bf.shape[1]
    per_batch = sh.shape[0] > 1
    mod_map = (lambda bi, i: (bi, 0, 0)) if per_batch else (lambda bi, i: (0, 0, 0))
    const2 = lambda bi, i: (0, 0)
    n_pairs = QK_WIDTH // PAIR_W
    outs = [jax.ShapeDtypeStruct((b, n, QK_WIDTH), BF16)] * 3 + [
        jax.ShapeDtypeStruct((n_pairs, b, n // CHUNK, CHUNK * PAIR_W), BF16)]
    tok_spec = pl.BlockSpec((1, tm, QK_WIDTH), lambda bi, i: (bi, i, 0))
    u_spec = pl.BlockSpec((n_pairs, 1, tm // CHUNK, CHUNK * PAIR_W), lambda bi, i: (0, bi, i, 0))
    return pl.pallas_call(
        _inproj_kernel,
        out_shape=outs,
        grid=(b, n // tm),
        in_specs=[pl.BlockSpec((1, tm, d), lambda bi, i: (bi, i, 0)),
                  pl.BlockSpec((1, 1, d), mod_map),
                  pl.BlockSpec((1, 1, d), mod_map),
                  pl.BlockSpec((1, d), const2),
                  pl.BlockSpec((d, wn), const2),
                  pl.BlockSpec((1, LANES), const2),
                  pl.BlockSpec((1, LANES), const2),
                  pl.BlockSpec((tm, LANES), lambda bi, i: (i, 0)),
                  pl.BlockSpec((tm, LANES), lambda bi, i: (i, 0)),
                  pl.BlockSpec(bd.shape, const2),
                  pl.BlockSpec((tm, tm), const2)],
        out_specs=[tok_spec] * 3 + [u_spec],
        compiler_params=_cparams(("parallel", "parallel")),
        name=name,
    )(x, sh, sc, g, w_bf, qg, kg, cosf, sinf, bd, _step_perm(tm))


def _attn_kernel(lam_ref, q_ref, kl_ref, vl_ref, kc_ref, vc_ref, sg_ref, o_ref, a1_ref, a2_ref, m1_ref, m2_ref,
                 *, tk, out_scale, bounded):
    q = q_ref[0]
    lane = lax.broadcasted_iota(jnp.int32, q.shape, 1)
    zero = jnp.zeros_like(q)
    qa = jnp.where(lane < HEAD_DIM, q, zero)
    qb = jnp.where(lane >= HEAD_DIM, q, zero)

    a1_ref[...] = jnp.zeros(a1_ref.shape, F32)
    a2_ref[...] = jnp.zeros(a2_ref.shape, F32)
    if not bounded:
        m1_ref[...] = jnp.full(m1_ref.shape, -jnp.inf, F32)
        m2_ref[...] = jnp.full(m2_ref.shape, -jnp.inf, F32)

    def ones_col(rows):
        col = lax.broadcasted_iota(jnp.int32, (rows, LANES), 1)
        return jnp.where(col == 0, 1.0, 0.0).astype(BF16)

    def one_map(qm, kc, va, a_ref, m_ref):
        s = lax.dot_general(qm, kc, (((1,), (1,)), ((), ())), preferred_element_type=F32)
        if bounded:
            a_ref[...] += _dot(jnp.exp2(s).astype(BF16), va)
        else:
            m_prev = m_ref[...]
            m_next = jnp.maximum(m_prev, jnp.max(s, axis=1, keepdims=True))
            p = jnp.exp2(s - m_next[:, :1])
            alpha = jnp.exp2(m_prev - m_next)
            a_ref[...] = jnp.concatenate([alpha, alpha], axis=1) * a_ref[...] + _dot(p.astype(BF16), va)
            m_ref[...] = m_next

    def step(kc, vc, ones):
        va = jnp.concatenate([vc, ones], axis=1)
        one_map(qa, kc, va, a1_ref, m1_ref)
        one_map(qb, kc, va, a2_ref, m2_ref)

    ones_lat = ones_col(tk)

    def lat_body(j, carry):
        off = pl.multiple_of(j * tk, tk)
        step(kl_ref[0, pl.ds(off, tk), :], vl_ref[0, pl.ds(off, tk), :], ones_lat)
        return carry

    lax.fori_loop(0, kl_ref.shape[1] // tk, lat_body, 0)
    step(kc_ref[0], vc_ref[0], ones_col(kc_ref.shape[1]))

    lam = lam_ref[...]
    a1, a2 = a1_ref[...], a2_ref[...]
    o = a1[:, :LANES] / a1[:, LANES:LANES + 1] - lam * (a2[:, :LANES] / a2[:, LANES:LANES + 1])
    ms = jnp.mean(o * o, axis=-1, keepdims=True)
    o = o * lax.rsqrt(ms + RMS_EPS) * sg_ref[...]
    o_ref[0] = (o * out_scale).astype(BF16)


def _attn_cfg(n_lat):
    return dict(tq=_pick_tile(n_lat, ATTN_TQ), tk=_pick_tile(n_lat, ATTN_TK))


def _attn_call(lam_row, q, k_lat, v_lat, k_ctx, v_ctx, sg, tq, tk, out_scale=1.0, bounded=True):
    b, n, _ = q.shape
    nc = k_ctx.shape[1]
    kv_lat = pl.BlockSpec((1, n, LANES), lambda bi, h, i: (bi, 0, h))
    kv_ctx = pl.BlockSpec((1, nc, LANES), lambda bi, h, i: (bi, 0, h))
    q_spec = pl.BlockSpec((1, tq, LANES), lambda bi, h, i: (bi, i, h))
    row = pl.BlockSpec((1, LANES), lambda bi, h, i: (0, 0))
    acc = pltpu.VMEM((tq, 2 * LANES), F32)
    run_max = pltpu.VMEM((tq, LANES), F32)
    return pl.pallas_call(
        functools.partial(_attn_kernel, tk=tk, out_scale=out_scale, bounded=bounded),
        out_shape=jax.ShapeDtypeStruct((b, n, V_WIDTH), BF16),
        grid=(b, HEADS, n // tq),
        in_specs=[row, q_spec, kv_lat, kv_lat, kv_ctx, kv_ctx, row],
        out_specs=q_spec,
        scratch_shapes=[acc, acc, run_max, run_max],
        compiler_params=_cparams(("parallel", "parallel", "arbitrary")),
        name="attn" if bounded else "attn_general",
    )(lam_row, q, k_lat, v_lat, k_ctx, v_ctx, sg)


def _s5_kernel(ul_ref, uc_ref, win_ref, m_ref, wo_ref, lam_ref, o_ref, xl_ref, xc_ref, s_ref, *, nb):
    ul = ul_ref[0]
    n_slab = xl_ref.shape[0]

    def to_slabs(x_ref, x):
        for k in range(n_slab):
            x_ref[k] = x[:, k * LANES:(k + 1) * LANES]

    to_slabs(xl_ref, _dot(ul, win_ref[0]))
    to_slabs(xc_ref, _dot(uc_ref[0], win_ref[0]))
    n_lat = ul.shape[0] // nb
    n_ctx = uc_ref.shape[1] // nb

    lam = lam_ref[0]
    lfr, lfi, lbr, lbi = (jnp.broadcast_to(lam[i:i + 1], (nb, LANES)) for i in range(4))

    def rows(c, n_chunks):
        return pl.ds(c, nb, stride=n_chunks)

    def advance(x_ref, c, n_chunks, slab, ar, ai, sr, si):
        xr = x_ref[slab, rows(c, n_chunks), :]
        xi = x_ref[slab + 1, rows(c, n_chunks), :]
        return ar * sr - ai * si + xr, ar * si + ai * sr + xi

    def ctx_body(i, carry):
        fr, fi, br, bi = carry
        fr, fi = advance(xc_ref, i, n_ctx, 0, lfr, lfi, fr, fi)
        br, bi = advance(xc_ref, n_ctx - 1 - i, n_ctx, 2, lbr, lbi, br, bi)
        return fr, fi, br, bi

    def lat_body(i, carry):
        fr, fi, br, bi = carry
        cb = n_lat - 1 - i
        s_ref[0, pl.ds(pl.multiple_of(i * nb, nb), nb), :] = fr
        s_ref[1, pl.ds(pl.multiple_of(i * nb, nb), nb), :] = fi
        s_ref[2, pl.ds(pl.multiple_of(cb * nb, nb), nb), :] = br
        s_ref[3, pl.ds(pl.multiple_of(cb * nb, nb), nb), :] = bi
        fr, fi = advance(xl_ref, i, n_lat, 0, lfr, lfi, fr, fi)
        br, bi = advance(xl_ref, cb, n_lat, 2, lbr, lbi, br, bi)
        return fr, fi, br, bi

    z = jnp.zeros((nb, LANES), F32)
    carry = lax.fori_loop(0, n_ctx, ctx_body, (z, z, z, z), unroll=SCAN_UNROLL)
    lax.fori_loop(0, n_lat, lat_body, carry, unroll=SCAN_UNROLL)

    def batch_major(k):
        return jnp.concatenate([s_ref[k, pl.ds(bi, n_lat, stride=nb), :] for bi in range(nb)], axis=0)

    s_in = jnp.concatenate([batch_major(k).astype(BF16) for k in range(n_slab)], axis=1)
    y = _dot(ul, m_ref[0]) + _dot(s_in, wo_ref[0])
    o_ref[0] = jax.nn.gelu(y).astype(BF16)


def _s5_call(u_lat, u_ctx, win, m, wo, lam16, b):
    npair, rows_lat, kw = u_lat.shape
    nb = S5_BATCH_BLOCK if b % S5_BATCH_BLOCK == 0 else b
    rl = rows_lat // b * nb
    rc = u_ctx.shape[1] // b * nb
    wspec = pl.BlockSpec((1, kw, kw), lambda p, h: (p, 0, 0))
    slabs = lambda r: pltpu.VMEM((kw // LANES, r, LANES), F32)
    return pl.pallas_call(
        functools.partial(_s5_kernel, nb=nb),
        out_shape=jax.ShapeDtypeStruct((npair, rows_lat, kw), BF16),
        grid=(npair, b // nb),
        in_specs=[pl.BlockSpec((1, rl, kw), lambda p, h: (p, h, 0)),
                  pl.BlockSpec((1, rc, kw), lambda p, h: (p, h, 0)),
                  wspec, wspec, wspec,
                  pl.BlockSpec((1, 4, LANES), lambda p, h: (p, 0, 0))],
        out_specs=pl.BlockSpec((1, rl, kw), lambda p, h: (p, h, 0)),
        scratch_shapes=[slabs(rl), slabs(rc), slabs(rl)],
        compiler_params=_cparams(("parallel", "parallel")),
        name="s5",
    )(u_lat, u_ctx, win, m, wo, lam16)


def _s5_weights(a_re, a_im, log_dt, b_re, b_im, c_re, c_im, d_skip):
    hp = lax.Precision.HIGHEST
    g_n, p_n = a_re.shape[1], a_re.shape[2]
    t_n = CHUNK
    dt = jnp.exp(log_dt.astype(F32))[..., None]
    ar, ai = a_re.astype(F32), a_im.astype(F32)
    mag = jnp.exp(ar * dt)
    lr, li = mag * jnp.cos(ai * dt), mag * jnp.sin(ai * dt)
    den = ar * ar + ai * ai
    nr, ni = lr - 1.0, li
    cr = (nr * ar + ni * ai) / den
    ci = (ni * ar - nr * ai) / den
    bbr = cr[..., None] * b_re - ci[..., None] * b_im
    bbi = cr[..., None] * b_im + ci[..., None] * b_re
    n = jnp.arange(t_n + 1, dtype=F32)[:, None, None, None]
    pm = jnp.exp(n * (ar * dt))
    pw_r, pw_i = pm * jnp.cos(n * (ai * dt)), pm * jnp.sin(n * (ai * dt))
    lb_r = pw_r[:t_n, ..., None] * bbr - pw_i[:t_n, ..., None] * bbi
    lb_i = pw_r[:t_n, ..., None] * bbi + pw_i[:t_n, ..., None] * bbr
    cre, cim = c_re.astype(F32), c_im.astype(F32)
    kern = (jnp.einsum('gip,tdgpj->tdgij', cre, lb_r, precision=hp)
            - jnp.einsum('gip,tdgpj->tdgij', cim, lb_i, precision=hp))
    hh = SSM_GROUP
    npair = g_n // 2
    eye2 = jnp.eye(2, dtype=F32)
    kw = 2 * t_n * hh

    lag0 = kern[0, 0] + kern[0, 1] + jnp.eye(hh, dtype=F32)[None] * d_skip.astype(F32)[:, :, None]
    by_lag = jnp.concatenate([kern[:0:-1, 1], lag0[None], kern[1:, 0]], axis=0)
    strip = jnp.einsum('ab,pajlq->pajlbq', eye2,
                       jnp.transpose(by_lag, (1, 3, 0, 2)).reshape(npair, 2, hh, 2 * t_n - 1, hh))
    strip = strip.reshape(npair, 2 * hh, (2 * t_n - 1) * 2 * hh)
    m_pair = jnp.stack([strip[:, :, (t_n - 1 - s) * 2 * hh:(t_n - 1 - s) * 2 * hh + kw] for s in range(t_n)],
                       axis=1).reshape(npair, kw, kw)

    def state_lanes(parts, mask_pair):
        x = jnp.stack(parts, axis=2).reshape(npair, 2, parts[0].shape[1], 4, 1, p_n)
        sel = eye2[None, :, None, None, :, None] if mask_pair else jnp.ones((1, 1, 1, 1, 2, 1), F32)
        return (x * sel).reshape(npair, 2, parts[0].shape[1], 8 * p_n)

    gsp = lambda z: jnp.transpose(z, (1, 0, 2))
    pf_r, pf_i, pb_r, pb_i = pw_r[:, 0], pw_i[:, 0], pw_r[:, 1], pw_i[:, 1]
    pa = state_lanes([gsp(pf_r[t_n - 1::-1]), gsp(pf_r[t_n - 1::-1]), gsp(pb_r[:t_n]), gsp(pb_r[:t_n])], True)
    pb = state_lanes([-gsp(pf_i[t_n - 1::-1]), gsp(pf_i[t_n - 1::-1]), -gsp(pb_i[:t_n]), gsp(pb_i[:t_n])], True)
    bt = lambda z: jnp.transpose(z, (0, 2, 1))
    ba = state_lanes([bt(bbr[0]), bt(bbi[0]), bt(bbr[1]), bt(bbi[1])], False)
    bb = state_lanes([bt(bbi[0]), bt(bbr[0]), bt(bbi[1]), bt(bbr[1])], False)
    sa = lambda z: jnp.transpose(z, (0, 2, 1, 3))
    win_pair = (sa(pa)[:, :, :, None, :] * ba[:, None] + sa(pb)[:, :, :, None, :] * bb[:, None]
                ).reshape(npair, kw, 8 * p_n)
    qa = state_lanes([gsp(pf_r[1:]), -gsp(pf_i[1:]), gsp(pb_r[t_n:0:-1]), -gsp(pb_i[t_n:0:-1])], True)
    qb = state_lanes([-gsp(pf_i[1:]), -gsp(pf_r[1:]), -gsp(pb_i[t_n:0:-1]), -gsp(pb_r[t_n:0:-1])], True)
    ca = state_lanes([cre] * 4, False)
    cb = state_lanes([cim] * 4, False)
    wo_t = (sa(qa)[:, :, :, None, :] * ca[:, None] + sa(qb)[:, :, :, None, :] * cb[:, None]
            ).reshape(npair, kw, 8 * p_n)
    wo_pair = jnp.swapaxes(wo_t.astype(BF16), 1, 2)
    lam16 = jnp.stack([pw_r[t_n, 0], pw_i[t_n, 0], pw_r[t_n, 1], pw_i[t_n, 1]], axis=1)
    lam16 = jnp.transpose(lam16.reshape(npair, 2, 4, p_n), (0, 2, 1, 3)).reshape(npair, 4, 2 * p_n)
    return win_pair.astype(BF16), m_pair.astype(BF16), wo_pair, lam16


def _outproj_kernel(a_ref, y_ref, x_ref, g1_ref, sh_ref, sc_ref, n2_ref, wglu_ref, bglu_ref, wout_ref,
                    wr_hi_ref, wr_lo_ref, br_ref, x1_ref, h2_ref, gate_ref, ys_ref):
    _chunk_rows_to_token_major(
        lambda pr, k: y_ref[pr, 0, :, k * LANES:(k + 1) * LANES].astype(F32), ys_ref)
    yf = jnp.concatenate([ys_ref[q] for q in range(ys_ref.shape[0])], axis=1)
    y = yf.astype(BF16)
    z = _dot(y, wglu_ref[...]) + bglu_ref[...]
    s = (yf * jax.nn.sigmoid(z)).astype(BF16)
    o = _dot(jnp.concatenate([a_ref[0], s], axis=1), wout_ref[...])
    x1 = x_ref[0] + g1_ref[0] * o
    x1_ref[0] = x1

    ms = jnp.mean(x1 * x1, axis=-1, keepdims=True)
    h = (x1 * lax.rsqrt(ms + RMS_EPS)) * n2_ref[...]
    h = h * (1.0 + sc_ref[0]) + sh_ref[0]
    for j in range(h.shape[1] // LANES):
        h2_ref[pl.ds(j, h.shape[0], stride=ROW_TILE), :] = h[:, j * LANES:(j + 1) * LANES]

    h_hi, h_lo = _split_bf16(h)
    hi_both = _dot(h_hi, jnp.concatenate([wr_hi_ref[...], wr_lo_ref[...]], axis=1))
    lg = hi_both[:, :LANES] + (hi_both[:, LANES:] + _dot(h_lo, wr_hi_ref[...])) + br_ref[...]
    lane = lax.broadcasted_iota(jnp.int32, lg.shape, 1)
    neg = jnp.float32(-jnp.inf)
    big = jnp.int32(LANES)

    def top1(vals):
        vmax = jnp.max(vals, axis=1, keepdims=True)
        idx = jnp.min(jnp.where(vals == vmax, lane, big), axis=1, keepdims=True)
        return vmax, idx

    is_grp = lane < MOE_GROUPS
    g_vals = jnp.where(is_grp, lg, neg)
    g_max, g_idx = top1(g_vals)
    p_grp = 1.0 / jnp.sum(jnp.where(is_grp, jnp.exp(g_vals - g_max), 0.0), axis=1, keepdims=True)
    e_lo = MOE_GROUPS + EXPERTS_PER_GROUP * g_idx
    in_grp = (lane >= e_lo) & (lane < e_lo + EXPERTS_PER_GROUP)
    e_vals = jnp.where(in_grp, lg, neg)
    v1, i1 = top1(e_vals)
    v2, i2 = top1(jnp.where(lane == i1, neg, e_vals))
    r = jnp.exp(v2 - v1)
    w1 = p_grp / (1.0 + r)
    w2 = w1 * r
    first_lo = i1 < i2
    a_loc = jnp.where(first_lo, i1, i2) - e_lo
    b_loc = jnp.where(first_lo, i2, i1) - e_lo
    pair = ((a_loc * (2 * EXPERTS_PER_GROUP - 1 - a_loc)) >> 1) + (b_loc - a_loc - 1)
    cls = g_idx * PAIRS_PER_GROUP + pair
    w_lo = jnp.where(first_lo, w1, w2)
    w_hi = jnp.where(first_lo, w2, w1)
    gate_ref[0] = (jnp.where(lane == 0, w_lo, 0.0) + jnp.where(lane == 1, w_hi, 0.0)
                   + jnp.where(lane == 2, cls.astype(F32), 0.0))


def _outproj_call(a, yg, x, g1, sh2, sc2, n2g, wglu, bglu, wout, wr_hi, wr_lo, br, tm):
    b, n, d = x.shape
    half = a.shape[2]
    tok = lambda w: pl.BlockSpec((1, tm, w), lambda bi, i: (bi, i, 0))
    mod = pl.BlockSpec((1, 1, d), lambda bi, i: (bi, 0, 0))
    const = lambda r, c: pl.BlockSpec((r, c), lambda bi, i: (0, 0))
    return pl.pallas_call(
        _outproj_kernel,
        out_shape=[jax.ShapeDtypeStruct((b, n, d), F32), jax.ShapeDtypeStruct((b * n * ROW_TILE, LANES), F32),
                   jax.ShapeDtypeStruct((b, n, LANES), F32)],
        grid=(b, n // tm),
        in_specs=[tok(half),
                  pl.BlockSpec((yg.shape[0], 1, tm // CHUNK, yg.shape[3]), lambda bi, i: (0, bi, i, 0)),
                  tok(d), mod, mod, mod, const(1, d),
                  const(half, half), const(1, half), const(d, d),
                  const(d, LANES), const(d, LANES), const(1, LANES)],
        out_specs=[tok(d), pl.BlockSpec((tm * ROW_TILE, LANES), lambda bi, i: (bi * (n // tm) + i, 0)), tok(LANES)],
        scratch_shapes=[pltpu.VMEM((half // LANES, tm, LANES), F32)],
        compiler_params=_cparams(("parallel", "parallel")),
        name="outproj",
    )(a, yg, x, g1, sh2, sc2, n2g, wglu, bglu, wout, wr_hi, wr_lo, br)


def _token_rows(tok, rows_per_token):
    return pl.ds(pl.multiple_of(tok * rows_per_token, rows_per_token), rows_per_token)


def _moe_expert_kernel(ea_ref, eb_ref, nv_ref, nt_ref, idx_hbm, h_hbm, wga_ref, wua_ref, wda_ref,
                       wgb_ref, wub_ref, wdb_ref, y_hbm, idx_smem, hbuf, ybuf, gsem, isem, psem, *, tm):
    i = pl.program_id(0)
    n_steps = pl.num_programs(0)
    n_tiles = nt_ref[0]
    hrows, yrows = tm * ROW_TILE, tm * 2 * ROW_TILE

    def idx_copy(tile):
        s = tile % IDX_SLOTS
        return pltpu.make_async_copy(idx_hbm.at[pl.ds(tile, 1), :], idx_smem.at[pl.ds(s, 1), :], isem.at[s])

    def pull_row(tile, r):
        tok = idx_smem[tile % IDX_SLOTS, r]
        s2 = tile % 3
        return pltpu.make_async_copy(h_hbm.at[_token_rows(tok, ROW_TILE), :],
                                     hbuf.at[pl.ds(s2 * hrows + r * ROW_TILE, ROW_TILE), :], gsem.at[s2])

    def pull_tile(tile):
        def body(r, carry):
            pull_row(tile, r).start()
            return carry
        lax.fori_loop(0, tm, body, 0, unroll=PUSH_UNROLL)

    def push_row(tile, r):
        tok = idx_smem[tile % IDX_SLOTS, r]
        s3 = tile % 3
        return pltpu.make_async_copy(ybuf.at[pl.ds(s3 * yrows + r * 2 * ROW_TILE, 2 * ROW_TILE), :],
                                     y_hbm.at[_token_rows(tok, 2 * ROW_TILE), :], psem.at[s3])

    def wait_pulls(tile):
        s2 = tile % 3
        pltpu.make_async_copy(h_hbm.at[pl.ds(0, hrows), :], hbuf.at[pl.ds(s2 * hrows, hrows), :], gsem.at[s2]).wait()

    def wait_pushes(tile):
        s3 = tile % 3
        n = nv_ref[tile] * (2 * ROW_TILE)

        @pl.when(n > 0)
        def _():
            pltpu.make_async_copy(ybuf.at[pl.ds(s3 * yrows, n), :], y_hbm.at[pl.ds(0, n), :], psem.at[s3]).wait()

    @pl.when(i == 0)
    def _():
        idx_copy(0).start()
        idx_copy(1).start()
        idx_copy(0).wait()
        idx_copy(1).wait()
        pull_tile(0)

        @pl.when(1 < n_tiles)
        def _():
            pull_tile(1)
        idx_copy(2).start()

    @pl.when(i + 3 < n_steps)
    def _():
        idx_copy(i + 3).start()

    @pl.when(i + 2 < n_steps)
    def _():
        idx_copy(i + 2).wait()

    @pl.when(i < n_tiles)
    def _():
        wait_pulls(i)

        @pl.when(i >= 3)
        def _():
            wait_pushes(i - 3)

        s3 = i % 3
        h = jnp.concatenate([hbuf[pl.ds(s3 * hrows + j, tm, stride=ROW_TILE), :] for j in range(ROW_TILE)],
                            axis=1).astype(BF16)
        pull_next = i + 2 < n_tiles
        n_push = jnp.where(i >= 1, nv_ref[jnp.maximum(i - 1, 0)], 0)

        def issue_rows(r0, r1):
            for r in range(r0, r1):
                @pl.when(pull_next)
                def _():
                    pull_row(i + 2, r).start()

                @pl.when(r < n_push)
                def _():
                    push_row(i - 1, r).start()

        def expert(wg_ref, wu_ref, wd_ref, row0):
            hid = jax.nn.silu(_dot(h, wg_ref[0])) * _dot(h, wu_ref[0])
            y = _dot(hid.astype(BF16), wd_ref[0])
            for j in range(ROW_TILE):
                ybuf[pl.ds(s3 * yrows + row0 + j, tm, stride=2 * ROW_TILE), :] = y[:, j * LANES:(j + 1) * LANES]

        issue_rows(0, tm // 2)
        expert(wga_ref, wua_ref, wda_ref, 0)
        issue_rows(tm // 2, tm)
        expert(wgb_ref, wub_ref, wdb_ref, ROW_TILE)

    @pl.when(i == n_tiles)
    def _():
        def body(r, carry):
            push_row(i - 1, r).start()
            return carry
        lax.fori_loop(0, nv_ref[i - 1], body, 0)

        for back in (3, 2, 1):
            @pl.when(i >= back)
            def _():
                wait_pushes(i - back)


def _moe_expert_call(ea, eb, nv, nt, idx, h_tiles, wg, wu, wd, n_tokens):
    n_steps, tm = idx.shape
    ne, d, f = wg.shape
    amap = lambda i, ea, eb, nv, nt: (ea[i], 0, 0)
    bmap = lambda i, ea, eb, nv, nt: (eb[i], 0, 0)
    up = lambda m: pl.BlockSpec((1, d, f), m)
    down = lambda m: pl.BlockSpec((1, f, d), m)
    hbm = pl.BlockSpec(memory_space=pl.ANY)
    grid_spec = pltpu.PrefetchScalarGridSpec(
        num_scalar_prefetch=4,
        grid=(n_steps,),
        in_specs=[hbm, hbm, up(amap), up(amap), down(amap), up(bmap), up(bmap), down(bmap)],
        out_specs=hbm,
        scratch_shapes=[pltpu.SMEM((IDX_SLOTS, tm), jnp.int32),
                        pltpu.VMEM((3 * tm * ROW_TILE, LANES), F32),
                        pltpu.VMEM((3 * tm * 2 * ROW_TILE, LANES), F32),
                        pltpu.SemaphoreType.DMA((3,)), pltpu.SemaphoreType.DMA((IDX_SLOTS,)),
                        pltpu.SemaphoreType.DMA((3,))])
    return pl.pallas_call(
        functools.partial(_moe_expert_kernel, tm=tm),
        out_shape=jax.ShapeDtypeStruct((n_tokens * 2 * ROW_TILE, LANES), F32),
        grid_spec=grid_spec,
        compiler_params=_cparams(("arbitrary",)),
        name="moe_experts",
    )(ea, eb, nv, nt, idx, h_tiles, wg, wu, wd, wg, wu, wd)


def _moe_combine_kernel(y_ref, route_ref, x1_ref, g2_ref, o_ref):
    tm = x1_ref.shape[0]
    ya = jnp.concatenate([y_ref[pl.ds(j, tm, stride=2 * ROW_TILE), :] for j in range(ROW_TILE)], axis=1)
    yb = jnp.concatenate([y_ref[pl.ds(ROW_TILE + j, tm, stride=2 * ROW_TILE), :] for j in range(ROW_TILE)], axis=1)
    route = route_ref[...]
    moe = route[:, 0:1] * ya + route[:, 1:2] * yb
    o_ref[...] = x1_ref[...] + g2_ref[0] * moe


def _moe_combine_call(y_tiles, route, x1, g2, tokens_per_batch, tm):
    t, d = x1.shape
    per_b = tokens_per_batch // tm
    tok = lambda w: pl.BlockSpec((tm, w), lambda i: (i, 0))
    return pl.pallas_call(
        _moe_combine_kernel,
        out_shape=jax.ShapeDtypeStruct((t, d), F32),
        grid=(t // tm,),
        in_specs=[pl.BlockSpec((tm * 2 * ROW_TILE, LANES), lambda i: (i, 0)), tok(LANES), tok(d),
                  pl.BlockSpec((1, 1, d), lambda i: (i // per_b, 0, 0))],
        out_specs=tok(d),
        compiler_params=_cparams(("parallel",)),
        name="moe_combine",
    )(y_tiles, route, x1, g2)


def _routing_plan(cls, tm):
    t = cls.shape[0]
    n_steps = t // tm + N_CLASSES + 1
    order = jnp.argsort(cls).astype(jnp.int32)
    classes = jnp.arange(N_CLASSES, dtype=jnp.int32)
    counts = jnp.sum((cls[:, None] == classes[None, :]).astype(jnp.int32), axis=0)
    cstart = jnp.cumsum(counts) - counts
    tiles_c = (counts + tm - 1) // tm
    tile_end = jnp.cumsum(tiles_c)
    n_tiles = tile_end[-1]
    tile_ids = jnp.arange(n_steps, dtype=jnp.int32)
    live = tile_ids < n_tiles
    c_of = jnp.sum((tile_end[None, :] <= jnp.minimum(tile_ids, n_tiles - 1)[:, None]).astype(jnp.int32), axis=1)
    k_of = jnp.minimum(tile_ids, n_tiles - 1) - (tile_end - tiles_c)[c_of]
    nv = jnp.where(live, jnp.clip(counts[c_of] - k_of * tm, 0, tm), 0).astype(jnp.int32)
    base = cstart[c_of] + k_of * tm
    idx = order[jnp.minimum(base[:, None] + jnp.arange(tm, dtype=jnp.int32)[None, :], t - 1)]
    grp, pair = c_of // PAIRS_PER_GROUP, c_of % PAIRS_PER_GROUP
    a_tab = jnp.array([a for a in range(EXPERTS_PER_GROUP) for _ in range(a + 1, EXPERTS_PER_GROUP)], jnp.int32)
    b_tab = jnp.array([b for a in range(EXPERTS_PER_GROUP) for b in range(a + 1, EXPERTS_PER_GROUP)], jnp.int32)
    ea = grp * EXPERTS_PER_GROUP + a_tab[pair]
    eb = grp * EXPERTS_PER_GROUP + b_tab[pair]
    return ea.astype(jnp.int32), eb.astype(jnp.int32), nv, n_tiles.reshape(1).astype(jnp.int32), idx


def _rope_tables(n_tokens):
    rows = n_tokens // GRID_W
    row = jnp.broadcast_to(jnp.arange(rows, dtype=F32)[:, None], (rows, GRID_W)).reshape(-1)
    col = jnp.broadcast_to(jnp.arange(GRID_W, dtype=F32)[None, :], (rows, GRID_W)).reshape(-1)
    half = HEAD_DIM // 2
    inv = ROPE_BASE ** (-jnp.arange(0, half, 2, dtype=F32) / half)
    ang = jnp.stack([row[:, None] * inv, col[:, None] * inv], axis=1)
    cos, sin = jnp.cos(ang), jnp.sin(ang)
    cos64 = jnp.concatenate([cos[:, 0], cos[:, 0], cos[:, 1], cos[:, 1]], axis=1)
    sin64 = jnp.concatenate([-sin[:, 0], sin[:, 0], -sin[:, 1], sin[:, 1]], axis=1)
    return jnp.tile(cos64, (1, LANES // HEAD_DIM)), jnp.tile(sin64, (1, LANES // HEAD_DIM))


def _pick_tile(n, target):
    t = min(n, target)
    while n % t:
        t //= 2
    return t


def kernel(x, c, ctx, c_ctx, w_ada, b_ada, norm1_g, w_in, q_norm_g, k_norm_g, lambda_q1, lambda_k1, lambda_q2, lambda_k2, subln_g, ssm_a_re, ssm_a_im, ssm_log_dt, ssm_b_re, ssm_b_im, ssm_c_re, ssm_c_im, ssm_d, w_glu, b_glu, w_out, norm2_g, w_route_group, b_route_group, w_route_expert, b_route_expert, w_exp_gate, w_exp_up, w_exp_down):
    depth = w_ada.shape[0]
    assert depth == 1, "single-layer block: the context stream is never updated"
    b, n_lat, d = x.shape
    n_ctx = ctx.shape[1]
    assert n_lat % CHUNK == 0 and n_ctx % CHUNK == 0 and n_lat % GRID_W == 0
    assert d == ROW_TILE * LANES, "MoE rows are moved as one (8, 128) tile per token"
    l = 0
    lam_init = 0.8 - 0.6 * math.exp(-0.3 * l)

    rows = b + 1
    rows_pad = -(-rows // 8) * 8
    cc = jnp.concatenate([c, c_ctx[None, :], jnp.zeros((rows_pad - rows, d), F32)], axis=0)
    mod = _mod_call(cc, w_ada[l], b_ada[l])
    sh1, sc1, g1, sh2, sc2, g2 = (mod[:b, i * d:(i + 1) * d].reshape(b, 1, d) for i in range(6))
    csh1, csc1 = (mod[b:b + 1, i * d:(i + 1) * d].reshape(1, 1, d) for i in range(2))

    w_in_bf = w_in[l].astype(BF16)
    bd = jnp.kron(jnp.eye(MXU_TILE // HEAD_DIM, dtype=F32), jnp.ones((HEAD_DIM, HEAD_DIM), F32)).astype(BF16)
    qg = jnp.tile(q_norm_g[l], LANES // HEAD_DIM).reshape(1, LANES)
    kg = jnp.tile(k_norm_g[l], LANES // HEAD_DIM).reshape(1, LANES)
    cosf, sinf = _rope_tables(n_lat)
    ones_c, zeros_c = jnp.ones((n_ctx, LANES), F32), jnp.zeros((n_ctx, LANES), F32)
    g1n = norm1_g[l].reshape(1, d)
    tm = _pick_tile(n_lat, INPROJ_TILE)
    q_x, k_x, v_x, u_x = _inproj_call(x, sh1, sc1, g1n, w_in_bf, qg, kg, cosf, sinf, bd, tm, "inproj_lat")
    _, k_c, v_c, u_c = _inproj_call(ctx, csh1, csc1, g1n, w_in_bf, qg, kg, ones_c, zeros_c, bd,
                                    _pick_tile(n_ctx, INPROJ_TILE), "inproj_ctx")

    e1 = jnp.exp(jnp.sum(lambda_q1[l] * lambda_k1[l]))
    e2 = jnp.exp(jnp.sum(lambda_q2[l] * lambda_k2[l]))
    lam_row = jnp.full((1, LANES), e1 - e2 + lam_init, F32)
    score_bound = math.sqrt(HEAD_DIM) * jnp.max(jnp.abs(q_norm_g[l])) * jnp.max(jnp.abs(k_norm_g[l]))

    def attn(bounded):
        return lambda *ops: _attn_call(*ops, **_attn_cfg(n_lat), out_scale=1.0 - lam_init, bounded=bounded)

    a_x = lax.cond(score_bound <= SCORE_BOUND, attn(True), attn(False),
                   lam_row, q_x, k_x, v_x, k_c, v_c, subln_g[l].reshape(1, LANES))

    win, m_op, wo, lam16 = _s5_weights(ssm_a_re[l], ssm_a_im[l], ssm_log_dt[l], ssm_b_re[l], ssm_b_im[l],
                                       ssm_c_re[l], ssm_c_im[l], ssm_d[l])
    n_pairs, kw = u_x.shape[0], u_x.shape[3]
    yg = _s5_call(u_x.reshape(n_pairs, b * (n_lat // CHUNK), kw), u_c.reshape(n_pairs, b * (n_ctx // CHUNK), kw),
                  win, m_op, wo, lam16, b)
    yg = yg.reshape(n_pairs, b, n_lat // CHUNK, kw)

    wr = jnp.concatenate([w_route_group[l], w_route_expert[l]], axis=1)
    wr = jnp.pad(wr, ((0, 0), (0, LANES - wr.shape[1])))
    wr_hi, wr_lo = _split_bf16(wr)
    br = jnp.pad(jnp.concatenate([b_route_group[l], b_route_expert[l]]), (0, LANES - MOE_GROUPS - N_EXPERTS))
    x1, h2, route = _outproj_call(a_x, yg, x, g1, sh2, sc2, norm2_g[l].reshape(1, d),
                                 w_glu[l].astype(BF16), b_glu[l].reshape(1, -1), w_out[l].astype(BF16),
                                 wr_hi, wr_lo, br.reshape(1, LANES), _pick_tile(n_lat, OUTPROJ_TILE))

    t_all = b * n_lat
    route = route.reshape(t_all, LANES)
    ea, eb, nv, n_tiles, idx = _routing_plan(route[:, 2].astype(jnp.int32), _pick_tile(t_all, MOE_TILE))
    y_tiles = _moe_expert_call(ea, eb, nv, n_tiles, idx, h2, w_exp_gate[l].astype(BF16),
                               w_exp_up[l].astype(BF16), w_exp_down[l].astype(BF16), t_all)
    out = _moe_combine_call(y_tiles, route, x1.reshape(t_all, d), g2, n_lat, _pick_tile(n_lat, COMBINE_TILE))
    return out.reshape(b, n_lat, d)
```

```python
import functools
import math

import jax
import jax.numpy as jnp
from jax import lax
from jax.experimental import pallas as pl
from jax.experimental.pallas import tpu as pltpu

F32 = jnp.float32
BF16 = jnp.bfloat16

LANES = 128
MXU_TILE = 256
HEADS = 4
HEAD_DIM = 64
QK_WIDTH = HEADS * 2 * HEAD_DIM
V_WIDTH = HEADS * 2 * HEAD_DIM
GRID_W = 64
ROPE_BASE = 10000.0
SSM_GROUP = 16
SSM_STATE = 64
CHUNK = 16
MOE_GROUPS = 4
EXPERTS_PER_GROUP = 8
N_EXPERTS = MOE_GROUPS * EXPERTS_PER_GROUP
RMS_EPS = 1e-6
INPROJ_TILE = 1024
OUTPROJ_TILE = 1024
ATTN_TQ = 512
ATTN_TK = 4096
SCORE_BOUND = 60.0
SCAN_UNROLL = 8
S5_BATCH_BLOCK = 8
PAIRS_PER_GROUP = EXPERTS_PER_GROUP * (EXPERTS_PER_GROUP - 1) // 2
N_CLASSES = MOE_GROUPS * PAIRS_PER_GROUP
ROW_TILE = 8
MOE_TILE = 256
PUSH_UNROLL = 8
IDX_SLOTS = 5
COMBINE_TILE = 512
VMEM_LIMIT = 48 * 1024 * 1024


def _cparams(sem):
    return pltpu.CompilerParams(dimension_semantics=sem, vmem_limit_bytes=VMEM_LIMIT)


def _split_bf16(a):
    hi = a.astype(BF16)
    lo = (a - hi.astype(F32)).astype(BF16)
    return hi, lo


def _dot(a, b):
    return jnp.dot(a, b, preferred_element_type=F32)


def _dot3(a, b):
    a_hi, a_lo = _split_bf16(a)
    b_hi, b_lo = _split_bf16(b)
    return _dot(a_hi, b_hi) + (_dot(a_hi, b_lo) + _dot(a_lo, b_hi))


def _mod_kernel(c_ref, w_ref, b_ref, o_ref):
    c = c_ref[...]
    a = c * jax.nn.sigmoid(c)
    o_ref[...] = _dot3(a, w_ref[...]) + b_ref[...]


def _mod_call(cc, w_ada, b_ada):
    rows, d = cc.shape
    n = w_ada.shape[1]
    bn = 1024
    return pl.pallas_call(
        _mod_kernel,
        out_shape=jax.ShapeDtypeStruct((rows, n), F32),
        grid=(n // bn,),
        in_specs=[pl.BlockSpec((rows, d), lambda j: (0, 0)),
                  pl.BlockSpec((d, bn), lambda j: (0, j)),
                  pl.BlockSpec((1, bn), lambda j: (0, j))],
        out_specs=pl.BlockSpec((rows, bn), lambda j: (0, j)),
        compiler_params=_cparams(("arbitrary",)),
        name="mod",
    )(cc, w_ada, b_ada.reshape(1, n))


def _inproj_kernel(x_ref, sh_ref, sc_ref, g_ref, w_ref, qg_ref, kg_ref, cos_ref, sin_ref, bd_ref, perm_ref,
                   q_ref, k_ref, v_ref, u_ref):
    x = x_ref[0]
    ms = jnp.mean(x * x, axis=-1, keepdims=True)
    h = (x * lax.rsqrt(ms + RMS_EPS)) * g_ref[...]
    h = h * (1.0 + sc_ref[0]) + sh_ref[0]
    p = _dot(h.astype(BF16), w_ref[...])

    cosf = cos_ref[...]
    sinf = sin_ref[...]
    lane = lax.broadcasted_iota(jnp.int32, cosf.shape, 1)
    first_half = (lane % 32) < 16

    def norm_rope(t, gain, scale):
        sq = (t * t).astype(BF16)
        half = bd_ref.shape[0]
        ss = jnp.concatenate([_dot(sq[:, c:c + half], bd_ref[...]) for c in range(0, QK_WIDTH, half)], axis=1)
        t = t * lax.rsqrt(ss * (1.0 / HEAD_DIM) + RMS_EPS)
        outs = []
        for s in range(QK_WIDTH // LANES):
            ts = t[:, s * LANES:(s + 1) * LANES] * gain
            partner = jnp.where(first_half, pltpu.roll(ts, LANES - 16, 1), pltpu.roll(ts, 16, 1))
            outs.append(((ts * cosf + partner * sinf) * scale).astype(BF16))
        return jnp.concatenate(outs, axis=1)

    q_ref[0] = norm_rope(p[:, :QK_WIDTH], qg_ref[...], HEAD_DIM ** -0.5 * math.log2(math.e))
    k_ref[0] = norm_rope(p[:, QK_WIDTH:2 * QK_WIDTH], kg_ref[...], 1.0)
    v_ref[0] = p[:, 2 * QK_WIDTH:2 * QK_WIDTH + V_WIDTH].astype(BF16)
    u_t = _dot(perm_ref[...], p[:, 2 * QK_WIDTH + V_WIDTH:].astype(BF16))

    def store_u(pr, k, val):
        u_ref[pr, 0, :, k * LANES:(k + 1) * LANES] = val.astype(BF16)

    _step_major_to_chunk_rows(u_t, store_u)


PAIR_W = 2 * SSM_GROUP
PAIRS_PER_TILE = LANES // PAIR_W


def _quarter_select(pieces):
    lane = lax.broadcasted_iota(jnp.int32, pieces[0].shape, 1)
    acc = pieces[0]
    for r in range(1, len(pieces)):
        acc = jnp.where(lane // PAIR_W == r, pieces[r], acc)
    return acc


def _step_perm(tm):
    n_chunk = tm // CHUNK
    src = (jnp.arange(tm) % n_chunk) * CHUNK + jnp.arange(tm) // n_chunk
    return (src[:, None] == jnp.arange(tm)[None, :]).astype(BF16)


def _step_major_to_chunk_rows(u_t, store):
    n_chunk = u_t.shape[0] // CHUNK
    for k in range(CHUNK // PAIRS_PER_TILE):
        for pr in range(u_t.shape[1] // PAIR_W):
            q, r_src = divmod(pr, PAIRS_PER_TILE)
            pieces = []
            for r in range(PAIRS_PER_TILE):
                t = PAIRS_PER_TILE * k + r
                src = u_t[t * n_chunk:(t + 1) * n_chunk, q * LANES:(q + 1) * LANES]
                shift = ((r - r_src) % PAIRS_PER_TILE) * PAIR_W
                pieces.append(pltpu.roll(src, shift, 1) if shift else src)
            store(pr, k, _quarter_select(pieces))


def _chunk_rows_to_token_major(load, ys_ref):
    n_tiles, n_chunk = ys_ref.shape[0], ys_ref.shape[1] // CHUNK
    for t in range(CHUNK):
        k, r_src = divmod(t, PAIRS_PER_TILE)
        for q in range(n_tiles):
            pieces = []
            for r in range(PAIRS_PER_TILE):
                shift = ((r - r_src) % PAIRS_PER_TILE) * PAIR_W
                src = load(PAIRS_PER_TILE * q + r, k)
                pieces.append(pltpu.roll(src, shift, 1) if shift else src)
            ys_ref[q, pl.ds(t, n_chunk, stride=CHUNK), :] = _quarter_select(pieces)


def _inproj_call(x, sh, sc, g, w_bf, qg, kg, cosf, sinf, bd, tm, name):
    b, n, d = x.shape
    wn = w_bf.shape[1]
    per_batch = sh.shape[0] > 1
    mod_map = (lambda bi, i: (bi, 0, 0)) if per_batch else (lambda bi, i: (0, 0, 0))
    const2 = lambda bi, i: (0, 0)
    n_pairs = QK_WIDTH // PAIR_W
    outs = [jax.ShapeDtypeStruct((b, n, QK_WIDTH), BF16)] * 3 + [
        jax.ShapeDtypeStruct((n_pairs, b, n // CHUNK, CHUNK * PAIR_W), BF16)]
    tok_spec = pl.BlockSpec((1, tm, QK_WIDTH), lambda bi, i: (bi, i, 0))
    u_spec = pl.BlockSpec((n_pairs, 1, tm // CHUNK, CHUNK * PAIR_W), lambda bi, i: (0, bi, i, 0))
    return pl.pallas_call(
        _inproj_kernel,
        out_shape=outs,
        grid=(b, n // tm),
        in_specs=[pl.BlockSpec((1, tm, d), lambda bi, i: (bi, i, 0)),
                  pl.BlockSpec((1, 1, d), mod_map),
                  pl.BlockSpec((1, 1, d), mod_map),
                  pl.BlockSpec((1, d), const2),
                  pl.BlockSpec((d, wn), const2),
                  pl.BlockSpec((1, LANES), const2),
                  pl.BlockSpec((1, LANES), const2),
                  pl.BlockSpec((tm, LANES), lambda bi, i: (i, 0)),
                  pl.BlockSpec((tm, LANES), lambda bi, i: (i, 0)),
                  pl.BlockSpec(bd.shape, const2),
                  pl.BlockSpec((tm, tm), const2)],
        out_specs=[tok_spec] * 3 + [u_spec],
        compiler_params=_cparams(("parallel", "parallel")),
        name=name,
    )(x, sh, sc, g, w_bf, qg, kg, cosf, sinf, bd, _step_perm(tm))


def _attn_kernel(lam_ref, q_ref, kl_ref, vl_ref, kc_ref, vc_ref, sg_ref, o_ref, a1_ref, a2_ref, m1_ref, m2_ref,
                 *, tk, out_scale, bounded):
    q = q_ref[0]
    lane = lax.broadcasted_iota(jnp.int32, q.shape, 1)
    zero = jnp.zeros_like(q)
    qa = jnp.where(lane < HEAD_DIM, q, zero)
    qb = jnp.where(lane >= HEAD_DIM, q, zero)

    a1_ref[...] = jnp.zeros(a1_ref.shape, F32)
    a2_ref[...] = jnp.zeros(a2_ref.shape, F32)
    if not bounded:
        m1_ref[...] = jnp.full(m1_ref.shape, -jnp.inf, F32)
        m2_ref[...] = jnp.full(m2_ref.shape, -jnp.inf, F32)

    def ones_col(rows):
        col = lax.broadcasted_iota(jnp.int32, (rows, LANES), 1)
        return jnp.where(col == 0, 1.0, 0.0).astype(BF16)

    def one_map(qm, kc, va, a_ref, m_ref):
        s = lax.dot_general(qm, kc, (((1,), (1,)), ((), ())), preferred_element_type=F32)
        if bounded:
            a_ref[...] += _dot(jnp.exp2(s).astype(BF16), va)
        else:
            m_prev = m_ref[...]
            m_next = jnp.maximum(m_prev, jnp.max(s, axis=1, keepdims=True))
            p = jnp.exp2(s - m_next[:, :1])
            alpha = jnp.exp2(m_prev - m_next)
            a_ref[...] = jnp.concatenate([alpha, alpha], axis=1) * a_ref[...] + _dot(p.astype(BF16), va)
            m_ref[...] = m_next

    def step(kc, vc, ones):
        va = jnp.concatenate([vc, ones], axis=1)
        one_map(qa, kc, va, a1_ref, m1_ref)
        one_map(qb, kc, va, a2_ref, m2_ref)

    ones_lat = ones_col(tk)

    def lat_body(j, carry):
        off = pl.multiple_of(j * tk, tk)
        step(kl_ref[0, pl.ds(off, tk), :], vl_ref[0, pl.ds(off, tk), :], ones_lat)
        return carry

    lax.fori_loop(0, kl_ref.shape[1] // tk, lat_body, 0)
    step(kc_ref[0], vc_ref[0], ones_col(kc_ref.shape[1]))

    lam = lam_ref[...]
    a1, a2 = a1_ref[...], a2_ref[...]
    o = a1[:, :LANES] / a1[:, LANES:LANES + 1] - lam * (a2[:, :LANES] / a2[:, LANES:LANES + 1])
    ms = jnp.mean(o * o, axis=-1, keepdims=True)
    o = o * lax.rsqrt(ms + RMS_EPS) * sg_ref[...]
    o_ref[0] = (o * out_scale).astype(BF16)


def _attn_cfg(n_lat):
    return dict(tq=_pick_tile(n_lat, ATTN_TQ), tk=_pick_tile(n_lat, ATTN_TK))


def _attn_call(lam_row, q, k_lat, v_lat, k_ctx, v_ctx, sg, tq, tk, out_scale=1.0, bounded=True):
    b, n, _ = q.shape
    nc = k_ctx.shape[1]
    kv_lat = pl.BlockSpec((1, n, LANES), lambda bi, h, i: (bi, 0, h))
    kv_ctx = pl.BlockSpec((1, nc, LANES), lambda bi, h, i: (bi, 0, h))
    q_spec = pl.BlockSpec((1, tq, LANES), lambda bi, h, i: (bi, i, h))
    row = pl.BlockSpec((1, LANES), lambda bi, h, i: (0, 0))
    acc = pltpu.VMEM((tq, 2 * LANES), F32)
    run_max = pltpu.VMEM((tq, LANES), F32)
    return pl.pallas_call(
        functools.partial(_attn_kernel, tk=tk, out_scale=out_scale, bounded=bounded),
        out_shape=jax.ShapeDtypeStruct((b, n, V_WIDTH), BF16),
        grid=(b, HEADS, n // tq),
        in_specs=[row, q_spec, kv_lat, kv_lat, kv_ctx, kv_ctx, row],
        out_specs=q_spec,
        scratch_shapes=[acc, acc, run_max, run_max],
        compiler_params=_cparams(("parallel", "parallel", "arbitrary")),
        name="attn" if bounded else "attn_general",
    )(lam_row, q, k_lat, v_lat, k_ctx, v_ctx, sg)


def _s5_kernel(ul_ref, uc_ref, win_ref, m_ref, wo_ref, lam_ref, o_ref, xl_ref, xc_ref, s_ref, *, nb):
    ul = ul_ref[0]
    n_slab = xl_ref.shape[0]

    def to_slabs(x_ref, x):
        for k in range(n_slab):
            x_ref[k] = x[:, k * LANES:(k + 1) * LANES]

    to_slabs(xl_ref, _dot(ul, win_ref[0]))
    to_slabs(xc_ref, _dot(uc_ref[0], win_ref[0]))
    n_lat = ul.shape[0] // nb
    n_ctx = uc_ref.shape[1] // nb

    lam = lam_ref[0]
    lfr, lfi, lbr, lbi = (jnp.broadcast_to(lam[i:i + 1], (nb, LANES)) for i in range(4))

    def rows(c, n_chunks):
        return pl.ds(c, nb, stride=n_chunks)

    def advance(x_ref, c, n_chunks, slab, ar, ai, sr, si):
        xr = x_ref[slab, rows(c, n_chunks), :]
        xi = x_ref[slab + 1, rows(c, n_chunks), :]
        return ar * sr - ai * si + xr, ar * si + ai * sr + xi

    def ctx_body(i, carry):
        fr, fi, br, bi = carry
        fr, fi = advance(xc_ref, i, n_ctx, 0, lfr, lfi, fr, fi)
        br, bi = advance(xc_ref, n_ctx - 1 - i, n_ctx, 2, lbr, lbi, br, bi)
        return fr, fi, br, bi

    def lat_body(i, carry):
        fr, fi, br, bi = carry
        cb = n_lat - 1 - i
        s_ref[0, pl.ds(pl.multiple_of(i * nb, nb), nb), :] = fr
        s_ref[1, pl.ds(pl.multiple_of(i * nb, nb), nb), :] = fi
        s_ref[2, pl.ds(pl.multiple_of(cb * nb, nb), nb), :] = br
        s_ref[3, pl.ds(pl.multiple_of(cb * nb, nb), nb), :] = bi
        fr, fi = advance(xl_ref, i, n_lat, 0, lfr, lfi, fr, fi)
        br, bi = advance(xl_ref, cb, n_lat, 2, lbr, lbi, br, bi)
        return fr, fi, br, bi

    z = jnp.zeros((nb, LANES), F32)
    carry = lax.fori_loop(0, n_ctx, ctx_body, (z, z, z, z), unroll=SCAN_UNROLL)
    lax.fori_loop(0, n_lat, lat_body, carry, unroll=SCAN_UNROLL)

    def batch_major(k):
        return jnp.concatenate([s_ref[k, pl.ds(bi, n_lat, stride=nb), :] for bi in range(nb)], axis=0)

    s_in = jnp.concatenate([batch_major(k).astype(BF16) for k in range(n_slab)], axis=1)
    y = _dot(ul, m_ref[0]) + _dot(s_in, wo_ref[0])
    o_ref[0] = jax.nn.gelu(y).astype(BF16)


def _s5_call(u_lat, u_ctx, win, m, wo, lam16, b):
    npair, rows_lat, kw = u_lat.shape
    nb = S5_BATCH_BLOCK if b % S5_BATCH_BLOCK == 0 else b
    rl = rows_lat // b * nb
    rc = u_ctx.shape[1] // b * nb
    wspec = pl.BlockSpec((1, kw, kw), lambda p, h: (p, 0, 0))
    slabs = lambda r: pltpu.VMEM((kw // LANES, r, LANES), F32)
    return pl.pallas_call(
        functools.partial(_s5_kernel, nb=nb),
        out_shape=jax.ShapeDtypeStruct((npair, rows_lat, kw), BF16),
        grid=(npair, b // nb),
        in_specs=[pl.BlockSpec((1, rl, kw), lambda p, h: (p, h, 0)),
                  pl.BlockSpec((1, rc, kw), lambda p, h: (p, h, 0)),
                  wspec, wspec, wspec,
                  pl.BlockSpec((1, 4, LANES), lambda p, h: (p, 0, 0))],
        out_specs=pl.BlockSpec((1, rl, kw), lambda p, h: (p, h, 0)),
        scratch_shapes=[slabs(rl), slabs(rc), slabs(rl)],
        compiler_params=_cparams(("parallel", "parallel")),
        name="s5",
    )(u_lat, u_ctx, win, m, wo, lam16)


def _s5_weights(a_re, a_im, log_dt, b_re, b_im, c_re, c_im, d_skip):
    hp = lax.Precision.HIGHEST
    g_n, p_n = a_re.shape[1], a_re.shape[2]
    t_n = CHUNK
    dt = jnp.exp(log_dt.astype(F32))[..., None]
    ar, ai = a_re.astype(F32), a_im.astype(F32)
    mag = jnp.exp(ar * dt)
    lr, li = mag * jnp.cos(ai * dt), mag * jnp.sin(ai * dt)
    den = ar * ar + ai * ai
    nr, ni = lr - 1.0, li
    cr = (nr * ar + ni * ai) / den
    ci = (ni * ar - nr * ai) / den
    bbr = cr[..., None] * b_re - ci[..., None] * b_im
    bbi = cr[..., None] * b_im + ci[..., None] * b_re
    n = jnp.arange(t_n + 1, dtype=F32)[:, None, None, None]
    pm = jnp.exp(n * (ar * dt))
    pw_r, pw_i = pm * jnp.cos(n * (ai * dt)), pm * jnp.sin(n * (ai * dt))
    lb_r = pw_r[:t_n, ..., None] * bbr - pw_i[:t_n, ..., None] * bbi
    lb_i = pw_r[:t_n, ..., None] * bbi + pw_i[:t_n, ..., None] * bbr
    cre, cim = c_re.astype(F32), c_im.astype(F32)
    kern = (jnp.einsum('gip,tdgpj->tdgij', cre, lb_r, precision=hp)
            - jnp.einsum('gip,tdgpj->tdgij', cim, lb_i, precision=hp))
    hh = SSM_GROUP
    npair = g_n // 2
    eye2 = jnp.eye(2, dtype=F32)
    kw = 2 * t_n * hh

    lag0 = kern[0, 0] + kern[0, 1] + jnp.eye(hh, dtype=F32)[None] * d_skip.astype(F32)[:, :, None]
    by_lag = jnp.concatenate([kern[:0:-1, 1], lag0[None], kern[1:, 0]], axis=0)
    strip = jnp.einsum('ab,pajlq->pajlbq', eye2,
                       jnp.transpose(by_lag, (1, 3, 0, 2)).reshape(npair, 2, hh, 2 * t_n - 1, hh))
    strip = strip.reshape(npair, 2 * hh, (2 * t_n - 1) * 2 * hh)
    m_pair = jnp.stack([strip[:, :, (t_n - 1 - s) * 2 * hh:(t_n - 1 - s) * 2 * hh + kw] for s in range(t_n)],
                       axis=1).reshape(npair, kw, kw)

    def state_lanes(parts, mask_pair):
        x = jnp.stack(parts, axis=2).reshape(npair, 2, parts[0].shape[1], 4, 1, p_n)
        sel = eye2[None, :, None, None, :, None] if mask_pair else jnp.ones((1, 1, 1, 1, 2, 1), F32)
        return (x * sel).reshape(npair, 2, parts[0].shape[1], 8 * p_n)

    gsp = lambda z: jnp.transpose(z, (1, 0, 2))
    pf_r, pf_i, pb_r, pb_i = pw_r[:, 0], pw_i[:, 0], pw_r[:, 1], pw_i[:, 1]
    pa = state_lanes([gsp(pf_r[t_n - 1::-1]), gsp(pf_r[t_n - 1::-1]), gsp(pb_r[:t_n]), gsp(pb_r[:t_n])], True)
    pb = state_lanes([-gsp(pf_i[t_n - 1::-1]), gsp(pf_i[t_n - 1::-1]), -gsp(pb_i[:t_n]), gsp(pb_i[:t_n])], True)
    bt = lambda z: jnp.transpose(z, (0, 2, 1))
    ba = state_lanes([bt(bbr[0]), bt(bbi[0]), bt(bbr[1]), bt(bbi[1])], False)
    bb = state_lanes([bt(bbi[0]), bt(bbr[0]), bt(bbi[1]), bt(bbr[1])], False)
    sa = lambda z: jnp.transpose(z, (0, 2, 1, 3))
    win_pair = (sa(pa)[:, :, :, None, :] * ba[:, None] + sa(pb)[:, :, :, None, :] * bb[:, None]
                ).reshape(npair, kw, 8 * p_n)
    qa = state_lanes([gsp(pf_r[1:]), -gsp(pf_i[1:]), gsp(pb_r[t_n:0:-1]), -gsp(pb_i[t_n:0:-1])], True)
    qb = state_lanes([-gsp(pf_i[1:]), -gsp(pf_r[1:]), -gsp(pb_i[t_n:0:-1]), -gsp(pb_r[t_n:0:-1])], True)
    ca = state_lanes([cre] * 4, False)
    cb = state_lanes([cim] * 4, False)
    wo_t = (sa(qa)[:, :, :, None, :] * ca[:, None] + sa(qb)[:, :, :, None, :] * cb[:, None]
            ).reshape(npair, kw, 8 * p_n)
    wo_pair = jnp.swapaxes(wo_t.astype(BF16), 1, 2)
    lam16 = jnp.stack([pw_r[t_n, 0], pw_i[t_n, 0], pw_r[t_n, 1], pw_i[t_n, 1]], axis=1)
    lam16 = jnp.transpose(lam16.reshape(npair, 2, 4, p_n), (0, 2, 1, 3)).reshape(npair, 4, 2 * p_n)
    return win_pair.astype(BF16), m_pair.astype(BF16), wo_pair, lam16


def _outproj_kernel(a_ref, y_ref, x_ref, g1_ref, sh_ref, sc_ref, n2_ref, wglu_ref, bglu_ref, wout_ref,
                    wr_hi_ref, wr_lo_ref, br_ref, x1_ref, h2_ref, gate_ref, ys_ref):
    _chunk_rows_to_token_major(
        lambda pr, k: y_ref[pr, 0, :, k * LANES:(k + 1) * LANES].astype(F32), ys_ref)
    yf = jnp.concatenate([ys_ref[q] for q in range(ys_ref.shape[0])], axis=1)
    y = yf.astype(BF16)
    z = _dot(y, wglu_ref[...]) + bglu_ref[...]
    s = (yf * jax.nn.sigmoid(z)).astype(BF16)
    o = _dot(jnp.concatenate([a_ref[0], s], axis=1), wout_ref[...])
    x1 = x_ref[0] + g1_ref[0] * o
    x1_ref[0] = x1

    ms = jnp.mean(x1 * x1, axis=-1, keepdims=True)
    h = (x1 * lax.rsqrt(ms + RMS_EPS)) * n2_ref[...]
    h = h * (1.0 + sc_ref[0]) + sh_ref[0]
    for j in range(h.shape[1] // LANES):
        h2_ref[pl.ds(j, h.shape[0], stride=ROW_TILE), :] = h[:, j * LANES:(j + 1) * LANES]

    h_hi, h_lo = _split_bf16(h)
    hi_both = _dot(h_hi, jnp.concatenate([wr_hi_ref[...], wr_lo_ref[...]], axis=1))
    lg = hi_both[:, :LANES] + (hi_both[:, LANES:] + _dot(h_lo, wr_hi_ref[...])) + br_ref[...]
    lane = lax.broadcasted_iota(jnp.int32, lg.shape, 1)
    neg = jnp.float32(-jnp.inf)
    lane_f = lane.astype(F32)

    def top1(vals):
        vmax = jnp.max(vals, axis=1, keepdims=True)
        idx = jnp.min(jnp.where(vals == vmax, lane_f, float(LANES)), axis=1, keepdims=True)
        return vmax, idx.astype(jnp.int32)

    is_grp = lane < MOE_GROUPS
    g_vals = jnp.where(is_grp, lg, neg)
    g_max, g_idx = top1(g_vals)
    p_grp = 1.0 / jnp.sum(jnp.where(is_grp, jnp.exp(g_vals - g_max), 0.0), axis=1, keepdims=True)
    e_lo = MOE_GROUPS + EXPERTS_PER_GROUP * g_idx
    in_grp = (lane >= e_lo) & (lane < e_lo + EXPERTS_PER_GROUP)
    e_vals = jnp.where(in_grp, lg, neg)
    v1, i1 = top1(e_vals)
    v2, i2 = top1(jnp.where(lane == i1, neg, e_vals))
    r = jnp.exp(v2 - v1)
    w1 = p_grp / (1.0 + r)
    w2 = w1 * r
    first_lo = i1 < i2
    a_loc = jnp.where(first_lo, i1, i2) - e_lo
    b_loc = jnp.where(first_lo, i2, i1) - e_lo
    pair = ((a_loc * (2 * EXPERTS_PER_GROUP - 1 - a_loc)) >> 1) + (b_loc - a_loc - 1)
    cls = g_idx * PAIRS_PER_GROUP + pair
    w_lo = jnp.where(first_lo, w1, w2)
    w_hi = jnp.where(first_lo, w2, w1)
    gate_ref[0] = (jnp.where(lane == 0, w_lo, 0.0) + jnp.where(lane == 1, w_hi, 0.0)
                   + jnp.where(lane == 2, cls.astype(F32), 0.0))


def _outproj_call(a, yg, x, g1, sh2, sc2, n2g, wglu, bglu, wout, wr_hi, wr_lo, br, tm):
    b, n, d = x.shape
    half = a.shape[2]
    tok = lambda w: pl.BlockSpec((1, tm, w), lambda bi, i: (bi, i, 0))
    mod = pl.BlockSpec((1, 1, d), lambda bi, i: (bi, 0, 0))
    const = lambda r, c: pl.BlockSpec((r, c), lambda bi, i: (0, 0))
    return pl.pallas_call(
        _outproj_kernel,
        out_shape=[jax.ShapeDtypeStruct((b, n, d), F32), jax.ShapeDtypeStruct((b * n * ROW_TILE, LANES), F32),
                   jax.ShapeDtypeStruct((b, n, LANES), F32)],
        grid=(b, n // tm),
        in_specs=[tok(half),
                  pl.BlockSpec((yg.shape[0], 1, tm // CHUNK, yg.shape[3]), lambda bi, i: (0, bi, i, 0)),
                  tok(d), mod, mod, mod, const(1, d),
                  const(half, half), const(1, half), const(d, d),
                  const(d, LANES), const(d, LANES), const(1, LANES)],
        out_specs=[tok(d), pl.BlockSpec((tm * ROW_TILE, LANES), lambda bi, i: (bi * (n // tm) + i, 0)), tok(LANES)],
        scratch_shapes=[pltpu.VMEM((half // LANES, tm, LANES), F32)],
        compiler_params=_cparams(("parallel", "parallel")),
        name="outproj",
    )(a, yg, x, g1, sh2, sc2, n2g, wglu, bglu, wout, wr_hi, wr_lo, br)


def _token_rows(tok, rows_per_token):
    return pl.ds(pl.multiple_of(tok * rows_per_token, rows_per_token), rows_per_token)


def _moe_expert_kernel(ea_ref, eb_ref, nv_ref, nt_ref, idx_hbm, h_hbm, wga_ref, wua_ref, wda_ref,
                       wgb_ref, wub_ref, wdb_ref, y_hbm, idx_smem, hbuf, ybuf, gsem, isem, psem, *, tm):
    i = pl.program_id(0)
    n_steps = pl.num_programs(0)
    n_tiles = nt_ref[0]
    hrows, yrows = tm * ROW_TILE, tm * 2 * ROW_TILE

    def idx_copy(tile):
        s = tile % IDX_SLOTS
        return pltpu.make_async_copy(idx_hbm.at[pl.ds(tile, 1), :], idx_smem.at[pl.ds(s, 1), :], isem.at[s])

    def pull_row(tile, r):
        tok = idx_smem[tile % IDX_SLOTS, r]
        s2 = tile % 3
        return pltpu.make_async_copy(h_hbm.at[_token_rows(tok, ROW_TILE), :],
                                     hbuf.at[pl.ds(s2 * hrows + r * ROW_TILE, ROW_TILE), :], gsem.at[s2])

    def pull_tile(tile):
        def body(r, carry):
            pull_row(tile, r).start()
            return carry
        lax.fori_loop(0, tm, body, 0, unroll=PUSH_UNROLL)

    def push_row(tile, r):
        tok = idx_smem[tile % IDX_SLOTS, r]
        s3 = tile % 3
        return pltpu.make_async_copy(ybuf.at[pl.ds(s3 * yrows + r * 2 * ROW_TILE, 2 * ROW_TILE), :],
                                     y_hbm.at[_token_rows(tok, 2 * ROW_TILE), :], psem.at[s3])

    def wait_pulls(tile):
        s2 = tile % 3
        pltpu.make_async_copy(h_hbm.at[pl.ds(0, hrows), :], hbuf.at[pl.ds(s2 * hrows, hrows), :], gsem.at[s2]).wait()

    def wait_pushes(tile):
        s3 = tile % 3
        n = nv_ref[tile] * (2 * ROW_TILE)

        @pl.when(n > 0)
        def _():
            pltpu.make_async_copy(ybuf.at[pl.ds(s3 * yrows, n), :], y_hbm.at[pl.ds(0, n), :], psem.at[s3]).wait()

    @pl.when(i == 0)
    def _():
        idx_copy(0).start()
        idx_copy(1).start()
        idx_copy(0).wait()
        idx_copy(1).wait()
        pull_tile(0)

        @pl.when(1 < n_tiles)
        def _():
            pull_tile(1)
        idx_copy(2).start()

    @pl.when(i + 3 < n_steps)
    def _():
        idx_copy(i + 3).start()

    @pl.when(i + 2 < n_steps)
    def _():
        idx_copy(i + 2).wait()

    @pl.when(i < n_tiles)
    def _():
        wait_pulls(i)

        @pl.when(i >= 3)
        def _():
            wait_pushes(i - 3)

        s3 = i % 3
        h = jnp.concatenate([hbuf[pl.ds(s3 * hrows + j, tm, stride=ROW_TILE), :] for j in range(ROW_TILE)],
                            axis=1).astype(BF16)
        pull_next = i + 2 < n_tiles
        n_push = jnp.where(i >= 1, nv_ref[jnp.maximum(i - 1, 0)], 0)

        def issue_rows(r0, r1):
            for r in range(r0, r1):
                @pl.when(pull_next)
                def _():
                    pull_row(i + 2, r).start()

                @pl.when(r < n_push)
                def _():
                    push_row(i - 1, r).start()

        def expert(wg_ref, wu_ref, wd_ref, row0):
            hid = jax.nn.silu(_dot(h, wg_ref[0])) * _dot(h, wu_ref[0])
            y = _dot(hid.astype(BF16), wd_ref[0])
            for j in range(ROW_TILE):
                ybuf[pl.ds(s3 * yrows + row0 + j, tm, stride=2 * ROW_TILE), :] = y[:, j * LANES:(j + 1) * LANES]

        issue_rows(0, tm // 2)
        expert(wga_ref, wua_ref, wda_ref, 0)
        issue_rows(tm // 2, tm)
        expert(wgb_ref, wub_ref, wdb_ref, ROW_TILE)

    @pl.when(i == n_tiles)
    def _():
        def body(r, carry):
            push_row(i - 1, r).start()
            return carry
        lax.fori_loop(0, nv_ref[i - 1], body, 0)

        for back in (3, 2, 1):
            @pl.when(i >= back)
            def _():
                wait_pushes(i - back)


def _moe_expert_call(ea, eb, nv, nt, idx, h_tiles, wg, wu, wd, n_tokens):
    n_steps, tm = idx.shape
    ne, d, f = wg.shape
    amap = lambda i, ea, eb, nv, nt: (ea[i], 0, 0)
    bmap = lambda i, ea, eb, nv, nt: (eb[i], 0, 0)
    up = lambda m: pl.BlockSpec((1, d, f), m)
    down = lambda m: pl.BlockSpec((1, f, d), m)
    hbm = pl.BlockSpec(memory_space=pl.ANY)
    grid_spec = pltpu.PrefetchScalarGridSpec(
        num_scalar_prefetch=4,
        grid=(n_steps,),
        in_specs=[hbm, hbm, up(amap), up(amap), down(amap), up(bmap), up(bmap), down(bmap)],
        out_specs=hbm,
        scratch_shapes=[pltpu.SMEM((IDX_SLOTS, tm), jnp.int32),
                        pltpu.VMEM((3 * tm * ROW_TILE, LANES), F32),
                        pltpu.VMEM((3 * tm * 2 * ROW_TILE, LANES), F32),
                        pltpu.SemaphoreType.DMA((3,)), pltpu.SemaphoreType.DMA((IDX_SLOTS,)),
                        pltpu.SemaphoreType.DMA((3,))])
    return pl.pallas_call(
        functools.partial(_moe_expert_kernel, tm=tm),
        out_shape=jax.ShapeDtypeStruct((n_tokens * 2 * ROW_TILE, LANES), F32),
        grid_spec=grid_spec,
        compiler_params=_cparams(("arbitrary",)),
        name="moe_experts",
    )(ea, eb, nv, nt, idx, h_tiles, wg, wu, wd, wg, wu, wd)


def _moe_combine_kernel(y_ref, route_ref, x1_ref, g2_ref, o_ref):
    tm = x1_ref.shape[0]
    ya = jnp.concatenate([y_ref[pl.ds(j, tm, stride=2 * ROW_TILE), :] for j in range(ROW_TILE)], axis=1)
    yb = jnp.concatenate([y_ref[pl.ds(ROW_TILE + j, tm, stride=2 * ROW_TILE), :] for j in range(ROW_TILE)], axis=1)
    route = route_ref[...]
    moe = route[:, 0:1] * ya + route[:, 1:2] * yb
    o_ref[...] = x1_ref[...] + g2_ref[0] * moe


def _moe_combine_call(y_tiles, route, x1, g2, tokens_per_batch, tm):
    t, d = x1.shape
    per_b = tokens_per_batch // tm
    tok = lambda w: pl.BlockSpec((tm, w), lambda i: (i, 0))
    return pl.pallas_call(
        _moe_combine_kernel,
        out_shape=jax.ShapeDtypeStruct((t, d), F32),
        grid=(t // tm,),
        in_specs=[pl.BlockSpec((tm * 2 * ROW_TILE, LANES), lambda i: (i, 0)), tok(LANES), tok(d),
                  pl.BlockSpec((1, 1, d), lambda i: (i // per_b, 0, 0))],
        out_specs=tok(d),
        compiler_params=_cparams(("parallel",)),
        name="moe_combine",
    )(y_tiles, route, x1, g2)


def _routing_plan(cls, tm):
    t = cls.shape[0]
    n_steps = t // tm + N_CLASSES + 1
    order = jnp.argsort(cls).astype(jnp.int32)
    classes = jnp.arange(N_CLASSES, dtype=jnp.int32)
    counts = jnp.sum((cls[:, None] == classes[None, :]).astype(jnp.int32), axis=0)
    cstart = jnp.cumsum(counts) - counts
    tiles_c = (counts + tm - 1) // tm
    tile_end = jnp.cumsum(tiles_c)
    n_tiles = tile_end[-1]
    tile_ids = jnp.arange(n_steps, dtype=jnp.int32)
    live = tile_ids < n_tiles
    c_of = jnp.sum((tile_end[None, :] <= jnp.minimum(tile_ids, n_tiles - 1)[:, None]).astype(jnp.int32), axis=1)
    k_of = jnp.minimum(tile_ids, n_tiles - 1) - (tile_end - tiles_c)[c_of]
    nv = jnp.where(live, jnp.clip(counts[c_of] - k_of * tm, 0, tm), 0).astype(jnp.int32)
    base = cstart[c_of] + k_of * tm
    idx = order[jnp.minimum(base[:, None] + jnp.arange(tm, dtype=jnp.int32)[None, :], t - 1)]
    grp, pair = c_of // PAIRS_PER_GROUP, c_of % PAIRS_PER_GROUP
    a_tab = jnp.array([a for a in range(EXPERTS_PER_GROUP) for _ in range(a + 1, EXPERTS_PER_GROUP)], jnp.int32)
    b_tab = jnp.array([b for a in range(EXPERTS_PER_GROUP) for b in range(a + 1, EXPERTS_PER_GROUP)], jnp.int32)
    ea = grp * EXPERTS_PER_GROUP + a_tab[pair]
    eb = grp * EXPERTS_PER_GROUP + b_tab[pair]
    return ea.astype(jnp.int32), eb.astype(jnp.int32), nv, n_tiles.reshape(1).astype(jnp.int32), idx


def _rope_tables(n_tokens):
    rows = n_tokens // GRID_W
    row = jnp.broadcast_to(jnp.arange(rows, dtype=F32)[:, None], (rows, GRID_W)).reshape(-1)
    col = jnp.broadcast_to(jnp.arange(GRID_W, dtype=F32)[None, :], (rows, GRID_W)).reshape(-1)
    half = HEAD_DIM // 2
    inv = ROPE_BASE ** (-jnp.arange(0, half, 2, dtype=F32) / half)
    ang = jnp.stack([row[:, None] * inv, col[:, None] * inv], axis=1)
    cos, sin = jnp.cos(ang), jnp.sin(ang)
    cos64 = jnp.concatenate([cos[:, 0], cos[:, 0], cos[:, 1], cos[:, 1]], axis=1)
    sin64 = jnp.concatenate([-sin[:, 0], sin[:, 0], -sin[:, 1], sin[:, 1]], axis=1)
    return jnp.tile(cos64, (1, LANES // HEAD_DIM)), jnp.tile(sin64, (1, LANES // HEAD_DIM))


def _pick_tile(n, target):
    t = min(n, target)
    while n % t:
        t //= 2
    return t


def kernel(x, c, ctx, c_ctx, w_ada, b_ada, norm1_g, w_in, q_norm_g, k_norm_g, lambda_q1, lambda_k1, lambda_q2, lambda_k2, subln_g, ssm_a_re, ssm_a_im, ssm_log_dt, ssm_b_re, ssm_b_im, ssm_c_re, ssm_c_im, ssm_d, w_glu, b_glu, w_out, norm2_g, w_route_group, b_route_group, w_route_expert, b_route_expert, w_exp_gate, w_exp_up, w_exp_down):
    depth = w_ada.shape[0]
    assert depth == 1, "single-layer block: the context stream is never updated"
    b, n_lat, d = x.shape
    n_ctx = ctx.shape[1]
    assert n_lat % CHUNK == 0 and n_ctx % CHUNK == 0 and n_lat % GRID_W == 0
    assert d == ROW_TILE * LANES, "MoE rows are moved as one (8, 128) tile per token"
    l = 0
    lam_init = 0.8 - 0.6 * math.exp(-0.3 * l)

    rows = b + 1
    rows_pad = -(-rows // 8) * 8
    cc = jnp.concatenate([c, c_ctx[None, :], jnp.zeros((rows_pad - rows, d), F32)], axis=0)
    mod = _mod_call(cc, w_ada[l], b_ada[l])
    sh1, sc1, g1, sh2, sc2, g2 = (mod[:b, i * d:(i + 1) * d].reshape(b, 1, d) for i in range(6))
    csh1, csc1 = (mod[b:b + 1, i * d:(i + 1) * d].reshape(1, 1, d) for i in range(2))

    w_in_bf = w_in[l].astype(BF16)
    bd = jnp.kron(jnp.eye(MXU_TILE // HEAD_DIM, dtype=F32), jnp.ones((HEAD_DIM, HEAD_DIM), F32)).astype(BF16)
    qg = jnp.tile(q_norm_g[l], LANES // HEAD_DIM).reshape(1, LANES)
    kg = jnp.tile(k_norm_g[l], LANES // HEAD_DIM).reshape(1, LANES)
    cosf, sinf = _rope_tables(n_lat)
    ones_c, zeros_c = jnp.ones((n_ctx, LANES), F32), jnp.zeros((n_ctx, LANES), F32)
    g1n = norm1_g[l].reshape(1, d)
    tm = _pick_tile(n_lat, INPROJ_TILE)
    q_x, k_x, v_x, u_x = _inproj_call(x, sh1, sc1, g1n, w_in_bf, qg, kg, cosf, sinf, bd, tm, "inproj_lat")
    _, k_c, v_c, u_c = _inproj_call(ctx, csh1, csc1, g1n, w_in_bf, qg, kg, ones_c, zeros_c, bd,
                                    _pick_tile(n_ctx, INPROJ_TILE), "inproj_ctx")

    e1 = jnp.exp(jnp.sum(lambda_q1[l] * lambda_k1[l]))
    e2 = jnp.exp(jnp.sum(lambda_q2[l] * lambda_k2[l]))
    lam_row = jnp.full((1, LANES), e1 - e2 + lam_init, F32)
    score_bound = math.sqrt(HEAD_DIM) * jnp.max(jnp.abs(q_norm_g[l])) * jnp.max(jnp.abs(k_norm_g[l]))

    def attn(bounded):
        return lambda *ops: _attn_call(*ops, **_attn_cfg(n_lat), out_scale=1.0 - lam_init, bounded=bounded)

    a_x = lax.cond(score_bound <= SCORE_BOUND, attn(True), attn(False),
                   lam_row, q_x, k_x, v_x, k_c, v_c, subln_g[l].reshape(1, LANES))

    win, m_op, wo, lam16 = _s5_weights(ssm_a_re[l], ssm_a_im[l], ssm_log_dt[l], ssm_b_re[l], ssm_b_im[l],
                                       ssm_c_re[l], ssm_c_im[l], ssm_d[l])
    n_pairs, kw = u_x.shape[0], u_x.shape[3]
    yg = _s5_call(u_x.reshape(n_pairs, b * (n_lat // CHUNK), kw), u_c.reshape(n_pairs, b * (n_ctx // CHUNK), kw),
                  win, m_op, wo, lam16, b)
    yg = yg.reshape(n_pairs, b, n_lat // CHUNK, kw)

    wr = jnp.concatenate([w_route_group[l], w_route_expert[l]], axis=1)
    wr = jnp.pad(wr, ((0, 0), (0, LANES - wr.shape[1])))
    wr_hi, wr_lo = _split_bf16(wr)
    br = jnp.pad(jnp.concatenate([b_route_group[l], b_route_expert[l]]), (0, LANES - MOE_GROUPS - N_EXPERTS))
    x1, h2, route = _outproj_call(a_x, yg, x, g1, sh2, sc2, norm2_g[l].reshape(1, d),
                                 w_glu[l].astype(BF16), b_glu[l].reshape(1, -1), w_out[l].astype(BF16),
                                 wr_hi, wr_lo, br.reshape(1, LANES), _pick_tile(n_lat, OUTPROJ_TILE))

    t_all = b * n_lat
    route = route.reshape(t_all, LANES)
    ea, eb, nv, n_tiles, idx = _routing_plan(route[:, 2].astype(jnp.int32), _pick_tile(t_all, MOE_TILE))
    y_tiles = _moe_expert_call(ea, eb, nv, n_tiles, idx, h2, w_exp_gate[l].astype(BF16),
                               w_exp_up[l].astype(BF16), w_exp_down[l].astype(BF16), t_all)
    out = _moe_combine_call(y_tiles, route, x1.reshape(t_all, d), g2, n_lat, _pick_tile(n_lat, COMBINE_TILE))
    return out.reshape(b, n_lat, d)
```

```python
import functools
import math

import jax
import jax.numpy as jnp
from jax import lax
from jax.experimental import pallas as pl
from jax.experimental.pallas import tpu as pltpu

F32 = jnp.float32
BF16 = jnp.bfloat16

LANES = 128
MXU_TILE = 256
HEADS = 4
HEAD_DIM = 64
QK_WIDTH = HEADS * 2 * HEAD_DIM
V_WIDTH = HEADS * 2 * HEAD_DIM
GRID_W = 64
ROPE_BASE = 10000.0
SSM_GROUP = 16
SSM_STATE = 64
CHUNK = 16
MOE_GROUPS = 4
EXPERTS_PER_GROUP = 8
N_EXPERTS = MOE_GROUPS * EXPERTS_PER_GROUP
RMS_EPS = 1e-6
INPROJ_TILE = 1024
OUTPROJ_TILE = 1024
ATTN_TQ = 512
ATTN_TK = 4096
SCORE_BOUND = 60.0
SCAN_UNROLL = 8
S5_BATCH_BLOCK = 8
PAIRS_PER_GROUP = EXPERTS_PER_GROUP * (EXPERTS_PER_GROUP - 1) // 2
N_CLASSES = MOE_GROUPS * PAIRS_PER_GROUP
ROW_TILE = 8
MOE_TILE = 256
PUSH_UNROLL = 8
DMA_THREADS = 2
IDX_SLOTS = 5
COMBINE_TILE = 512
VMEM_LIMIT = 48 * 1024 * 1024


def _cparams(sem):
    return pltpu.CompilerParams(dimension_semantics=sem, vmem_limit_bytes=VMEM_LIMIT)


def _split_bf16(a):
    hi = a.astype(BF16)
    lo = (a - hi.astype(F32)).astype(BF16)
    return hi, lo


def _dot(a, b):
    return jnp.dot(a, b, preferred_element_type=F32)


def _dot3(a, b):
    a_hi, a_lo = _split_bf16(a)
    b_hi, b_lo = _split_bf16(b)
    return _dot(a_hi, b_hi) + (_dot(a_hi, b_lo) + _dot(a_lo, b_hi))


def _mod_kernel(c_ref, w_ref, b_ref, o_ref):
    c = c_ref[...]
    a = c * jax.nn.sigmoid(c)
    o_ref[...] = _dot3(a, w_ref[...]) + b_ref[...]


def _mod_call(cc, w_ada, b_ada):
    rows, d = cc.shape
    n = w_ada.shape[1]
    bn = 1024
    return pl.pallas_call(
        _mod_kernel,
        out_shape=jax.ShapeDtypeStruct((rows, n), F32),
        grid=(n // bn,),
        in_specs=[pl.BlockSpec((rows, d), lambda j: (0, 0)),
                  pl.BlockSpec((d, bn), lambda j: (0, j)),
                  pl.BlockSpec((1, bn), lambda j: (0, j))],
        out_specs=pl.BlockSpec((rows, bn), lambda j: (0, j)),
        compiler_params=_cparams(("arbitrary",)),
        name="mod",
    )(cc, w_ada, b_ada.reshape(1, n))


def _inproj_kernel(x_ref, sh_ref, sc_ref, g_ref, w_ref, qg_ref, kg_ref, cos_ref, sin_ref, bd_ref, perm_ref,
                   q_ref, k_ref, v_ref, u_ref):
    x = x_ref[0]
    ms = jnp.mean(x * x, axis=-1, keepdims=True)
    h = (x * lax.rsqrt(ms + RMS_EPS)) * g_ref[...]
    h = h * (1.0 + sc_ref[0]) + sh_ref[0]
    p = _dot(h.astype(BF16), w_ref[...])

    cosf = cos_ref[...]
    sinf = sin_ref[...]
    lane = lax.broadcasted_iota(jnp.int32, cosf.shape, 1)
    first_half = (lane % 32) < 16

    def norm_rope(t, gain, scale):
        sq = (t * t).astype(BF16)
        half = bd_ref.shape[0]
        ss = jnp.concatenate([_dot(sq[:, c:c + half], bd_ref[...]) for c in range(0, QK_WIDTH, half)], axis=1)
        t = t * lax.rsqrt(ss * (1.0 / HEAD_DIM) + RMS_EPS)
        outs = []
        for s in range(QK_WIDTH // LANES):
            ts = t[:, s * LANES:(s + 1) * LANES] * gain
            partner = jnp.where(first_half, pltpu.roll(ts, LANES - 16, 1), pltpu.roll(ts, 16, 1))
            outs.append(((ts * cosf + partner * sinf) * scale).astype(BF16))
        return jnp.concatenate(outs, axis=1)

    q_ref[0] = norm_rope(p[:, :QK_WIDTH], qg_ref[...], HEAD_DIM ** -0.5 * math.log2(math.e))
    k_ref[0] = norm_rope(p[:, QK_WIDTH:2 * QK_WIDTH], kg_ref[...], 1.0)
    v_ref[0] = p[:, 2 * QK_WIDTH:2 * QK_WIDTH + V_WIDTH].astype(BF16)
    u_t = _dot(perm_ref[...], p[:, 2 * QK_WIDTH + V_WIDTH:].astype(BF16))

    def store_u(pr, k, val):
        u_ref[pr, 0, :, k * LANES:(k + 1) * LANES] = val.astype(BF16)

    _step_major_to_chunk_rows(u_t, store_u)


PAIR_W = 2 * SSM_GROUP
PAIRS_PER_TILE = LANES // PAIR_W


def _quarter_select(pieces):
    lane = lax.broadcasted_iota(jnp.int32, pieces[0].shape, 1)
    acc = pieces[0]
    for r in range(1, len(pieces)):
        acc = jnp.where(lane // PAIR_W == r, pieces[r], acc)
    return acc


def _step_perm(tm):
    n_chunk = tm // CHUNK
    src = (jnp.arange(tm) % n_chunk) * CHUNK + jnp.arange(tm) // n_chunk
    return (src[:, None] == jnp.arange(tm)[None, :]).astype(BF16)


def _step_major_to_chunk_rows(u_t, store):
    n_chunk = u_t.shape[0] // CHUNK
    for k in range(CHUNK // PAIRS_PER_TILE):
        for pr in range(u_t.shape[1] // PAIR_W):
            q, r_src = divmod(pr, PAIRS_PER_TILE)
            pieces = []
            for r in range(PAIRS_PER_TILE):
                t = PAIRS_PER_TILE * k + r
                src = u_t[t * n_chunk:(t + 1) * n_chunk, q * LANES:(q + 1) * LANES]
                shift = ((r - r_src) % PAIRS_PER_TILE) * PAIR_W
                pieces.append(pltpu.roll(src, shift, 1) if shift else src)
            store(pr, k, _quarter_select(pieces))


def _chunk_rows_to_token_major(load, ys_ref):
    n_tiles, n_chunk = ys_ref.shape[0], ys_ref.shape[1] // CHUNK
    for t in range(CHUNK):
        k, r_src = divmod(t, PAIRS_PER_TILE)
        for q in range(n_tiles):
            pieces = []
            for r in range(PAIRS_PER_TILE):
                shift = ((r - r_src) % PAIRS_PER_TILE) * PAIR_W
                src = load(PAIRS_PER_TILE * q + r, k)
                pieces.append(pltpu.roll(src, shift, 1) if shift else src)
            ys_ref[q, pl.ds(t, n_chunk, stride=CHUNK), :] = _quarter_select(pieces)


def _inproj_call(x, sh, sc, g, w_bf, qg, kg, cosf, sinf, bd, tm, name):
    b, n, d = x.shape
    wn = w_bf.shape[1]
    per_batch = sh.shape[0] > 1
    mod_map = (lambda bi, i: (bi, 0, 0)) if per_batch else (lambda bi, i: (0, 0, 0))
    const2 = lambda bi, i: (0, 0)
    n_pairs = QK_WIDTH // PAIR_W
    outs = [jax.ShapeDtypeStruct((b, n, QK_WIDTH), BF16)] * 3 + [
        jax.ShapeDtypeStruct((n_pairs, b, n // CHUNK, CHUNK * PAIR_W), BF16)]
    tok_spec = pl.BlockSpec((1, tm, QK_WIDTH), lambda bi, i: (bi, i, 0))
    u_spec = pl.BlockSpec((n_pairs, 1, tm // CHUNK, CHUNK * PAIR_W), lambda bi, i: (0, bi, i, 0))
    return pl.pallas_call(
        _inproj_kernel,
        out_shape=outs,
        grid=(b, n // tm),
        in_specs=[pl.BlockSpec((1, tm, d), lambda bi, i: (bi, i, 0)),
                  pl.BlockSpec((1, 1, d), mod_map),
                  pl.BlockSpec((1, 1, d), mod_map),
                  pl.BlockSpec((1, d), const2),
                  pl.BlockSpec((d, wn), const2),
                  pl.BlockSpec((1, LANES), const2),
                  pl.BlockSpec((1, LANES), const2),
                  pl.BlockSpec((tm, LANES), lambda bi, i: (i, 0)),
                  pl.BlockSpec((tm, LANES), lambda bi, i: (i, 0)),
                  pl.BlockSpec(bd.shape, const2),
                  pl.BlockSpec((tm, tm), const2)],
        out_specs=[tok_spec] * 3 + [u_spec],
        compiler_params=_cparams(("parallel", "parallel")),
        name=name,
    )(x, sh, sc, g, w_bf, qg, kg, cosf, sinf, bd, _step_perm(tm))


def _attn_kernel(lam_ref, q_ref, kl_ref, vl_ref, kc_ref, vc_ref, sg_ref, o_ref, a1_ref, a2_ref, m1_ref, m2_ref,
                 *, tk, out_scale, bounded):
    q = q_ref[0]
    lane = lax.broadcasted_iota(jnp.int32, q.shape, 1)
    zero = jnp.zeros_like(q)
    qa = jnp.where(lane < HEAD_DIM, q, zero)
    qb = jnp.where(lane >= HEAD_DIM, q, zero)

    a1_ref[...] = jnp.zeros(a1_ref.shape, F32)
    a2_ref[...] = jnp.zeros(a2_ref.shape, F32)
    if not bounded:
        m1_ref[...] = jnp.full(m1_ref.shape, -jnp.inf, F32)
        m2_ref[...] = jnp.full(m2_ref.shape, -jnp.inf, F32)

    def ones_col(rows):
        col = lax.broadcasted_iota(jnp.int32, (rows, LANES), 1)
        return jnp.where(col == 0, 1.0, 0.0).astype(BF16)

    def one_map(qm, kc, va, a_ref, m_ref):
        s = lax.dot_general(qm, kc, (((1,), (1,)), ((), ())), preferred_element_type=F32)
        if bounded:
            a_ref[...] += _dot(jnp.exp2(s).astype(BF16), va)
        else:
            m_prev = m_ref[...]
            m_next = jnp.maximum(m_prev, jnp.max(s, axis=1, keepdims=True))
            p = jnp.exp2(s - m_next[:, :1])
            alpha = jnp.exp2(m_prev - m_next)
            a_ref[...] = jnp.concatenate([alpha, alpha], axis=1) * a_ref[...] + _dot(p.astype(BF16), va)
            m_ref[...] = m_next

    def step(kc, vc, ones):
        va = jnp.concatenate([vc, ones], axis=1)
        one_map(qa, kc, va, a1_ref, m1_ref)
        one_map(qb, kc, va, a2_ref, m2_ref)

    ones_lat = ones_col(tk)

    def lat_body(j, carry):
        off = pl.multiple_of(j * tk, tk)
        step(kl_ref[0, pl.ds(off, tk), :], vl_ref[0, pl.ds(off, tk), :], ones_lat)
        return carry

    lax.fori_loop(0, kl_ref.shape[1] // tk, lat_body, 0)
    step(kc_ref[0], vc_ref[0], ones_col(kc_ref.shape[1]))

    lam = lam_ref[...]
    a1, a2 = a1_ref[...], a2_ref[...]
    o = a1[:, :LANES] / a1[:, LANES:LANES + 1] - lam * (a2[:, :LANES] / a2[:, LANES:LANES + 1])
    ms = jnp.mean(o * o, axis=-1, keepdims=True)
    o = o * lax.rsqrt(ms + RMS_EPS) * sg_ref[...]
    o_ref[0] = (o * out_scale).astype(BF16)


def _attn_cfg(n_lat):
    return dict(tq=_pick_tile(n_lat, ATTN_TQ), tk=_pick_tile(n_lat, ATTN_TK))


def _attn_call(lam_row, q, k_lat, v_lat, k_ctx, v_ctx, sg, tq, tk, out_scale=1.0, bounded=True):
    b, n, _ = q.shape
    nc = k_ctx.shape[1]
    kv_lat = pl.BlockSpec((1, n, LANES), lambda bi, h, i: (bi, 0, h))
    kv_ctx = pl.BlockSpec((1, nc, LANES), lambda bi, h, i: (bi, 0, h))
    q_spec = pl.BlockSpec((1, tq, LANES), lambda bi, h, i: (bi, i, h))
    row = pl.BlockSpec((1, LANES), lambda bi, h, i: (0, 0))
    acc = pltpu.VMEM((tq, 2 * LANES), F32)
    run_max = pltpu.VMEM((tq, LANES), F32)
    return pl.pallas_call(
        functools.partial(_attn_kernel, tk=tk, out_scale=out_scale, bounded=bounded),
        out_shape=jax.ShapeDtypeStruct((b, n, V_WIDTH), BF16),
        grid=(b, HEADS, n // tq),
        in_specs=[row, q_spec, kv_lat, kv_lat, kv_ctx, kv_ctx, row],
        out_specs=q_spec,
        scratch_shapes=[acc, acc, run_max, run_max],
        compiler_params=_cparams(("parallel", "parallel", "arbitrary")),
        name="attn" if bounded else "attn_general",
    )(lam_row, q, k_lat, v_lat, k_ctx, v_ctx, sg)


def _s5_kernel(ul_ref, uc_ref, win_ref, m_ref, wo_ref, lam_ref, o_ref, xl_ref, xc_ref, s_ref, *, nb):
    ul = ul_ref[0]
    n_slab = xl_ref.shape[0]

    def to_slabs(x_ref, x):
        for k in range(n_slab):
            x_ref[k] = x[:, k * LANES:(k + 1) * LANES]

    to_slabs(xl_ref, _dot(ul, win_ref[0]))
    to_slabs(xc_ref, _dot(uc_ref[0], win_ref[0]))
    n_lat = ul.shape[0] // nb
    n_ctx = uc_ref.shape[1] // nb

    lam = lam_ref[0]
    lfr, lfi, lbr, lbi = (jnp.broadcast_to(lam[i:i + 1], (nb, LANES)) for i in range(4))

    def rows(c, n_chunks):
        return pl.ds(c, nb, stride=n_chunks)

    def advance(x_ref, c, n_chunks, slab, ar, ai, sr, si):
        xr = x_ref[slab, rows(c, n_chunks), :]
        xi = x_ref[slab + 1, rows(c, n_chunks), :]
        return ar * sr - ai * si + xr, ar * si + ai * sr + xi

    def ctx_body(i, carry):
        fr, fi, br, bi = carry
        fr, fi = advance(xc_ref, i, n_ctx, 0, lfr, lfi, fr, fi)
        br, bi = advance(xc_ref, n_ctx - 1 - i, n_ctx, 2, lbr, lbi, br, bi)
        return fr, fi, br, bi

    def lat_body(i, carry):
        fr, fi, br, bi = carry
        cb = n_lat - 1 - i
        s_ref[0, pl.ds(pl.multiple_of(i * nb, nb), nb), :] = fr
        s_ref[1, pl.ds(pl.multiple_of(i * nb, nb), nb), :] = fi
        s_ref[2, pl.ds(pl.multiple_of(cb * nb, nb), nb), :] = br
        s_ref[3, pl.ds(pl.multiple_of(cb * nb, nb), nb), :] = bi
        fr, fi = advance(xl_ref, i, n_lat, 0, lfr, lfi, fr, fi)
        br, bi = advance(xl_ref, cb, n_lat, 2, lbr, lbi, br, bi)
        return fr, fi, br, bi

    z = jnp.zeros((nb, LANES), F32)
    carry = lax.fori_loop(0, n_ctx, ctx_body, (z, z, z, z), unroll=SCAN_UNROLL)
    lax.fori_loop(0, n_lat, lat_body, carry, unroll=SCAN_UNROLL)

    def batch_major(k):
        return jnp.concatenate([s_ref[k, pl.ds(bi, n_lat, stride=nb), :] for bi in range(nb)], axis=0)

    s_in = jnp.concatenate([batch_major(k).astype(BF16) for k in range(n_slab)], axis=1)
    y = _dot(ul, m_ref[0]) + _dot(s_in, wo_ref[0])
    o_ref[0] = jax.nn.gelu(y).astype(BF16)


def _s5_call(u_lat, u_ctx, win, m, wo, lam16, b):
    npair, rows_lat, kw = u_lat.shape
    nb = S5_BATCH_BLOCK if b % S5_BATCH_BLOCK == 0 else b
    rl = rows_lat // b * nb
    rc = u_ctx.shape[1] // b * nb
    wspec = pl.BlockSpec((1, kw, kw), lambda p, h: (p, 0, 0))
    slabs = lambda r: pltpu.VMEM((kw // LANES, r, LANES), F32)
    return pl.pallas_call(
        functools.partial(_s5_kernel, nb=nb),
        out_shape=jax.ShapeDtypeStruct((npair, rows_lat, kw), BF16),
        grid=(npair, b // nb),
        in_specs=[pl.BlockSpec((1, rl, kw), lambda p, h: (p, h, 0)),
                  pl.BlockSpec((1, rc, kw), lambda p, h: (p, h, 0)),
                  wspec, wspec, wspec,
                  pl.BlockSpec((1, 4, LANES), lambda p, h: (p, 0, 0))],
        out_specs=pl.BlockSpec((1, rl, kw), lambda p, h: (p, h, 0)),
        scratch_shapes=[slabs(rl), slabs(rc), slabs(rl)],
        compiler_params=_cparams(("parallel", "parallel")),
        name="s5",
    )(u_lat, u_ctx, win, m, wo, lam16)


def _s5_weights(a_re, a_im, log_dt, b_re, b_im, c_re, c_im, d_skip):
    hp = lax.Precision.HIGHEST
    g_n, p_n = a_re.shape[1], a_re.shape[2]
    t_n = CHUNK
    dt = jnp.exp(log_dt.astype(F32))[..., None]
    ar, ai = a_re.astype(F32), a_im.astype(F32)
    mag = jnp.exp(ar * dt)
    lr, li = mag * jnp.cos(ai * dt), mag * jnp.sin(ai * dt)
    den = ar * ar + ai * ai
    nr, ni = lr - 1.0, li
    cr = (nr * ar + ni * ai) / den
    ci = (ni * ar - nr * ai) / den
    bbr = cr[..., None] * b_re - ci[..., None] * b_im
    bbi = cr[..., None] * b_im + ci[..., None] * b_re
    n = jnp.arange(t_n + 1, dtype=F32)[:, None, None, None]
    pm = jnp.exp(n * (ar * dt))
    pw_r, pw_i = pm * jnp.cos(n * (ai * dt)), pm * jnp.sin(n * (ai * dt))
    lb_r = pw_r[:t_n, ..., None] * bbr - pw_i[:t_n, ..., None] * bbi
    lb_i = pw_r[:t_n, ..., None] * bbi + pw_i[:t_n, ..., None] * bbr
    cre, cim = c_re.astype(F32), c_im.astype(F32)
    kern = (jnp.einsum('gip,tdgpj->tdgij', cre, lb_r, precision=hp)
            - jnp.einsum('gip,tdgpj->tdgij', cim, lb_i, precision=hp))
    hh = SSM_GROUP
    npair = g_n // 2
    eye2 = jnp.eye(2, dtype=F32)
    kw = 2 * t_n * hh

    lag0 = kern[0, 0] + kern[0, 1] + jnp.eye(hh, dtype=F32)[None] * d_skip.astype(F32)[:, :, None]
    by_lag = jnp.concatenate([kern[:0:-1, 1], lag0[None], kern[1:, 0]], axis=0)
    strip = jnp.einsum('ab,pajlq->pajlbq', eye2,
                       jnp.transpose(by_lag, (1, 3, 0, 2)).reshape(npair, 2, hh, 2 * t_n - 1, hh))
    strip = strip.reshape(npair, 2 * hh, (2 * t_n - 1) * 2 * hh)
    m_pair = jnp.stack([strip[:, :, (t_n - 1 - s) * 2 * hh:(t_n - 1 - s) * 2 * hh + kw] for s in range(t_n)],
                       axis=1).reshape(npair, kw, kw)

    def state_lanes(parts, mask_pair):
        x = jnp.stack(parts, axis=2).reshape(npair, 2, parts[0].shape[1], 4, 1, p_n)
        sel = eye2[None, :, None, None, :, None] if mask_pair else jnp.ones((1, 1, 1, 1, 2, 1), F32)
        return (x * sel).reshape(npair, 2, parts[0].shape[1], 8 * p_n)

    gsp = lambda z: jnp.transpose(z, (1, 0, 2))
    pf_r, pf_i, pb_r, pb_i = pw_r[:, 0], pw_i[:, 0], pw_r[:, 1], pw_i[:, 1]
    pa = state_lanes([gsp(pf_r[t_n - 1::-1]), gsp(pf_r[t_n - 1::-1]), gsp(pb_r[:t_n]), gsp(pb_r[:t_n])], True)
    pb = state_lanes([-gsp(pf_i[t_n - 1::-1]), gsp(pf_i[t_n - 1::-1]), -gsp(pb_i[:t_n]), gsp(pb_i[:t_n])], True)
    bt = lambda z: jnp.transpose(z, (0, 2, 1))
    ba = state_lanes([bt(bbr[0]), bt(bbi[0]), bt(bbr[1]), bt(bbi[1])], False)
    bb = state_lanes([bt(bbi[0]), bt(bbr[0]), bt(bbi[1]), bt(bbr[1])], False)
    sa = lambda z: jnp.transpose(z, (0, 2, 1, 3))
    win_pair = (sa(pa)[:, :, :, None, :] * ba[:, None] + sa(pb)[:, :, :, None, :] * bb[:, None]
                ).reshape(npair, kw, 8 * p_n)
    qa = state_lanes([gsp(pf_r[1:]), -gsp(pf_i[1:]), gsp(pb_r[t_n:0:-1]), -gsp(pb_i[t_n:0:-1])], True)
    qb = state_lanes([-gsp(pf_i[1:]), -gsp(pf_r[1:]), -gsp(pb_i[t_n:0:-1]), -gsp(pb_r[t_n:0:-1])], True)
    ca = state_lanes([cre] * 4, False)
    cb = state_lanes([cim] * 4, False)
    wo_t = (sa(qa)[:, :, :, None, :] * ca[:, None] + sa(qb)[:, :, :, None, :] * cb[:, None]
            ).reshape(npair, kw, 8 * p_n)
    wo_pair = jnp.swapaxes(wo_t.astype(BF16), 1, 2)
    lam16 = jnp.stack([pw_r[t_n, 0], pw_i[t_n, 0], pw_r[t_n, 1], pw_i[t_n, 1]], axis=1)
    lam16 = jnp.transpose(lam16.reshape(npair, 2, 4, p_n), (0, 2, 1, 3)).reshape(npair, 4, 2 * p_n)
    return win_pair.astype(BF16), m_pair.astype(BF16), wo_pair, lam16


def _outproj_kernel(a_ref, y_ref, x_ref, g1_ref, sh_ref, sc_ref, n2_ref, wglu_ref, bglu_ref, wout_ref,
                    wr_hi_ref, wr_lo_ref, br_ref, x1_ref, h2_ref, gate_ref, ys_ref):
    _chunk_rows_to_token_major(
        lambda pr, k: y_ref[pr, 0, :, k * LANES:(k + 1) * LANES].astype(F32), ys_ref)
    yf = jnp.concatenate([ys_ref[q] for q in range(ys_ref.shape[0])], axis=1)
    y = yf.astype(BF16)
    z = _dot(y, wglu_ref[...]) + bglu_ref[...]
    s = (yf * jax.nn.sigmoid(z)).astype(BF16)
    o = _dot(jnp.concatenate([a_ref[0], s], axis=1), wout_ref[...])
    x1 = x_ref[0] + g1_ref[0] * o
    x1_ref[0] = x1

    ms = jnp.mean(x1 * x1, axis=-1, keepdims=True)
    h = (x1 * lax.rsqrt(ms + RMS_EPS)) * n2_ref[...]
    h = h * (1.0 + sc_ref[0]) + sh_ref[0]
    for j in range(h.shape[1] // LANES):
        h2_ref[pl.ds(j, h.shape[0], stride=ROW_TILE), :] = h[:, j * LANES:(j + 1) * LANES]

    h_hi, h_lo = _split_bf16(h)
    hi_both = _dot(h_hi, jnp.concatenate([wr_hi_ref[...], wr_lo_ref[...]], axis=1))
    lg = hi_both[:, :LANES] + (hi_both[:, LANES:] + _dot(h_lo, wr_hi_ref[...])) + br_ref[...]
    lane = lax.broadcasted_iota(jnp.int32, lg.shape, 1)
    neg = jnp.float32(-jnp.inf)
    lane_f = lane.astype(F32)

    def top1(vals):
        vmax = jnp.max(vals, axis=1, keepdims=True)
        idx = jnp.min(jnp.where(vals == vmax, lane_f, float(LANES)), axis=1, keepdims=True)
        return vmax, idx.astype(jnp.int32)

    is_grp = lane < MOE_GROUPS
    g_vals = jnp.where(is_grp, lg, neg)
    g_max, g_idx = top1(g_vals)
    p_grp = 1.0 / jnp.sum(jnp.where(is_grp, jnp.exp(g_vals - g_max), 0.0), axis=1, keepdims=True)
    e_lo = MOE_GROUPS + EXPERTS_PER_GROUP * g_idx
    in_grp = (lane >= e_lo) & (lane < e_lo + EXPERTS_PER_GROUP)
    e_vals = jnp.where(in_grp, lg, neg)
    v1, i1 = top1(e_vals)
    v2, i2 = top1(jnp.where(lane == i1, neg, e_vals))
    r = jnp.exp(v2 - v1)
    w1 = p_grp / (1.0 + r)
    w2 = w1 * r
    first_lo = i1 < i2
    a_loc = jnp.where(first_lo, i1, i2) - e_lo
    b_loc = jnp.where(first_lo, i2, i1) - e_lo
    pair = ((a_loc * (2 * EXPERTS_PER_GROUP - 1 - a_loc)) >> 1) + (b_loc - a_loc - 1)
    cls = g_idx * PAIRS_PER_GROUP + pair
    w_lo = jnp.where(first_lo, w1, w2)
    w_hi = jnp.where(first_lo, w2, w1)
    gate_ref[0] = (jnp.where(lane == 0, w_lo, 0.0) + jnp.where(lane == 1, w_hi, 0.0)
                   + jnp.where(lane == 2, cls.astype(F32), 0.0))


def _outproj_call(a, yg, x, g1, sh2, sc2, n2g, wglu, bglu, wout, wr_hi, wr_lo, br, tm):
    b, n, d = x.shape
    half = a.shape[2]
    tok = lambda w: pl.BlockSpec((1, tm, w), lambda bi, i: (bi, i, 0))
    mod = pl.BlockSpec((1, 1, d), lambda bi, i: (bi, 0, 0))
    const = lambda r, c: pl.BlockSpec((r, c), lambda bi, i: (0, 0))
    return pl.pallas_call(
        _outproj_kernel,
        out_shape=[jax.ShapeDtypeStruct((b, n, d), F32), jax.ShapeDtypeStruct((b * n * ROW_TILE, LANES), F32),
                   jax.ShapeDtypeStruct((b, n, LANES), F32)],
        grid=(b, n // tm),
        in_specs=[tok(half),
                  pl.BlockSpec((yg.shape[0], 1, tm // CHUNK, yg.shape[3]), lambda bi, i: (0, bi, i, 0)),
                  tok(d), mod, mod, mod, const(1, d),
                  const(half, half), const(1, half), const(d, d),
                  const(d, LANES), const(d, LANES), const(1, LANES)],
        out_specs=[tok(d), pl.BlockSpec((tm * ROW_TILE, LANES), lambda bi, i: (bi * (n // tm) + i, 0)), tok(LANES)],
        scratch_shapes=[pltpu.VMEM((half // LANES, tm, LANES), F32)],
        compiler_params=_cparams(("parallel", "parallel")),
        name="outproj",
    )(a, yg, x, g1, sh2, sc2, n2g, wglu, bglu, wout, wr_hi, wr_lo, br)


def _token_rows(tok, rows_per_token):
    return pl.ds(pl.multiple_of(tok * rows_per_token, rows_per_token), rows_per_token)


def _moe_expert_kernel(ea_ref, eb_ref, nv_ref, nt_ref, idx_hbm, h_hbm, wga_ref, wua_ref, wda_ref,
                       wgb_ref, wub_ref, wdb_ref, y_hbm, idx_smem, hbuf, ybuf, gsem, isem, psem, *, tm):
    i = pl.program_id(0)
    n_steps = pl.num_programs(0)
    n_tiles = nt_ref[0]
    hrows, yrows = tm * ROW_TILE, tm * 2 * ROW_TILE

    def idx_copy(tile):
        s = tile % IDX_SLOTS
        return pltpu.make_async_copy(idx_hbm.at[pl.ds(tile, 1), :], idx_smem.at[pl.ds(s, 1), :], isem.at[s])

    def pull_row(tile, r):
        tok = idx_smem[tile % IDX_SLOTS, r]
        s2 = tile % 3
        return pltpu.make_async_copy(h_hbm.at[_token_rows(tok, ROW_TILE), :],
                                     hbuf.at[pl.ds(s2 * hrows + r * ROW_TILE, ROW_TILE), :], gsem.at[s2])

    def pull_tile(tile):
        def body(r, carry):
            pull_row(tile, r).start()
            return carry
        lax.fori_loop(0, tm, body, 0, unroll=PUSH_UNROLL)

    def push_row(tile, r):
        tok = idx_smem[tile % IDX_SLOTS, r]
        s3 = tile % 3
        return pltpu.make_async_copy(ybuf.at[pl.ds(s3 * yrows + r * 2 * ROW_TILE, 2 * ROW_TILE), :],
                                     y_hbm.at[_token_rows(tok, 2 * ROW_TILE), :], psem.at[s3])

    def wait_pulls(tile):
        s2 = tile % 3
        pltpu.make_async_copy(h_hbm.at[pl.ds(0, hrows), :], hbuf.at[pl.ds(s2 * hrows, hrows), :], gsem.at[s2]).wait()

    def wait_pushes(tile):
        s3 = tile % 3
        n = nv_ref[tile] * (2 * ROW_TILE)

        @pl.when(n > 0)
        def _():
            pltpu.make_async_copy(ybuf.at[pl.ds(s3 * yrows, n), :], y_hbm.at[pl.ds(0, n), :], psem.at[s3]).wait()

    @pl.when(i == 0)
    def _():
        idx_copy(0).start()
        idx_copy(1).start()
        idx_copy(0).wait()
        idx_copy(1).wait()
        pull_tile(0)

        @pl.when(1 < n_tiles)
        def _():
            pull_tile(1)
        idx_copy(2).start()

    @pl.when(i + 3 < n_steps)
    def _():
        idx_copy(i + 3).start()

    @pl.when(i + 2 < n_steps)
    def _():
        idx_copy(i + 2).wait()

    @pl.when(i < n_tiles)
    def _():
        wait_pulls(i)

        @pl.when(i >= 3)
        def _():
            wait_pushes(i - 3)

        s3 = i % 3
        h = jnp.concatenate([hbuf[pl.ds(s3 * hrows + j, tm, stride=ROW_TILE), :] for j in range(ROW_TILE)],
                            axis=1).astype(BF16)
        pull_next = i + 2 < n_tiles
        n_push = jnp.where(i >= 1, nv_ref[jnp.maximum(i - 1, 0)], 0)

        def issue_rows(r0, r1):
            for r in range(r0, r1):
                @pl.when(pull_next)
                def _():
                    pull_row(i + 2, r).start(priority=r % DMA_THREADS)

                @pl.when(r < n_push)
                def _():
                    push_row(i - 1, r).start(priority=r % DMA_THREADS)

        def expert(wg_ref, wu_ref, wd_ref, row0):
            hid = jax.nn.silu(_dot(h, wg_ref[0])) * _dot(h, wu_ref[0])
            y = _dot(hid.astype(BF16), wd_ref[0])
            for j in range(ROW_TILE):
                ybuf[pl.ds(s3 * yrows + row0 + j, tm, stride=2 * ROW_TILE), :] = y[:, j * LANES:(j + 1) * LANES]

        issue_rows(0, tm // 2)
        expert(wga_ref, wua_ref, wda_ref, 0)
        issue_rows(tm // 2, tm)
        expert(wgb_ref, wub_ref, wdb_ref, ROW_TILE)

    @pl.when(i == n_tiles)
    def _():
        def body(r, carry):
            push_row(i - 1, r).start()
            return carry
        lax.fori_loop(0, nv_ref[i - 1], body, 0)

        for back in (3, 2, 1):
            @pl.when(i >= back)
            def _():
                wait_pushes(i - back)


def _moe_expert_call(ea, eb, nv, nt, idx, h_tiles, wg, wu, wd, n_tokens):
    n_steps, tm = idx.shape
    ne, d, f = wg.shape
    amap = lambda i, ea, eb, nv, nt: (ea[i], 0, 0)
    bmap = lambda i, ea, eb, nv, nt: (eb[i], 0, 0)
    up = lambda m: pl.BlockSpec((1, d, f), m)
    down = lambda m: pl.BlockSpec((1, f, d), m)
    hbm = pl.BlockSpec(memory_space=pl.ANY)
    grid_spec = pltpu.PrefetchScalarGridSpec(
        num_scalar_prefetch=4,
        grid=(n_steps,),
        in_specs=[hbm, hbm, up(amap), up(amap), down(amap), up(bmap), up(bmap), down(bmap)],
        out_specs=hbm,
        scratch_shapes=[pltpu.SMEM((IDX_SLOTS, tm), jnp.int32),
                        pltpu.VMEM((3 * tm * ROW_TILE, LANES), F32),
                        pltpu.VMEM((3 * tm * 2 * ROW_TILE, LANES), F32),
                        pltpu.SemaphoreType.DMA((3,)), pltpu.SemaphoreType.DMA((IDX_SLOTS,)),
                        pltpu.SemaphoreType.DMA((3,))])
    return pl.pallas_call(
        functools.partial(_moe_expert_kernel, tm=tm),
        out_shape=jax.ShapeDtypeStruct((n_tokens * 2 * ROW_TILE, LANES), F32),
        grid_spec=grid_spec,
        compiler_params=_cparams(("arbitrary",)),
        name="moe_experts",
    )(ea, eb, nv, nt, idx, h_tiles, wg, wu, wd, wg, wu, wd)


def _moe_combine_kernel(y_ref, route_ref, x1_ref, g2_ref, o_ref):
    tm = x1_ref.shape[0]
    ya = jnp.concatenate([y_ref[pl.ds(j, tm, stride=2 * ROW_TILE), :] for j in range(ROW_TILE)], axis=1)
    yb = jnp.concatenate([y_ref[pl.ds(ROW_TILE + j, tm, stride=2 * ROW_TILE), :] for j in range(ROW_TILE)], axis=1)
    route = route_ref[...]
    moe = route[:, 0:1] * ya + route[:, 1:2] * yb
    o_ref[...] = x1_ref[...] + g2_ref[0] * moe


def _moe_combine_call(y_tiles, route, x1, g2, tokens_per_batch, tm):
    t, d = x1.shape
    per_b = tokens_per_batch // tm
    tok = lambda w: pl.BlockSpec((tm, w), lambda i: (i, 0))
    return pl.pallas_call(
        _moe_combine_kernel,
        out_shape=jax.ShapeDtypeStruct((t, d), F32),
        grid=(t // tm,),
        in_specs=[pl.BlockSpec((tm * 2 * ROW_TILE, LANES), lambda i: (i, 0)), tok(LANES), tok(d),
                  pl.BlockSpec((1, 1, d), lambda i: (i // per_b, 0, 0))],
        out_specs=tok(d),
        compiler_params=_cparams(("parallel",)),
        name="moe_combine",
    )(y_tiles, route, x1, g2)


def _routing_plan(cls, tm):
    t = cls.shape[0]
    n_steps = t // tm + N_CLASSES + 1
    order = jnp.argsort(cls).astype(jnp.int32)
    classes = jnp.arange(N_CLASSES, dtype=jnp.int32)
    counts = jnp.sum((cls[:, None] == classes[None, :]).astype(jnp.int32), axis=0)
    cstart = jnp.cumsum(counts) - counts
    tiles_c = (counts + tm - 1) // tm
    tile_end = jnp.cumsum(tiles_c)
    n_tiles = tile_end[-1]
    tile_ids = jnp.arange(n_steps, dtype=jnp.int32)
    live = tile_ids < n_tiles
    c_of = jnp.sum((tile_end[None, :] <= jnp.minimum(tile_ids, n_tiles - 1)[:, None]).astype(jnp.int32), axis=1)
    k_of = jnp.minimum(tile_ids, n_tiles - 1) - (tile_end - tiles_c)[c_of]
    nv = jnp.where(live, jnp.clip(counts[c_of] - k_of * tm, 0, tm), 0).astype(jnp.int32)
    base = cstart[c_of] + k_of * tm
    idx = order[jnp.minimum(base[:, None] + jnp.arange(tm, dtype=jnp.int32)[None, :], t - 1)]
    grp, pair = c_of // PAIRS_PER_GROUP, c_of % PAIRS_PER_GROUP
    a_tab = jnp.array([a for a in range(EXPERTS_PER_GROUP) for _ in range(a + 1, EXPERTS_PER_GROUP)], jnp.int32)
    b_tab = jnp.array([b for a in range(EXPERTS_PER_GROUP) for b in range(a + 1, EXPERTS_PER_GROUP)], jnp.int32)
    ea = grp * EXPERTS_PER_GROUP + a_tab[pair]
    eb = grp * EXPERTS_PER_GROUP + b_tab[pair]
    return ea.astype(jnp.int32), eb.astype(jnp.int32), nv, n_tiles.reshape(1).astype(jnp.int32), idx


def _rope_tables(n_tokens):
    rows = n_tokens // GRID_W
    row = jnp.broadcast_to(jnp.arange(rows, dtype=F32)[:, None], (rows, GRID_W)).reshape(-1)
    col = jnp.broadcast_to(jnp.arange(GRID_W, dtype=F32)[None, :], (rows, GRID_W)).reshape(-1)
    half = HEAD_DIM // 2
    inv = ROPE_BASE ** (-jnp.arange(0, half, 2, dtype=F32) / half)
    ang = jnp.stack([row[:, None] * inv, col[:, None] * inv], axis=1)
    cos, sin = jnp.cos(ang), jnp.sin(ang)
    cos64 = jnp.concatenate([cos[:, 0], cos[:, 0], cos[:, 1], cos[:, 1]], axis=1)
    sin64 = jnp.concatenate([-sin[:, 0], sin[:, 0], -sin[:, 1], sin[:, 1]], axis=1)
    return jnp.tile(cos64, (1, LANES // HEAD_DIM)), jnp.tile(sin64, (1, LANES // HEAD_DIM))


def _pick_tile(n, target):
    t = min(n, target)
    while n % t:
        t //= 2
    return t


def kernel(x, c, ctx, c_ctx, w_ada, b_ada, norm1_g, w_in, q_norm_g, k_norm_g, lambda_q1, lambda_k1, lambda_q2, lambda_k2, subln_g, ssm_a_re, ssm_a_im, ssm_log_dt, ssm_b_re, ssm_b_im, ssm_c_re, ssm_c_im, ssm_d, w_glu, b_glu, w_out, norm2_g, w_route_group, b_route_group, w_route_expert, b_route_expert, w_exp_gate, w_exp_up, w_exp_down):
    depth = w_ada.shape[0]
    assert depth == 1, "single-layer block: the context stream is never updated"
    b, n_lat, d = x.shape
    n_ctx = ctx.shape[1]
    assert n_lat % CHUNK == 0 and n_ctx % CHUNK == 0 and n_lat % GRID_W == 0
    assert d == ROW_TILE * LANES, "MoE rows are moved as one (8, 128) tile per token"
    l = 0
    lam_init = 0.8 - 0.6 * math.exp(-0.3 * l)

    rows = b + 1
    rows_pad = -(-rows // 8) * 8
    cc = jnp.concatenate([c, c_ctx[None, :], jnp.zeros((rows_pad - rows, d), F32)], axis=0)
    mod = _mod_call(cc, w_ada[l], b_ada[l])
    sh1, sc1, g1, sh2, sc2, g2 = (mod[:b, i * d:(i + 1) * d].reshape(b, 1, d) for i in range(6))
    csh1, csc1 = (mod[b:b + 1, i * d:(i + 1) * d].reshape(1, 1, d) for i in range(2))

    w_in_bf = w_in[l].astype(BF16)
    bd = jnp.kron(jnp.eye(MXU_TILE // HEAD_DIM, dtype=F32), jnp.ones((HEAD_DIM, HEAD_DIM), F32)).astype(BF16)
    qg = jnp.tile(q_norm_g[l], LANES // HEAD_DIM).reshape(1, LANES)
    kg = jnp.tile(k_norm_g[l], LANES // HEAD_DIM).reshape(1, LANES)
    cosf, sinf = _rope_tables(n_lat)
    ones_c, zeros_c = jnp.ones((n_ctx, LANES), F32), jnp.zeros((n_ctx, LANES), F32)
    g1n = norm1_g[l].reshape(1, d)
    tm = _pick_tile(n_lat, INPROJ_TILE)
    q_x, k_x, v_x, u_x = _inproj_call(x, sh1, sc1, g1n, w_in_bf, qg, kg, cosf, sinf, bd, tm, "inproj_lat")
    _, k_c, v_c, u_c = _inproj_call(ctx, csh1, csc1, g1n, w_in_bf, qg, kg, ones_c, zeros_c, bd,
                                    _pick_tile(n_ctx, INPROJ_TILE), "inproj_ctx")

    e1 = jnp.exp(jnp.sum(lambda_q1[l] * lambda_k1[l]))
    e2 = jnp.exp(jnp.sum(lambda_q2[l] * lambda_k2[l]))
    lam_row = jnp.full((1, LANES), e1 - e2 + lam_init, F32)
    score_bound = math.sqrt(HEAD_DIM) * jnp.max(jnp.abs(q_norm_g[l])) * jnp.max(jnp.abs(k_norm_g[l]))

    def attn(bounded):
        return lambda *ops: _attn_call(*ops, **_attn_cfg(n_lat), out_scale=1.0 - lam_init, bounded=bounded)

    a_x = lax.cond(score_bound <= SCORE_BOUND, attn(True), attn(False),
                   lam_row, q_x, k_x, v_x, k_c, v_c, subln_g[l].reshape(1, LANES))

    win, m_op, wo, lam16 = _s5_weights(ssm_a_re[l], ssm_a_im[l], ssm_log_dt[l], ssm_b_re[l], ssm_b_im[l],
                                       ssm_c_re[l], ssm_c_im[l], ssm_d[l])
    n_pairs, kw = u_x.shape[0], u_x.shape[3]
    yg = _s5_call(u_x.reshape(n_pairs, b * (n_lat // CHUNK), kw), u_c.reshape(n_pairs, b * (n_ctx // CHUNK), kw),
                  win, m_op, wo, lam16, b)
    yg = yg.reshape(n_pairs, b, n_lat // CHUNK, kw)

    wr = jnp.concatenate([w_route_group[l], w_route_expert[l]], axis=1)
    wr = jnp.pad(wr, ((0, 0), (0, LANES - wr.shape[1])))
    wr_hi, wr_lo = _split_bf16(wr)
    br = jnp.pad(jnp.concatenate([b_route_group[l], b_route_expert[l]]), (0, LANES - MOE_GROUPS - N_EXPERTS))
    x1, h2, route = _outproj_call(a_x, yg, x, g1, sh2, sc2, norm2_g[l].reshape(1, d),
                                 w_glu[l].astype(BF16), b_glu[l].reshape(1, -1), w_out[l].astype(BF16),
                                 wr_hi, wr_lo, br.reshape(1, LANES), _pick_tile(n_lat, OUTPROJ_TILE))

    t_all = b * n_lat
    route = route.reshape(t_all, LANES)
    ea, eb, nv, n_tiles, idx = _routing_plan(route[:, 2].astype(jnp.int32), _pick_tile(t_all, MOE_TILE))
    y_tiles = _moe_expert_call(ea, eb, nv, n_tiles, idx, h2, w_exp_gate[l].astype(BF16),
                               w_exp_up[l].astype(BF16), w_exp_down[l].astype(BF16), t_all)
    out = _moe_combine_call(y_tiles, route, x1.reshape(t_all, d), g2, n_lat, _pick_tile(n_lat, COMBINE_TILE))
    return out.reshape(b, n_lat, d)
```
